```python
import jax
import jax.numpy as jnp
from jax import lax
import numpy as np

D_MODEL = 1024
BATCH = 2
SEQ = 8192
DEPTH = 2

GRID_W = 64
CTX_LEN = 256
ROPE_THETA = 10000.0
RMS_EPS = 1e-6

GLA_HEADS = 4
GLA_DK = 64
GLA_DV = 128
GLA_RANK = 16
GLA_TAU = 16.0
GLA_CHUNK = 64

SWA_HEADS = 8
SWA_KV_HEADS = 2
SWA_HD = 64
WINDOW = 128
SWA_BLOCK = WINDOW

MLA_HEADS = 8
MLA_Q_RANK = 512
MLA_KV_RANK = 256
MLA_NOPE = 128
MLA_ROPE = 64
MLA_V = 128
MLA_BLOCK = 128
MLA_SCALE = (MLA_NOPE + MLA_ROPE) ** -0.5

N_EXPERTS = 64
TOP_K = 8
N_GROUPS = 8
TOPK_GROUPS = 4
D_EXPERT = 256
D_SHARED = 256
ROUTED_SCALE = 2.5
MOE_BLOCK = 128

EVEN_SPLITS = (GLA_HEADS * GLA_DK, GLA_HEADS * GLA_DK, GLA_HEADS * GLA_DV, GLA_HEADS * GLA_DV, GLA_RANK, GLA_RANK, SWA_HEADS * SWA_HD, SWA_KV_HEADS * SWA_HD, SWA_KV_HEADS * SWA_HD)
EVEN_IN = sum(EVEN_SPLITS)
EVEN_OUT = GLA_HEADS * GLA_DV + SWA_HEADS * SWA_HD
ODD_SPLITS = (MLA_Q_RANK, MLA_KV_RANK, MLA_ROPE)
ODD_IN = sum(ODD_SPLITS)
ODD_OUT = MLA_HEADS * MLA_V
N_EVEN = (DEPTH + 1) // 2
N_ODD = DEPTH // 2

kernel_name = "hybrid_gla_swa_mla_moe_dit"


def rmsnorm(x, g):
    xf = x.astype(jnp.float32)
    y = xf * lax.rsqrt(jnp.mean(xf * xf, axis=-1, keepdims=True) + RMS_EPS)
    return y.astype(x.dtype) * g


def modulate(x, g, shift, scale):
    return rmsnorm(x, g) * (1 + scale) + shift


def split_cols(z, sizes):
    cuts = [int(s) for s in np.cumsum(sizes)[:-1]]
    return jnp.split(z, cuts, axis=-1)


def axial_rope_angles(seq_len, rot_dim):
    rows = seq_len // GRID_W
    row = jnp.repeat(jnp.arange(rows, dtype=jnp.float32), GRID_W)
    col = jnp.tile(jnp.arange(GRID_W, dtype=jnp.float32), rows)
    axis_dim = rot_dim // 2
    inv_freq = ROPE_THETA ** (-jnp.arange(0, axis_dim, 2, dtype=jnp.float32) / axis_dim)
    return jnp.concatenate([row[:, None] * inv_freq, col[:, None] * inv_freq], axis=-1)


def apply_rope(x, ang):
    seq_len, half = ang.shape
    a = ang.reshape((1, seq_len) + (1,) * (x.ndim - 3) + (half,))
    cos, sin = jnp.cos(a), jnp.sin(a)
    xf = x.astype(jnp.float32)
    x1, x2 = xf[..., :half], xf[..., half:]
    return jnp.concatenate([x1 * cos - x2 * sin, x2 * cos + x1 * sin], axis=-1).astype(x.dtype)


def gla_log_decay(a_low, wa2, ba):
    return jax.nn.log_sigmoid((a_low @ wa2 + ba).astype(jnp.float32)) / GLA_TAU


def gla_chunk_scan(q, k, v, log_a, s0):
    b_, length, h, _ = q.shape
    dv = v.shape[-1]
    n = length // GLA_CHUNK

    def to_chunks(t):
        return t.astype(jnp.float32).reshape(b_, n, GLA_CHUNK, h, t.shape[-1]).transpose(1, 0, 3, 2, 4)

    incl = jnp.tril(jnp.ones((GLA_CHUNK, GLA_CHUNK), dtype=bool))[None, None, :, :, None]

    def step(state, inp):
        qc, kc, vc, ac = inp
        cum = jnp.cumsum(ac, axis=2)
        o = jnp.einsum('bhcd,bhde->bhce', qc * jnp.exp(cum), state)
        rel = jnp.exp(jnp.where(incl, cum[:, :, :, None, :] - cum[:, :, None, :, :], -jnp.inf))
        scores = jnp.einsum('bhijd,bhjd->bhij', qc[:, :, :, None, :] * rel, kc)
        o = o + jnp.einsum('bhij,bhje->bhie', scores, vc)
        last = cum[:, :, -1:, :]
        state = jnp.exp(last[:, :, 0, :])[..., None] * state + jnp.einsum('bhjd,bhje->bhde', kc * jnp.exp(last - cum), vc)
        return state, o

    s_fin, o = lax.scan(step, s0, (to_chunks(q), to_chunks(k), to_chunks(v), to_chunks(log_a)))
    o = o.transpose(1, 0, 3, 2, 4).reshape(b_, length, h, dv)
    return o.astype(v.dtype), s_fin


def gla_bidir(q, k, v, la_f, la_b, s_f, s_b):
    o_f, s_f = gla_chunk_scan(q, k, v, la_f, s_f)
    flip = lambda t: jnp.flip(t, axis=1)
    o_b, s_b = gla_chunk_scan(flip(q), flip(k), flip(v), flip(la_b), s_b)
    return o_f + flip(o_b), s_f, s_b


def gla_output(o, g, norm_g):
    b_, length = g.shape[:2]
    return rmsnorm(o, norm_g).reshape(b_, length, -1) * jax.nn.silu(g)


def swa_context(q, k, v, sink):
    b_, length = q.shape[:2]
    g = SWA_HEADS // SWA_KV_HEADS
    qg = q.reshape(b_, length, SWA_KV_HEADS, g, SWA_HD)
    s = jnp.einsum('bqhgd,bkhd->bhgqk', qg, k).astype(jnp.float32) * (SWA_HD ** -0.5)
    sink_b = jnp.broadcast_to(sink.reshape(1, SWA_KV_HEADS, g, 1, 1).astype(jnp.float32), s.shape[:-1] + (1,))
    p = jax.nn.softmax(jnp.concatenate([sink_b, s], axis=-1), axis=-1)[..., 1:].astype(v.dtype)
    o = jnp.einsum('bhgqk,bkhd->bqhgd', p, v)
    return o.reshape(b_, length, SWA_HEADS * SWA_HD)


def swa_latent(q, k, v, k_ctx, v_ctx, sink):
    b_, length = q.shape[:2]
    g = SWA_HEADS // SWA_KV_HEADS
    blk = SWA_BLOCK
    nb = length // blk
    scale = SWA_HD ** -0.5
    qb = q.reshape(b_, nb, blk, SWA_KV_HEADS, g, SWA_HD)

    def band(t):
        tp = jnp.pad(t, ((0, 0), (blk, blk), (0, 0), (0, 0))).reshape(b_, nb + 2, blk, SWA_KV_HEADS, SWA_HD)
        return jnp.concatenate([tp[:, :-2], tp[:, 1:-1], tp[:, 2:]], axis=2)

    kw, vw = band(k), band(v)
    s_win = jnp.einsum('bnqhgd,bnshd->bnhgqs', qb, kw).astype(jnp.float32) * scale
    s_ctx = jnp.einsum('bnqhgd,bchd->bnhgqc', qb, k_ctx).astype(jnp.float32) * scale
    qi = jnp.arange(blk)[:, None]
    si = jnp.arange(3 * blk)[None, :]
    kpos = jnp.arange(nb)[:, None, None] * blk - blk + si[None]
    valid = (jnp.abs(si - blk - qi) <= WINDOW)[None] & (kpos >= 0) & (kpos < length)
    s_win = jnp.where(valid[None, :, None, None], s_win, -jnp.inf)
    sink_b = jnp.broadcast_to(sink.reshape(1, 1, SWA_KV_HEADS, g, 1, 1).astype(jnp.float32), s_win.shape[:-1] + (1,))
    p = jax.nn.softmax(jnp.concatenate([sink_b, s_ctx, s_win], axis=-1), axis=-1).astype(v.dtype)
    n_ctx = k_ctx.shape[1]
    o = (jnp.einsum('bnhgqc,bchd->bnqhgd', p[..., 1:1 + n_ctx], v_ctx)
         + jnp.einsum('bnhgqs,bnshd->bnqhgd', p[..., 1 + n_ctx:], vw))
    return o.reshape(b_, length, SWA_HEADS * SWA_HD)


def even_project(u, w_in, wa2, ba):
    b_, length = u.shape[:2]
    q, k, v, g, a_f, a_b, sq, sk, sv = split_cols(u @ w_in, EVEN_SPLITS)
    heads = lambda t, h: t.reshape(b_, length, h, -1)
    la_f = gla_log_decay(a_f, wa2[0], ba[0])
    la_b = gla_log_decay(a_b, wa2[1], ba[1])
    return (heads(q * GLA_DK ** -0.5, GLA_HEADS), heads(k, GLA_HEADS), heads(v, GLA_HEADS), g,
            heads(la_f, GLA_HEADS), heads(la_b, GLA_HEADS),
            heads(sq, SWA_HEADS), heads(sk, SWA_KV_HEADS), heads(sv, SWA_KV_HEADS))


def even_mixer(ul, uc, w_in, wa2, ba, norm_g, sink, w_out, ang, need_ctx):
    cq, ck, cv, cg, claf, clab, csq, csk, csv = even_project(uc, w_in, wa2, ba)
    lq, lk, lv, lg, llaf, llab, lsq, lsk, lsv = even_project(ul, w_in, wa2, ba)
    zero = jnp.zeros((uc.shape[0], GLA_HEADS, GLA_DK, GLA_DV), jnp.float32)
    o_c, s_f, s_b = gla_bidir(cq, ck, cv, claf, clab, zero, zero)
    o_l, _, _ = gla_bidir(lq, lk, lv, llaf, llab, s_f, s_b)
    a_l = swa_latent(apply_rope(lsq, ang), apply_rope(lsk, ang), lsv, csk, csv, sink)
    y_l = jnp.concatenate([gla_output(o_l, lg, norm_g), a_l], axis=-1) @ w_out
    y_c = None
    if need_ctx:
        a_c = swa_context(csq, csk, csv, sink)
        y_c = jnp.concatenate([gla_output(o_c, cg, norm_g), a_c], axis=-1) @ w_out
    return y_l, y_c


def mla_project(u, w_in, q_norm, kv_norm, w_uq, w_ukv):
    b_, length = u.shape[:2]
    cq, ckv, k_rope = split_cols(u @ w_in, ODD_SPLITS)
    q = (rmsnorm(cq, q_norm) @ w_uq).reshape(b_, length, MLA_HEADS, MLA_NOPE + MLA_ROPE)
    kv = (rmsnorm(ckv, kv_norm) @ w_ukv).reshape(b_, length, MLA_HEADS, MLA_NOPE + MLA_V)
    return q[..., :MLA_NOPE], q[..., MLA_NOPE:], kv[..., :MLA_NOPE], kv[..., MLA_NOPE:], k_rope


def mla_scores(qn, qr, kn, kr):
    s = jnp.einsum('bqhd,bkhd->bhqk', qn, kn) + jnp.einsum('bqhr,bkr->bhqk', qr, kr)
    return s.astype(jnp.float32) * MLA_SCALE


def mla_context(qn, qr, kn, kr, v):
    p = jax.nn.softmax(mla_scores(qn, qr, kn, kr), axis=-1).astype(v.dtype)
    return jnp.einsum('bhqk,bkhd->bqhd', p, v)


def mla_latent(qn, qr, kn, kr, v, kn_c, kr_c, v_c):
    b_, length = qn.shape[:2]
    nb = length // MLA_BLOCK
    n_ctx = kn_c.shape[1]

    def blocks(t):
        return t.reshape((b_, nb, MLA_BLOCK) + t.shape[2:]).swapaxes(0, 1)

    def attend(args):
        qn_i, qr_i = args
        s = jnp.concatenate([mla_scores(qn_i, qr_i, kn_c, kr_c), mla_scores(qn_i, qr_i, kn, kr)], axis=-1)
        p = jax.nn.softmax(s, axis=-1).astype(v.dtype)
        return (jnp.einsum('bhqc,bchd->bqhd', p[..., :n_ctx], v_c)
                + jnp.einsum('bhqk,bkhd->bqhd', p[..., n_ctx:], v))

    o = lax.map(attend, (blocks(qn), blocks(qr)))
    return o.swapaxes(0, 1).reshape(b_, length, MLA_HEADS * MLA_V)


def odd_mixer(ul, uc, w_in, q_norm, kv_norm, w_uq, w_ukv, w_out, ang, need_ctx):
    b_, length = ul.shape[:2]
    lqn, lqr, lkn, lv, lkr = mla_project(ul, w_in, q_norm, kv_norm, w_uq, w_ukv)
    cqn, cqr, ckn, cv, ckr = mla_project(uc, w_in, q_norm, kv_norm, w_uq, w_ukv)
    o_l = mla_latent(lqn, apply_rope(lqr, ang), lkn, apply_rope(lkr, ang), lv, ckn, ckr, cv)
    y_l = o_l @ w_out
    y_c = None
    if need_ctx:
        y_c = mla_context(cqn, cqr, ckn, ckr, cv).reshape(uc.shape[0], uc.shape[1], -1) @ w_out
    return y_l, y_c


def route(h, w_router, b_corr):
    t = h.shape[0]
    scores = jax.nn.sigmoid(h.astype(jnp.float32) @ w_router.astype(jnp.float32))
    sel = scores + b_corr.astype(jnp.float32)
    grp = lax.top_k(sel.reshape(t, N_GROUPS, N_EXPERTS // N_GROUPS), 2)[0].sum(-1)
    _, gidx = lax.top_k(grp, TOPK_GROUPS)
    gmask = jax.nn.one_hot(gidx, N_GROUPS, dtype=jnp.float32).sum(1)
    emask = jnp.repeat(gmask, N_EXPERTS // N_GROUPS, axis=1) > 0
    _, idx = lax.top_k(jnp.where(emask, sel, -jnp.inf), TOP_K)
    w = jnp.take_along_axis(scores, idx, axis=1)
    w = w / jnp.sum(w, axis=-1, keepdims=True) * ROUTED_SCALE
    return idx, w


def moe(h, w_router, b_corr, w_g, w_u, w_d, ws_g, ws_u, ws_d):
    t, d = h.shape
    idx, wts = route(h, w_router, b_corr)
    n_assign = t * TOP_K
    flat_e = idx.reshape(-1)
    flat_t = jnp.repeat(jnp.arange(t, dtype=jnp.int32), TOP_K)
    flat_w = wts.reshape(-1)
    order = jnp.argsort(flat_e)
    se, st, sw = flat_e[order], flat_t[order], flat_w[order]
    counts = jnp.bincount(flat_e, length=N_EXPERTS)
    starts = jnp.cumsum(counts) - counts
    padded = (counts + MOE_BLOCK - 1) // MOE_BLOCK * MOE_BLOCK
    pad_end = jnp.cumsum(padded)
    pad_start = pad_end - padded
    dest = pad_start[se] + (jnp.arange(n_assign) - starts[se])
    n_blk = -(-n_assign // MOE_BLOCK) + N_EXPERTS
    buf_tok = jnp.zeros((n_blk * MOE_BLOCK,), jnp.int32).at[dest].set(st)
    buf_w = jnp.zeros((n_blk * MOE_BLOCK,), jnp.float32).at[dest].set(sw)
    blk_start = jnp.arange(n_blk, dtype=pad_end.dtype) * MOE_BLOCK
    blk_e = jnp.minimum(jnp.searchsorted(pad_end, blk_start, side='right'), N_EXPERTS - 1)

    def expert_block(args):
        tok, wt, e = args
        xb = h[tok]
        hid = jax.nn.silu(xb @ w_g[e]) * (xb @ w_u[e])
        return (hid @ w_d[e]) * wt[:, None].astype(h.dtype)

    yb = lax.map(expert_block, (buf_tok.reshape(n_blk, MOE_BLOCK), buf_w.reshape(n_blk, MOE_BLOCK), blk_e))
    routed = jnp.zeros_like(h).at[buf_tok].add(yb.reshape(-1, d))
    shared = (jax.nn.silu(h @ ws_g) * (h @ ws_u)) @ ws_d
    return shared + routed


def setup_inputs(seed: int = 0) -> dict:
    key = jax.random.key(seed)
    ks = jax.random.split(key, 30)
    d = D_MODEL

    def nrm(i, shape, scale):
        return jax.random.normal(ks[i], shape, jnp.float32) * scale

    return {
        'x': nrm(0, (BATCH, SEQ, d), 1.0),
        'c': nrm(1, (BATCH, d), 1.0),
        'ctx': nrm(2, (BATCH, CTX_LEN, d), 1.0),
        'c_ctx': nrm(3, (d,), 1.0),
        'w_mod': nrm(4, (DEPTH, d, 6 * d), 0.5 * d ** -0.5),
        'b_mod': nrm(5, (DEPTH, 6 * d), 0.02),
        'g_mix_pre': 1.0 + nrm(6, (DEPTH, d), 0.05),
        'g_mix_post': 1.0 + nrm(7, (DEPTH, d), 0.05),
        'g_ffn_pre': 1.0 + nrm(8, (DEPTH, d), 0.05),
        'g_ffn_post': 1.0 + nrm(9, (DEPTH, d), 0.05),
        'w_in_e': nrm(10, (N_EVEN, d, EVEN_IN), d ** -0.5),
        'gla_wa2': nrm(11, (N_EVEN, 2, GLA_RANK, GLA_HEADS * GLA_DK), GLA_RANK ** -0.5),
        'gla_ba': nrm(12, (N_EVEN, 2, GLA_HEADS * GLA_DK), 0.5),
        'gla_norm': 1.0 + nrm(13, (N_EVEN, GLA_DV), 0.05),
        'swa_sink': nrm(14, (N_EVEN, SWA_HEADS), 0.5),
        'w_out_e': nrm(15, (N_EVEN, EVEN_OUT, d), EVEN_OUT ** -0.5),
        'w_in_o': nrm(16, (N_ODD, d, ODD_IN), d ** -0.5),
        'mla_q_norm': 1.0 + nrm(17, (N_ODD, MLA_Q_RANK), 0.05),
        'mla_kv_norm': 1.0 + nrm(18, (N_ODD, MLA_KV_RANK), 0.05),
        'w_uq': nrm(19, (N_ODD, MLA_Q_RANK, MLA_HEADS * (MLA_NOPE + MLA_ROPE)), MLA_Q_RANK ** -0.5),
        'w_ukv': nrm(20, (N_ODD, MLA_KV_RANK, MLA_HEADS * (MLA_NOPE + MLA_V)), MLA_KV_RANK ** -0.5),
        'w_out_o': nrm(21, (N_ODD, ODD_OUT, d), ODD_OUT ** -0.5),
        'w_router': nrm(22, (DEPTH, d, N_EXPERTS), d ** -0.5),
        'b_router': nrm(23, (DEPTH, N_EXPERTS), 0.01),
        'w_exp_gate': nrm(24, (DEPTH, N_EXPERTS, d, D_EXPERT), d ** -0.5),
        'w_exp_up': nrm(25, (DEPTH, N_EXPERTS, d, D_EXPERT), d ** -0.5),
        'w_exp_down': nrm(26, (DEPTH, N_EXPERTS, D_EXPERT, d), D_EXPERT ** -0.5),
        'w_sh_gate': nrm(27, (DEPTH, d, D_SHARED), d ** -0.5),
        'w_sh_up': nrm(28, (DEPTH, d, D_SHARED), d ** -0.5),
        'w_sh_down': nrm(29, (DEPTH, D_SHARED, d), D_SHARED ** -0.5),
    }


def reference(x, c, ctx, c_ctx, w_mod, b_mod, g_mix_pre, g_mix_post, g_ffn_pre, g_ffn_post,
              w_in_e, gla_wa2, gla_ba, gla_norm, swa_sink, w_out_e,
              w_in_o, mla_q_norm, mla_kv_norm, w_uq, w_ukv, w_out_o,
              w_router, b_router, w_exp_gate, w_exp_up, w_exp_down, w_sh_gate, w_sh_up, w_sh_down):
    b_, seq_len, d = x.shape
    ang_swa = axial_rope_angles(seq_len, SWA_HD)
    ang_mla = axial_rope_angles(seq_len, MLA_ROPE)
    hl, hc = x, ctx
    for l in range(DEPTH):
        last = l == DEPTH - 1
        mod_l = (jax.nn.silu(c) @ w_mod[l] + b_mod[l])[:, None, :]
        mod_c = (jax.nn.silu(c_ctx) @ w_mod[l] + b_mod[l])[None, None, :]
        sh_a, sc_a, gt_a, sh_f, sc_f, gt_f = jnp.split(mod_l, 6, axis=-1)
        csh_a, csc_a, cgt_a, csh_f, csc_f, cgt_f = jnp.split(mod_c, 6, axis=-1)
        ul = modulate(hl, g_mix_pre[l], sh_a, sc_a)
        uc = modulate(hc, g_mix_pre[l], csh_a, csc_a)
        i = l // 2
        if l % 2 == 0:
            yl, yc = even_mixer(ul, uc, w_in_e[i], gla_wa2[i], gla_ba[i], gla_norm[i], swa_sink[i], w_out_e[i], ang_swa, not last)
        else:
            yl, yc = odd_mixer(ul, uc, w_in_o[i], mla_q_norm[i], mla_kv_norm[i], w_uq[i], w_ukv[i], w_out_o[i], ang_mla, not last)
        hl = hl + gt_a * rmsnorm(yl, g_mix_post[l])
        vl = modulate(hl, g_ffn_pre[l], sh_f, sc_f)
        moe_w = (w_router[l], b_router[l], w_exp_gate[l], w_exp_up[l], w_exp_down[l], w_sh_gate[l], w_sh_up[l], w_sh_down[l])
        if last:
            fl = moe(vl.reshape(-1, d), *moe_w).reshape(b_, seq_len, d)
        else:
            hc = hc + cgt_a * rmsnorm(yc, g_mix_post[l])
            vc = modulate(hc, g_ffn_pre[l], csh_f, csc_f)
            n_c = vc.shape[0] * vc.shape[1]
            f = moe(jnp.concatenate([vc.reshape(-1, d), vl.reshape(-1, d)], axis=0), *moe_w)
            hc = hc + cgt_f * rmsnorm(f[:n_c].reshape(hc.shape), g_ffn_post[l])
            fl = f[n_c:].reshape(b_, seq_len, d)
        hl = hl + gt_f * rmsnorm(fl, g_ffn_post[l])
    return hl
```

```python
import functools
import math

import jax
import jax.numpy as jnp
import numpy as np
from jax import lax
from jax.experimental import pallas as pl
from jax.experimental.pallas import tpu as pltpu

F32 = jnp.float32
MX = jnp.bfloat16
HI = lax.Precision.HIGHEST

GRID_W = 64
ROPE_THETA = 10000.0
RMS_EPS = 1e-6

GLA_HEADS, GLA_DK, GLA_DV, GLA_RANK, GLA_TAU = 4, 64, 128, 16, 16.0
GLA_QK = GLA_HEADS * GLA_DK
GLA_VW = GLA_HEADS * GLA_DV
GLA_CHUNK = 128

SWA_HEADS, SWA_KV_HEADS, SWA_HD, WINDOW = 8, 2, 64, 128
SWA_G = SWA_HEADS // SWA_KV_HEADS
SWA_W = SWA_HEADS * SWA_HD

MLA_HEADS, MLA_Q_RANK, MLA_KV_RANK = 8, 512, 256
MLA_NOPE, MLA_ROPE, MLA_V = 128, 64, 128
MLA_QK_PAD = 256
MLA_SCALE = (MLA_NOPE + MLA_ROPE) ** -0.5

N_EXPERTS, TOP_K, N_GROUPS, TOPK_GROUPS = 64, 8, 8, 4
GROUP_SIZE = N_EXPERTS // N_GROUPS
D_EXPERT = 256
ROUTED_SCALE = 2.5

ROW_TILE = 512
VMEM_LIMIT = 56 * 1024 * 1024


def _cparams(sem):
    return pltpu.CompilerParams(dimension_semantics=sem, vmem_limit_bytes=VMEM_LIMIT)


def _sigmoid(x):
    return 1.0 / (1.0 + jnp.exp(-x))


def _silu(x):
    return x * _sigmoid(x)


def _rms(x, g):
    ms = jnp.mean(x * x, axis=-1, keepdims=True)
    return x * lax.rsqrt(ms + RMS_EPS) * g


def _modulate(h, g, shift, scale):
    return _rms(h, g) * (1.0 + scale) + shift


def _dot(a, b):
    return jnp.dot(a, b, preferred_element_type=F32)


def _dot_t(a, b):
    return lax.dot_general(a, b, (((1,), (1,)), ((), ())), preferred_element_type=F32)


def _rope(x, cos, sin_signed):
    n = x.shape[-1]
    lane = lax.broadcasted_iota(jnp.int32, x.shape, 1)
    first = (lane % 64) < 32
    partner = jnp.where(first, pltpu.roll(x, n - 32, 1), pltpu.roll(x, 32, 1))
    return x * cos + partner * sin_signed


def _mod_body(c_ref, w_ref, b_ref, o_ref):
    s = _silu(c_ref[...])
    o_ref[0] = jnp.dot(s, w_ref[0], precision=HI, preferred_element_type=F32) + b_ref[0]


def _mod_vectors(cvec, w_mod, b_mod):
    depth, d, n = w_mod.shape
    tn = 1536
    return pl.pallas_call(
        _mod_body,
        out_shape=jax.ShapeDtypeStruct((depth, 8, n), F32),
        grid=(depth, n // tn),
        in_specs=[pl.BlockSpec((8, d), lambda l, j: (0, 0)),
                  pl.BlockSpec((1, d, tn), lambda l, j: (l, 0, j)),
                  pl.BlockSpec((1, 1, tn), lambda l, j: (l, 0, j))],
        out_specs=pl.BlockSpec((1, 8, tn), lambda l, j: (l, 0, j)),
        compiler_params=_cparams(("arbitrary", "arbitrary")),
    )(cvec, w_mod, b_mod.reshape(depth, 1, n))


def _mod_spec(seg_fn, which, d):
    return pl.BlockSpec((None, None, 1, d), lambda i, *_: (seg_fn(i), which, 0, 0))


def _full_spec(arr):
    nd = arr.ndim
    return pl.BlockSpec(arr.shape, lambda *_: (0,) * nd)


_EV_Q, _EV_K, _EV_V, _EV_G, _EV_A, _EV_SQ, _EV_SK, _EV_SV, _EV_END = (
    0, 256, 512, 1024, 1536, 1664, 2176, 2688, 3200)


def _even_in_body(h_ref, g_ref, sh_ref, sc_ref, w_ref, cos_ref, sin_ref,
                  q_ref, k_ref, v_ref, gg_ref, a_ref, sq_ref, sk_ref, sv_ref):
    u = _modulate(h_ref[...], g_ref[...], sh_ref[...], sc_ref[...]).astype(MX)
    q_ref[...] = _dot(u, w_ref[:, _EV_Q:_EV_K]) * (GLA_DK ** -0.5)
    k_ref[...] = _dot(u, w_ref[:, _EV_K:_EV_V])
    v_ref[...] = _dot(u, w_ref[:, _EV_V:_EV_G])
    gg_ref[...] = _dot(u, w_ref[:, _EV_G:_EV_A])
    a_ref[...] = _dot(u, w_ref[:, _EV_A:_EV_SQ])
    cos = cos_ref[...]
    sin = sin_ref[...]
    sq = _rope(_dot(u, w_ref[:, _EV_SQ:_EV_SK]), cos, sin)
    sq_ref[...] = (sq * (SWA_HD ** -0.5)).astype(sq_ref.dtype)
    sk_ref[...] = _rope(_dot(u, w_ref[:, _EV_SK:_EV_SV]), cos, sin).astype(sk_ref.dtype)
    sv_ref[...] = _dot(u, w_ref[:, _EV_SV:_EV_END]).astype(sv_ref.dtype)


def _even_in(h, g_pre, mod4, w_cat, cos_t, sin_t, seg_fn, pos_fn):
    t, d = h.shape
    tm = ROW_TILE
    row = lambda n: pl.BlockSpec((tm, n), lambda i: (i, 0))
    outs = [(GLA_QK, F32), (GLA_QK, F32), (GLA_VW, F32), (GLA_VW, F32), (128, F32),
            (SWA_W, MX), (SWA_W, MX), (SWA_W, MX)]
    return pl.pallas_call(
        _even_in_body,
        out_shape=[jax.ShapeDtypeStruct((t, n), dt) for n, dt in outs],
        grid=(t // tm,),
        in_specs=[row(d), _full_spec(g_pre), _mod_spec(seg_fn, 0, d), _mod_spec(seg_fn, 1, d),
                  _full_spec(w_cat),
                  pl.BlockSpec((tm, SWA_W), lambda i: (pos_fn(i), 0)),
                  pl.BlockSpec((tm, SWA_W), lambda i: (pos_fn(i), 0))],
        out_specs=[row(n) for n, _ in outs],
        compiler_params=_cparams(("parallel",)),
    )(h, g_pre, mod4, mod4, w_cat, cos_t, sin_t)


def _gla_consts(c, reverse):
    nlev = int(round(math.log2(c)))
    idx = np.arange(c)
    i, m = idx[:, None], idx[None, :]
    sizes = [c >> (l + 1) for l in range(nlev)]
    mats = [m <= i, m > i]
    for s in sizes:
        mats.append((m >= (i // s) * s) & (m <= i))
    for s in sizes:
        mats.append((m > i) & (m <= (i // s) * s + s - 1))
    a = np.stack(mats).astype(np.float32)
    masks = [np.eye(c, dtype=bool)]
    for s in sizes:
        bi, bj = i // s, m // s
        masks.append((bi % 2 == 1) & (bj == bi - 1))
    msk = np.stack(masks).astype(np.float32)
    if reverse:
        a = a[:, ::-1, ::-1]
        msk = msk[:, ::-1, ::-1]
    return (np.ascontiguousarray(a.reshape(-1, c)),
            np.ascontiguousarray(np.tile(msk, (1, 1, GLA_HEADS))))


def _split3(x):
    hi = x.astype(MX)
    r1 = x - hi.astype(F32)
    mid = r1.astype(MX)
    lo = (r1 - mid.astype(F32)).astype(MX)
    return hi, mid, lo


def _gla_direction(q, k, v, a, wa, ba, amat, lmask, hmask, vmask, bdmask, st_ref, d, last_row):
    c = q.shape[0]
    nlev = int(round(math.log2(c)))
    x = jnp.dot(a, wa, precision=HI, preferred_element_type=F32) + ba
    la = (jnp.minimum(x, 0.0) - jnp.log1p(jnp.exp(-jnp.abs(x)))) * (1.0 / GLA_TAU)
    hi, mid, lo = _split3(la)
    am = amat.astype(MX)
    ex = jnp.exp(_dot(am, hi) + _dot(am, mid) + _dot(am, lo))
    qd = (q * ex[0:c]).astype(MX)
    kd = (k * ex[c:2 * c]).astype(MX)
    st = st_ref[d]
    o = _dot_t(qd, st.astype(MX))
    scat = jnp.zeros((c, GLA_HEADS * c), F32)
    for lev in range(nlev + 1):
        if lev == 0:
            ql, kl = q, k
        else:
            ql = q * ex[(1 + lev) * c:(2 + lev) * c]
            kl = k * ex[(1 + nlev + lev) * c:(2 + nlev + lev) * c]
        kst = (jnp.concatenate([kl] * GLA_HEADS, axis=0) * hmask).astype(MX)
        scat = scat + _dot_t(ql.astype(MX), kst) * lmask[lev]
    vbd = (jnp.concatenate([v] * GLA_HEADS, axis=0) * vmask).astype(MX)
    o = o + _dot(scat.astype(MX), vbd)
    upd = lax.dot_general(v.astype(MX), kd, (((0,), (0,)), ((), ())), preferred_element_type=F32)
    st_ref[d] = st * ex[last_row:last_row + 1] + upd * bdmask
    return o


def _gla_body(qf_ref, kf_ref, vf_ref, af_ref, qb_ref, kb_ref, vb_ref, ab_ref,
              wa_ref, ba_ref, amf_ref, lmf_ref, amb_ref, lmb_ref, hm_ref, vm_ref, bd_ref,
              of_ref, ob_ref, st_ref):
    @pl.when(pl.program_id(1) == 0)
    def _():
        st_ref[...] = jnp.zeros_like(st_ref)

    c = qf_ref.shape[0]
    hm, vm, bd = hm_ref[...], vm_ref[...], bd_ref[...]
    of_ref[...] = _gla_direction(qf_ref[...], kf_ref[...], vf_ref[...], af_ref[...],
                                 wa_ref[0], ba_ref[0], amf_ref[...], lmf_ref, hm, vm, bd,
                                 st_ref, 0, c - 1)
    ob_ref[...] = _gla_direction(qb_ref[...], kb_ref[...], vb_ref[...], ab_ref[...],
                                 wa_ref[1], ba_ref[1], amb_ref[...], lmb_ref, hm, vm, bd,
                                 st_ref, 1, 0)


def _gla(q, k, v, a, wa, ba, batch, ctx_len, seq_len):
    t = q.shape[0]
    c = GLA_CHUNK
    nc, nl = ctx_len // c, seq_len // c
    amf, lmf = _gla_consts(c, False)
    amb, lmb = _gla_consts(c, True)
    r = np.arange(GLA_HEADS * c)[:, None] // c
    hm = (r == np.arange(GLA_QK)[None, :] // GLA_DK).astype(np.float32)
    vm = (r == np.arange(GLA_VW)[None, :] // GLA_DV).astype(np.float32)
    bd = (np.arange(GLA_VW)[:, None] // GLA_DV
          == np.arange(GLA_QK)[None, :] // GLA_DK).astype(np.float32)

    def fwd(b, s):
        return jnp.where(s < nc, nc * b + s, batch * nc + nl * b + (s - nc))

    def bwd(b, s):
        return jnp.where(s < nc, nc * b + (nc - 1 - s), batch * nc + nl * b + (nl - 1 - (s - nc)))

    def chunk(n, fn):
        return pl.BlockSpec((c, n), lambda b, s: (fn(b, s), 0))

    consts = [jnp.asarray(z) for z in (amf, lmf, amb, lmb, hm, vm, bd)]
    ins = [q, k, v, a, q, k, v, a, wa, ba] + consts
    specs = ([chunk(GLA_QK, fwd), chunk(GLA_QK, fwd), chunk(GLA_VW, fwd), chunk(128, fwd),
              chunk(GLA_QK, bwd), chunk(GLA_QK, bwd), chunk(GLA_VW, bwd), chunk(128, bwd)]
             + [_full_spec(z) for z in ins[8:]])
    return pl.pallas_call(
        _gla_body,
        out_shape=[jax.ShapeDtypeStruct((t, GLA_VW), F32)] * 2,
        grid=(batch, nc + nl),
        in_specs=specs,
        out_specs=[chunk(GLA_VW, fwd), chunk(GLA_VW, bwd)],
        scratch_shapes=[pltpu.VMEM((2, GLA_VW, GLA_QK), F32)],
        compiler_params=_cparams(("parallel", "arbitrary")),
    )(*ins)


def _swa_heads(q, kcat, vcat, valid, sink_ref, h):
    lane_head = lax.broadcasted_iota(jnp.int32, (1, SWA_G * SWA_HD), 1) // SWA_HD
    acc = jnp.zeros((q.shape[0], SWA_G * SWA_HD), F32)
    for g in range(SWA_G):
        hm = lane_head == g
        s = _dot_t(jnp.where(hm, q, jnp.zeros_like(q)), kcat)
        if valid is not None:
            s = jnp.where(valid, s, -jnp.inf)
        sk = sink_ref[h * SWA_G + g]
        m = jnp.maximum(jnp.max(s, axis=-1, keepdims=True), sk)
        p = jnp.exp(s - m)
        den = jnp.sum(p, axis=-1, keepdims=True) + jnp.exp(sk - m)
        og = _dot(p.astype(MX), jnp.where(hm, vcat, jnp.zeros_like(vcat)))
        acc = acc + og * (1.0 / den)
    return acc


def _swa_latent_body(sink_ref, q_ref, kc_ref, vc_ref, kp_ref, k0_ref, kn_ref,
                     vp_ref, v0_ref, vn_ref, o_ref, *, nblk):
    h, n = pl.program_id(1), pl.program_id(2)
    blk = q_ref.shape[0]
    nctx = kc_ref.shape[0]
    kcat = jnp.concatenate([kc_ref[...], kp_ref[...], k0_ref[...], kn_ref[...]], axis=0)
    vcat = jnp.concatenate([vc_ref[...], vp_ref[...], v0_ref[...], vn_ref[...]], axis=0)
    shape = (blk, nctx + 3 * blk)
    qi = lax.broadcasted_iota(jnp.int32, shape, 0)
    col = lax.broadcasted_iota(jnp.int32, shape, 1)
    si = col - nctx
    kpos = (n - 1) * blk + si
    valid = (col < nctx) | ((jnp.abs(si - blk - qi) <= WINDOW) & (kpos >= 0) & (kpos < nblk * blk))
    o_ref[...] = _swa_heads(q_ref[...], kcat, vcat, valid, sink_ref, h).astype(o_ref.dtype)


def _swa_ctx_body(sink_ref, q_ref, kc_ref, vc_ref, o_ref):
    h = pl.program_id(1)
    o_ref[...] = _swa_heads(q_ref[...], kc_ref[...], vc_ref[...], None, sink_ref, h).astype(o_ref.dtype)


def _swa(sq, skr, svr, sink, batch, ctx_len, seq_len):
    t = sq.shape[0]
    blk = WINDOW
    w = SWA_G * SWA_HD
    nb = seq_len // blk
    cb = ctx_len // blk
    lat0 = batch * cb
    smem = pl.BlockSpec(memory_space=pltpu.SMEM)
    ctx_kv = pl.BlockSpec((ctx_len, w), lambda b, h, n: (b, h))

    def win(off):
        return pl.BlockSpec((blk, w), lambda b, h, n: (lat0 + b * nb + jnp.clip(n + off, 0, nb - 1), h))

    lat = pl.pallas_call(
        functools.partial(_swa_latent_body, nblk=nb),
        out_shape=jax.ShapeDtypeStruct((batch * seq_len, SWA_W), MX),
        grid=(batch, SWA_KV_HEADS, nb),
        in_specs=[smem, pl.BlockSpec((blk, w), lambda b, h, n: (lat0 + b * nb + n, h)),
                  ctx_kv, ctx_kv, win(-1), win(0), win(1), win(-1), win(0), win(1)],
        out_specs=pl.BlockSpec((blk, w), lambda b, h, n: (b * nb + n, h)),
        compiler_params=_cparams(("parallel", "parallel", "arbitrary")),
    )(sink, sq, skr, svr, skr, skr, skr, svr, svr, svr)
    ctx = pl.pallas_call(
        _swa_ctx_body,
        out_shape=jax.ShapeDtypeStruct((batch * ctx_len, SWA_W), MX),
        grid=(batch, SWA_KV_HEADS, cb),
        in_specs=[smem, pl.BlockSpec((blk, w), lambda b, h, n: (b * cb + n, h)), ctx_kv, ctx_kv],
        out_specs=pl.BlockSpec((blk, w), lambda b, h, n: (b * cb + n, h)),
        compiler_params=_cparams(("parallel", "parallel", "arbitrary")),
    )(sink, sq, skr, svr)
    return jnp.concatenate([ctx, lat], axis=0)


def _residual(h, y, gpost, gate):
    return h + gate * _rms(y, gpost)


def _even_out_body(h_ref, of_ref, ob_ref, gg_ref, a_ref, gn_ref, w1_ref, w2_ref, gpost_ref,
                   gate_ref, o_ref):
    o = of_ref[...] + ob_ref[...]
    gn = gn_ref[...]
    parts = [_rms(o[:, j * GLA_DV:(j + 1) * GLA_DV], gn) for j in range(GLA_HEADS)]
    gl = jnp.concatenate(parts, axis=-1) * _silu(gg_ref[...])
    y = _dot(gl.astype(MX), w1_ref[...]) + _dot(a_ref[...], w2_ref[...])
    o_ref[...] = _residual(h_ref[...], y, gpost_ref[...], gate_ref[...])


def _even_out(h, o_f, o_b, gg, a_swa, gn, w1, w2, gpost, mod4, seg_fn):
    t, d = h.shape
    tm = ROW_TILE
    row = lambda n: pl.BlockSpec((tm, n), lambda i: (i, 0))
    return pl.pallas_call(
        _even_out_body,
        out_shape=jax.ShapeDtypeStruct((t, d), F32),
        grid=(t // tm,),
        in_specs=[row(d), row(GLA_VW), row(GLA_VW), row(GLA_VW), row(SWA_W), _full_spec(gn),
                  _full_spec(w1), _full_spec(w2), _full_spec(gpost), _mod_spec(seg_fn, 2, d)],
        out_specs=row(d),
        compiler_params=_cparams(("parallel",)),
    )(h, o_f, o_b, gg, a_swa, gn, w1, w2, gpost, mod4)


def _odd_out_body(h_ref, o_ref_in, w_ref, gpost_ref, gate_ref, o_ref):
    y = _dot(o_ref_in[...], w_ref[...])
    o_ref[...] = _residual(h_ref[...], y, gpost_ref[...], gate_ref[...])


def _odd_out(h, h_off, o, w, gpost, mod4, seg_fn):
    t, d = o.shape[0], h.shape[1]
    tm = ROW_TILE
    row = lambda n: pl.BlockSpec((tm, n), lambda i: (i, 0))
    return pl.pallas_call(
        _odd_out_body,
        out_shape=jax.ShapeDtypeStruct((t, d), F32),
        grid=(t // tm,),
        in_specs=[pl.BlockSpec((tm, d), lambda i: (i + h_off, 0)), row(o.shape[1]),
                  _full_spec(w), _full_spec(gpost), _mod_spec(seg_fn, 2, d)],
        out_specs=row(d),
        compiler_params=_cparams(("parallel",)),
    )(h, o, w, gpost, mod4)


def _odd_in_body(h_ref, g_ref, sh_ref, sc_ref, w_ref, qn_ref, kvn_ref, cos_ref, sin_ref,
                 cq_ref, ckv_ref, kr_ref):
    u = _modulate(h_ref[...], g_ref[...], sh_ref[...], sc_ref[...]).astype(MX)
    cq_ref[...] = _rms(_dot(u, w_ref[:, 0:MLA_Q_RANK]), qn_ref[...]).astype(cq_ref.dtype)
    c1 = MLA_Q_RANK + MLA_KV_RANK
    ckv_ref[...] = _rms(_dot(u, w_ref[:, MLA_Q_RANK:c1]), kvn_ref[...]).astype(ckv_ref.dtype)
    kr = _dot(u, w_ref[:, c1:c1 + 128])
    kr_ref[...] = _rope(kr, cos_ref[...], sin_ref[...]).astype(kr_ref.dtype)


def _odd_in(h, g_pre, mod4, w_cat, qn, kvn, cos_t, sin_t, seg_fn, pos_fn):
    t, d = h.shape
    tm = ROW_TILE
    row = lambda n: pl.BlockSpec((tm, n), lambda i: (i, 0))
    tab = pl.BlockSpec((tm, 128), lambda i: (pos_fn(i), 1))
    return pl.pallas_call(
        _odd_in_body,
        out_shape=[jax.ShapeDtypeStruct((t, MLA_Q_RANK), MX),
                   jax.ShapeDtypeStruct((t, MLA_KV_RANK), MX),
                   jax.ShapeDtypeStruct((t, 128), MX)],
        grid=(t // tm,),
        in_specs=[row(d), _full_spec(g_pre), _mod_spec(seg_fn, 0, d), _mod_spec(seg_fn, 1, d),
                  _full_spec(w_cat), _full_spec(qn), _full_spec(kvn), tab, tab],
        out_specs=[row(MLA_Q_RANK), row(MLA_KV_RANK), row(128)],
        compiler_params=_cparams(("parallel",)),
    )(h, g_pre, mod4, mod4, w_cat, qn, kvn, cos_t, sin_t)


def _q_up_body(cq_ref, w_ref, cos_ref, sin_ref, o_ref):
    z = _rope(_dot(cq_ref[...], w_ref[...]), cos_ref[...], sin_ref[...])
    o_ref[...] = (z * MLA_SCALE).astype(o_ref.dtype)


def _q_up(cq, row_off, nrows, w_pad, cos_t, sin_t, pos_fn):
    tm = ROW_TILE
    return pl.pallas_call(
        _q_up_body,
        out_shape=jax.ShapeDtypeStruct((nrows, MLA_HEADS * MLA_QK_PAD), MX),
        grid=(nrows // tm, MLA_HEADS),
        in_specs=[pl.BlockSpec((tm, MLA_Q_RANK), lambda i, h: (i + row_off, 0)),
                  pl.BlockSpec((MLA_Q_RANK, MLA_QK_PAD), lambda i, h: (0, h)),
                  pl.BlockSpec((tm, MLA_QK_PAD), lambda i, h: (pos_fn(i + row_off), 0)),
                  pl.BlockSpec((tm, MLA_QK_PAD), lambda i, h: (pos_fn(i + row_off), 0))],
        out_specs=pl.BlockSpec((tm, MLA_QK_PAD), lambda i, h: (i, h)),
        compiler_params=_cparams(("parallel", "arbitrary")),
    )(cq, w_pad, cos_t, sin_t)


def _kv_up_body(ckv_ref, kr_ref, w_ref, k_ref, v_ref):
    z = _dot(ckv_ref[...], w_ref[...])
    k_ref[...] = jnp.concatenate([z[:, :MLA_NOPE].astype(k_ref.dtype), kr_ref[...]], axis=-1)
    v_ref[...] = z[:, MLA_NOPE:].astype(v_ref.dtype)


def _kv_up(ckv, kr, row_off, nrows, tm, w_ukv):
    return pl.pallas_call(
        _kv_up_body,
        out_shape=[jax.ShapeDtypeStruct((nrows, MLA_HEADS * MLA_QK_PAD), MX),
                   jax.ShapeDtypeStruct((nrows, MLA_HEADS * MLA_V), MX)],
        grid=(nrows // tm, MLA_HEADS),
        in_specs=[pl.BlockSpec((tm, MLA_KV_RANK), lambda i, h: (i + row_off, 0)),
                  pl.BlockSpec((tm, 128), lambda i, h: (i + row_off, 0)),
                  pl.BlockSpec((MLA_KV_RANK, MLA_NOPE + MLA_V), lambda i, h: (0, h))],
        out_specs=[pl.BlockSpec((tm, MLA_QK_PAD), lambda i, h: (i, h)),
                   pl.BlockSpec((tm, MLA_V), lambda i, h: (i, h))],
        compiler_params=_cparams(("parallel", "arbitrary")),
    )(ckv, kr, w_ukv)


def _mla_attn_body(q_ref, kc_ref, vc_ref, k_ref, v_ref, o_ref, *, tk):
    q = q_ref[...]
    s = _dot_t(q, kc_ref[...])
    m = jnp.max(s, axis=-1, keepdims=True)
    p = jnp.exp(s - m)
    l = jnp.sum(p, axis=-1, keepdims=True)
    acc = _dot(p.astype(MX), vc_ref[...])

    def step(j, carry):
        m, l, acc = carry
        start = pl.multiple_of(j * tk, tk)
        s = _dot_t(q, k_ref[pl.ds(start, tk), :])
        mn = jnp.maximum(m, jnp.max(s, axis=-1, keepdims=True))
        alpha = jnp.exp(m - mn)
        p = jnp.exp(s - mn)
        l = alpha * l + jnp.sum(p, axis=-1, keepdims=True)
        acc = alpha * acc + _dot(p.astype(MX), v_ref[pl.ds(start, tk), :])
        return mn, l, acc

    m, l, acc = lax.fori_loop(0, k_ref.shape[0] // tk, step, (m, l, acc))
    o_ref[...] = (acc * (1.0 / l)).astype(o_ref.dtype)


def _mla_attn(q, k_ctx, v_ctx, k_lat, v_lat, batch, ctx_len, seq_len, tq=512, tk=512):
    nq = seq_len // tq
    return pl.pallas_call(
        functools.partial(_mla_attn_body, tk=tk),
        out_shape=jax.ShapeDtypeStruct((batch * seq_len, MLA_HEADS * MLA_V), MX),
        grid=(batch, MLA_HEADS, nq),
        in_specs=[pl.BlockSpec((tq, MLA_QK_PAD), lambda b, h, i: (b * nq + i, h)),
                  pl.BlockSpec((ctx_len, MLA_QK_PAD), lambda b, h, i: (b, h)),
                  pl.BlockSpec((ctx_len, MLA_V), lambda b, h, i: (b, h)),
                  pl.BlockSpec((seq_len, MLA_QK_PAD), lambda b, h, i: (b, h)),
                  pl.BlockSpec((seq_len, MLA_V), lambda b, h, i: (b, h))],
        out_specs=pl.BlockSpec((tq, MLA_V), lambda b, h, i: (b * nq + i, h)),
        compiler_params=_cparams(("parallel", "parallel", "arbitrary")),
    )(q, k_ctx, v_ctx, k_lat, v_lat)


def _route_weights(scores, sel):
    shape = sel.shape
    lane = lax.broadcasted_iota(jnp.int32, shape, 1)
    grp = lane // GROUP_SIZE
    neg = -jnp.inf
    gsum = jnp.zeros(shape, F32)
    gcols = []
    for gi in range(N_GROUPS):
        xg = jnp.where(grp == gi, sel, neg)
        m1 = jnp.max(xg, axis=-1, keepdims=True)
        i1 = jnp.min(jnp.where(xg == m1, lane, N_EXPERTS), axis=-1, keepdims=True)
        m2 = jnp.max(jnp.where(lane == i1, neg, xg), axis=-1, keepdims=True)
        gcols.append(m1 + m2)
        gsum = jnp.where(grp == gi, m1 + m2, gsum)
    beaten = jnp.zeros(shape, jnp.int32)
    for gi in range(N_GROUPS):
        ahead = (gcols[gi] > gsum) | ((gcols[gi] == gsum) & (gi < grp))
        beaten = beaten + ahead.astype(jnp.int32)
    cur = jnp.where(beaten < TOPK_GROUPS, sel, neg)
    chosen = jnp.zeros(shape, jnp.bool_)
    for _ in range(TOP_K):
        m = jnp.max(cur, axis=-1, keepdims=True)
        i = jnp.min(jnp.where(cur == m, lane, N_EXPERTS), axis=-1, keepdims=True)
        hit = lane == i
        chosen = chosen | hit
        cur = jnp.where(hit, neg, cur)
    w = jnp.where(chosen, scores, 0.0)
    return w / jnp.sum(w, axis=-1, keepdims=True) * ROUTED_SCALE


def _ffn_pre_body(h_ref, g_ref, sh_ref, sc_ref, wr_ref, br_ref, v_ref, wc_ref):
    vl = _modulate(h_ref[...], g_ref[...], sh_ref[...], sc_ref[...])
    v_ref[...] = vl.astype(v_ref.dtype)
    scores = _sigmoid(jnp.dot(vl, wr_ref[...], precision=HI, preferred_element_type=F32))
    wc_ref[...] = _route_weights(scores, scores + br_ref[...])


def _ffn_pre(h, g_pre, mod4, w_router, b_router, seg_fn):
    t, d = h.shape
    tm = ROW_TILE
    row = lambda n: pl.BlockSpec((tm, n), lambda i: (i, 0))
    return pl.pallas_call(
        _ffn_pre_body,
        out_shape=[jax.ShapeDtypeStruct((t, d), MX), jax.ShapeDtypeStruct((t, N_EXPERTS), F32)],
        grid=(t // tm,),
        in_specs=[row(d), _full_spec(g_pre), _mod_spec(seg_fn, 3, d), _mod_spec(seg_fn, 4, d),
                  _full_spec(w_router), _full_spec(b_router)],
        out_specs=[row(d), row(N_EXPERTS)],
        compiler_params=_cparams(("parallel",)),
    )(h, g_pre, mod4, mod4, w_router, b_router)


def _moe_body(x_ref, wc_ref, wg_ref, wu_ref, wd_ref, sg_ref, su_ref, sd_ref, h_ref, gpost_ref,
              gate_ref, o_ref, acc_ref, *, eb):
    e = pl.program_id(1)
    x = x_ref[...]

    @pl.when(e == 0)
    def _():
        hid = _silu(_dot(x, sg_ref[...])) * _dot(x, su_ref[...])
        acc_ref[...] = _dot(hid.astype(MX), sd_ref[...])

    wc = wc_ref[...]
    lane = lax.broadcasted_iota(jnp.int32, wc.shape, 1)
    for j in range(eb):
        wcol = jnp.sum(jnp.where(lane == e * eb + j, wc, 0.0), axis=-1, keepdims=True)
        hid = _silu(_dot(x, wg_ref[j])) * _dot(x, wu_ref[j]) * wcol
        acc_ref[...] += _dot(hid.astype(MX), wd_ref[j])

    @pl.when(e == pl.num_programs(1) - 1)
    def _():
        o_ref[...] = _residual(h_ref[...], acc_ref[...], gpost_ref[...], gate_ref[...])


def _moe(x, wc, wg, wu, wd, sg, su, sd, h, gpost, mod4, seg_fn, eb=4):
    t, d = x.shape
    tm = ROW_TILE
    row = lambda n: pl.BlockSpec((tm, n), lambda i, e: (i, 0))
    const = lambda arr: pl.BlockSpec(arr.shape, lambda i, e: (0,) * arr.ndim)
    return pl.pallas_call(
        functools.partial(_moe_body, eb=eb),
        out_shape=jax.ShapeDtypeStruct((t, d), F32),
        grid=(t // tm, N_EXPERTS // eb),
        in_specs=[row(d), row(N_EXPERTS),
                  pl.BlockSpec((eb, d, D_EXPERT), lambda i, e: (e, 0, 0)),
                  pl.BlockSpec((eb, d, D_EXPERT), lambda i, e: (e, 0, 0)),
                  pl.BlockSpec((eb, D_EXPERT, d), lambda i, e: (e, 0, 0)),
                  const(sg), const(su), const(sd), row(d), const(gpost),
                  pl.BlockSpec((None, None, 1, d), lambda i, e: (seg_fn(i), 5, 0, 0))],
        out_specs=row(d),
        scratch_shapes=[pltpu.VMEM((tm, d), F32)],
        compiler_params=_cparams(("parallel", "arbitrary")),
    )(x, wc, wg, wu, wd, sg, su, sd, h, gpost, mod4)


def _rope_tables(seq_len, n_ident):
    rows = seq_len // GRID_W
    row = jnp.repeat(jnp.arange(rows, dtype=F32), GRID_W)
    col = jnp.tile(jnp.arange(GRID_W, dtype=F32), rows)
    axis_dim = 32
    inv_freq = ROPE_THETA ** (-jnp.arange(0, axis_dim, 2, dtype=F32) / axis_dim)
    ang = jnp.concatenate([row[:, None] * inv_freq, col[:, None] * inv_freq], axis=-1)
    cos, sin = jnp.cos(ang), jnp.sin(ang)
    cos64 = jnp.concatenate([cos, cos], axis=-1)
    sin64 = jnp.concatenate([-sin, sin], axis=-1)
    cos64 = jnp.concatenate([jnp.ones((n_ident, 64), F32), cos64], axis=0)
    sin64 = jnp.concatenate([jnp.zeros((n_ident, 64), F32), sin64], axis=0)
    return cos64, sin64


def _even_weights(w_in):
    d = w_in.shape[0]
    cuts = np.cumsum([GLA_QK, GLA_QK, GLA_VW, GLA_VW, GLA_RANK, GLA_RANK, SWA_W,
                      SWA_KV_HEADS * SWA_HD])
    q, k, v, g, af, ab, sq, sk, sv = jnp.split(w_in, [int(c) for c in cuts], axis=1)
    a = jnp.concatenate([af, ab, jnp.zeros((d, 128 - 2 * GLA_RANK), w_in.dtype)], axis=1)

    def rep(wkv):
        return jnp.tile(wkv.reshape(d, SWA_KV_HEADS, 1, SWA_HD), (1, 1, SWA_G, 1)).reshape(d, SWA_W)

    return jnp.concatenate([q, k, v, g, a, sq, rep(sk), rep(sv)], axis=1).astype(MX)


def _layer_tail(h, h_lat_view, l, mod4, seg_fn, p):
    del h_lat_view
    x, wc = _ffn_pre(h, p['g_ffn_pre'][l][None], mod4, p['w_router'][l], p['b_router'][l][None], seg_fn)
    return _moe(x, wc, p['w_exp_gate'][l].astype(MX), p['w_exp_up'][l].astype(MX),
                p['w_exp_down'][l].astype(MX), p['w_sh_gate'][l].astype(MX),
                p['w_sh_up'][l].astype(MX), p['w_sh_down'][l].astype(MX),
                h, p['g_ffn_post'][l][None], mod4, seg_fn)


def kernel(x, c, ctx, c_ctx, w_mod, b_mod, g_mix_pre, g_mix_post, g_ffn_pre, g_ffn_post, w_in_e, gla_wa2, gla_ba, gla_norm, swa_sink, w_out_e, w_in_o, mla_q_norm, mla_kv_norm, w_uq, w_ukv, w_out_o, w_router, b_router, w_exp_gate, w_exp_up, w_exp_down, w_sh_gate, w_sh_up, w_sh_down):
    p = dict(g_ffn_pre=g_ffn_pre, g_ffn_post=g_ffn_post, w_router=w_router, b_router=b_router,
             w_exp_gate=w_exp_gate, w_exp_up=w_exp_up, w_exp_down=w_exp_down,
             w_sh_gate=w_sh_gate, w_sh_up=w_sh_up, w_sh_down=w_sh_down)
    batch, seq_len, d = x.shape
    ctx_len = ctx.shape[1]
    depth = w_mod.shape[0]
    assert depth == 2 and batch * ctx_len == ROW_TILE and seq_len % ROW_TILE == 0
    lat_blocks = seq_len // ROW_TILE

    cvec = jnp.concatenate([c, c_ctx[None], jnp.zeros((8 - batch - 1, d), F32)], axis=0)
    mod = _mod_vectors(cvec, w_mod, b_mod)
    h = jnp.concatenate([ctx.reshape(batch * ctx_len, d), x.reshape(batch * seq_len, d)], axis=0)

    seg_all = lambda i: jnp.where(i == 0, batch, (i - 1) // lat_blocks)
    seg_lat = lambda i: i // lat_blocks
    pos_all = lambda i: jnp.where(i == 0, 0, 1 + (i - 1) % lat_blocks)
    cos64, sin64 = _rope_tables(seq_len, ROW_TILE)

    mod4 = mod[0].reshape(8, 6, 1, d)
    q, k, v, gg, a, sq, skr, svr = _even_in(
        h, g_mix_pre[0][None], mod4, _even_weights(w_in_e[0]),
        jnp.tile(cos64, (1, SWA_HEADS)), jnp.tile(sin64, (1, SWA_HEADS)), seg_all, pos_all)
    wa = jnp.zeros((2, 128, GLA_QK), F32)
    wa = wa.at[0, :GLA_RANK].set(gla_wa2[0, 0]).at[1, GLA_RANK:2 * GLA_RANK].set(gla_wa2[0, 1])
    o_f, o_b = _gla(q, k, v, a, wa, gla_ba[0][:, None, :], batch, ctx_len, seq_len)
    a_swa = _swa(sq, skr, svr, swa_sink[0], batch, ctx_len, seq_len)
    w_out = w_out_e[0].astype(MX)
    h = _even_out(h, o_f, o_b, gg, a_swa, gla_norm[0][None], w_out[:GLA_VW], w_out[GLA_VW:],
                  g_mix_post[0][None], mod4, seg_all)
    h = _layer_tail(h, None, 0, mod4, seg_all, p)

    mod4 = mod[1].reshape(8, 6, 1, d)
    ones = jnp.ones_like(cos64)
    cos_h = jnp.concatenate([ones, ones, cos64, ones], axis=1)
    sin_h = jnp.concatenate([0 * ones, 0 * ones, sin64, 0 * ones], axis=1)
    w_in = jnp.concatenate([w_in_o[0], jnp.zeros((d, 128 - MLA_ROPE), F32)], axis=1).astype(MX)
    cq, ckv, kr = _odd_in(h, g_mix_pre[1][None], mod4, w_in, mla_q_norm[0][None],
                          mla_kv_norm[0][None], cos_h, sin_h, seg_all, pos_all)
    w_q = w_uq[0].reshape(MLA_Q_RANK, MLA_HEADS, MLA_NOPE + MLA_ROPE)
    w_q = jnp.pad(w_q, ((0, 0), (0, 0), (0, MLA_QK_PAD - MLA_NOPE - MLA_ROPE)))
    w_q = w_q.reshape(MLA_Q_RANK, MLA_HEADS * MLA_QK_PAD).astype(MX)
    n_lat = batch * seq_len
    qh = _q_up(cq, 1, n_lat, w_q, cos_h, sin_h, pos_all)
    w_kv = w_ukv[0].astype(MX)
    k_lat, v_lat = _kv_up(ckv, kr, 1, n_lat, ROW_TILE, w_kv)
    k_ctx, v_ctx = _kv_up(ckv, kr, 0, batch * ctx_len, ctx_len, w_kv)
    o = _mla_attn(qh, k_ctx, v_ctx, k_lat, v_lat, batch, ctx_len, seq_len)
    hl = _odd_out(h, 1, o, w_out_o[0].astype(MX), g_mix_post[1][None], mod4, seg_lat)
    hl = _layer_tail(hl, None, 1, mod4, seg_lat, p)
    return hl.reshape(batch, seq_len, d)
```

```python
import functools
import math

import jax
import jax.numpy as jnp
import numpy as np
from jax import lax
from jax.experimental import pallas as pl
from jax.experimental.pallas import tpu as pltpu

F32 = jnp.float32
MX = jnp.bfloat16
HI = lax.Precision.HIGHEST

GRID_W = 64
ROPE_THETA = 10000.0
RMS_EPS = 1e-6

GLA_HEADS, GLA_DK, GLA_DV, GLA_RANK, GLA_TAU = 4, 64, 128, 16, 16.0
GLA_QK = GLA_HEADS * GLA_DK
GLA_VW = GLA_HEADS * GLA_DV
GLA_CHUNK = 128

SWA_HEADS, SWA_KV_HEADS, SWA_HD, WINDOW = 8, 2, 64, 128
SWA_G = SWA_HEADS // SWA_KV_HEADS
SWA_W = SWA_HEADS * SWA_HD

MLA_HEADS, MLA_Q_RANK, MLA_KV_RANK = 8, 512, 256
MLA_NOPE, MLA_ROPE, MLA_V = 128, 64, 128
MLA_QK_PAD = 256
MLA_SCALE = (MLA_NOPE + MLA_ROPE) ** -0.5

N_EXPERTS, TOP_K, N_GROUPS, TOPK_GROUPS = 64, 8, 8, 4
GROUP_SIZE = N_EXPERTS // N_GROUPS
D_EXPERT = 256
ROUTED_SCALE = 2.5

ROW_TILE = 512
VMEM_LIMIT = 56 * 1024 * 1024


def _cparams(sem):
    return pltpu.CompilerParams(dimension_semantics=sem, vmem_limit_bytes=VMEM_LIMIT)


def _sigmoid(x):
    return 1.0 / (1.0 + jnp.exp(-x))


def _silu(x):
    return x * _sigmoid(x)


def _rms(x, g):
    ms = jnp.mean(x * x, axis=-1, keepdims=True)
    return x * lax.rsqrt(ms + RMS_EPS) * g


def _modulate(h, g, shift, scale):
    return _rms(h, g) * (1.0 + scale) + shift


def _dot(a, b):
    return jnp.dot(a, b, preferred_element_type=F32)


def _dot_t(a, b):
    return lax.dot_general(a, b, (((1,), (1,)), ((), ())), preferred_element_type=F32)


def _rope(x, cos, sin_signed):
    n = x.shape[-1]
    lane = lax.broadcasted_iota(jnp.int32, x.shape, 1)
    first = (lane % 64) < 32
    partner = jnp.where(first, pltpu.roll(x, n - 32, 1), pltpu.roll(x, 32, 1))
    return x * cos + partner * sin_signed


def _mod_body(c_ref, w_ref, b_ref, o_ref):
    s = _silu(c_ref[...])
    o_ref[0] = jnp.dot(s, w_ref[0], precision=HI, preferred_element_type=F32) + b_ref[0]


def _mod_vectors(cvec, w_mod, b_mod):
    depth, d, n = w_mod.shape
    tn = 1536
    return pl.pallas_call(
        _mod_body,
        name="mod_vectors",
        out_shape=jax.ShapeDtypeStruct((depth, 8, n), F32),
        grid=(depth, n // tn),
        in_specs=[pl.BlockSpec((8, d), lambda l, j: (0, 0)),
                  pl.BlockSpec((1, d, tn), lambda l, j: (l, 0, j)),
                  pl.BlockSpec((1, 1, tn), lambda l, j: (l, 0, j))],
        out_specs=pl.BlockSpec((1, 8, tn), lambda l, j: (l, 0, j)),
        compiler_params=_cparams(("arbitrary", "arbitrary")),
    )(cvec, w_mod, b_mod.reshape(depth, 1, n))


def _mod_spec(seg_fn, which, d):
    return pl.BlockSpec((None, None, 1, d), lambda i, *_: (seg_fn(i), which, 0, 0))


def _full_spec(arr):
    nd = arr.ndim
    return pl.BlockSpec(arr.shape, lambda *_: (0,) * nd)


_EV_Q, _EV_K, _EV_V, _EV_G, _EV_A, _EV_SQ, _EV_SK, _EV_SV, _EV_END = (
    0, 256, 512, 1024, 1536, 1664, 2176, 2688, 3200)


def _even_in_body(h_ref, g_ref, sh_ref, sc_ref, w_ref, cos_ref, sin_ref,
                  q_ref, k_ref, v_ref, gg_ref, a_ref, sq_ref, sk_ref, sv_ref):
    u = _modulate(h_ref[...], g_ref[...], sh_ref[...], sc_ref[...]).astype(MX)
    q_ref[...] = _dot(u, w_ref[:, _EV_Q:_EV_K]) * (GLA_DK ** -0.5)
    k_ref[...] = _dot(u, w_ref[:, _EV_K:_EV_V])
    v_ref[...] = _dot(u, w_ref[:, _EV_V:_EV_G])
    gg_ref[...] = _dot(u, w_ref[:, _EV_G:_EV_A])
    a_ref[...] = _dot(u, w_ref[:, _EV_A:_EV_SQ])
    cos = cos_ref[...]
    sin = sin_ref[...]
    sq = _rope(_dot(u, w_ref[:, _EV_SQ:_EV_SK]), cos, sin)
    sq_ref[...] = (sq * (SWA_HD ** -0.5)).astype(sq_ref.dtype)
    sk_ref[...] = _rope(_dot(u, w_ref[:, _EV_SK:_EV_SV]), cos, sin).astype(sk_ref.dtype)
    sv_ref[...] = _dot(u, w_ref[:, _EV_SV:_EV_END]).astype(sv_ref.dtype)


def _even_in(h, g_pre, mod4, w_cat, cos_t, sin_t, seg_fn, pos_fn):
    t, d = h.shape
    tm = ROW_TILE
    row = lambda n: pl.BlockSpec((tm, n), lambda i: (i, 0))
    outs = [(GLA_QK, F32), (GLA_QK, F32), (GLA_VW, F32), (GLA_VW, F32), (128, F32),
            (SWA_W, MX), (SWA_W, MX), (SWA_W, MX)]
    return pl.pallas_call(
        _even_in_body,
        name="even_in",
        out_shape=[jax.ShapeDtypeStruct((t, n), dt) for n, dt in outs],
        grid=(t // tm,),
        in_specs=[row(d), _full_spec(g_pre), _mod_spec(seg_fn, 0, d), _mod_spec(seg_fn, 1, d),
                  _full_spec(w_cat),
                  pl.BlockSpec((tm, SWA_W), lambda i: (pos_fn(i), 0)),
                  pl.BlockSpec((tm, SWA_W), lambda i: (pos_fn(i), 0))],
        out_specs=[row(n) for n, _ in outs],
        compiler_params=_cparams(("parallel",)),
    )(h, g_pre, mod4, mod4, w_cat, cos_t, sin_t)


def _gla_consts(c, reverse):
    nlev = int(round(math.log2(c)))
    idx = np.arange(c)
    i, m = idx[:, None], idx[None, :]
    sizes = [c >> (l + 1) for l in range(nlev)]
    mats = [m <= i, m > i]
    for s in sizes:
        mats.append((m >= (i // s) * s) & (m <= i))
    for s in sizes:
        mats.append((m > i) & (m <= (i // s) * s + s - 1))
    a = np.stack(mats).astype(np.float32)
    masks = [np.eye(c, dtype=bool)]
    for s in sizes:
        bi, bj = i // s, m // s
        masks.append((bi % 2 == 1) & (bj == bi - 1))
    msk = np.stack(masks).astype(np.float32)
    if reverse:
        a = a[:, ::-1, ::-1]
        msk = msk[:, ::-1, ::-1]
    return (np.ascontiguousarray(a.reshape(-1, c)),
            np.ascontiguousarray(np.tile(msk, (1, 1, GLA_HEADS))))


def _split3(x):
    hi = x.astype(MX)
    r1 = x - hi.astype(F32)
    mid = r1.astype(MX)
    lo = (r1 - mid.astype(F32)).astype(MX)
    return hi, mid, lo


def _gla_direction(q, k, v, a, wa, ba, amat, lmask, hmask, vmask, bdmask, st_ref, d, last_row):
    c = q.shape[0]
    nlev = int(round(math.log2(c)))
    x = jnp.dot(a, wa, precision=HI, preferred_element_type=F32) + ba
    la = (jnp.minimum(x, 0.0) - jnp.log1p(jnp.exp(-jnp.abs(x)))) * (1.0 / GLA_TAU)
    hi, mid, lo = _split3(la)
    am = amat.astype(MX)
    ex = jnp.exp(_dot(am, hi) + _dot(am, mid) + _dot(am, lo))
    qd = (q * ex[0:c]).astype(MX)
    kd = (k * ex[c:2 * c]).astype(MX)
    st = st_ref[d]
    o = _dot_t(qd, st.astype(MX))
    scat = jnp.zeros((c, GLA_HEADS * c), F32)
    for lev in range(nlev + 1):
        if lev == 0:
            ql, kl = q, k
        else:
            ql = q * ex[(1 + lev) * c:(2 + lev) * c]
            kl = k * ex[(1 + nlev + lev) * c:(2 + nlev + lev) * c]
        kst = (jnp.concatenate([kl] * GLA_HEADS, axis=0) * hmask).astype(MX)
        scat = scat + _dot_t(ql.astype(MX), kst) * lmask[lev]
    vbd = (jnp.concatenate([v] * GLA_HEADS, axis=0) * vmask).astype(MX)
    o = o + _dot(scat.astype(MX), vbd)
    upd = lax.dot_general(v.astype(MX), kd, (((0,), (0,)), ((), ())), preferred_element_type=F32)
    st_ref[d] = st * ex[last_row:last_row + 1] + upd * bdmask
    return o


def _gla_body(qf_ref, kf_ref, vf_ref, af_ref, qb_ref, kb_ref, vb_ref, ab_ref,
              wa_ref, ba_ref, amf_ref, lmf_ref, amb_ref, lmb_ref, hm_ref, vm_ref, bd_ref,
              of_ref, ob_ref, st_ref):
    @pl.when(pl.program_id(1) == 0)
    def _():
        st_ref[...] = jnp.zeros_like(st_ref)

    c = qf_ref.shape[0]
    hm, vm, bd = hm_ref[...], vm_ref[...], bd_ref[...]
    of_ref[...] = _gla_direction(qf_ref[...], kf_ref[...], vf_ref[...], af_ref[...],
                                 wa_ref[0], ba_ref[0], amf_ref[...], lmf_ref, hm, vm, bd,
                                 st_ref, 0, c - 1)
    ob_ref[...] = _gla_direction(qb_ref[...], kb_ref[...], vb_ref[...], ab_ref[...],
                                 wa_ref[1], ba_ref[1], amb_ref[...], lmb_ref, hm, vm, bd,
                                 st_ref, 1, 0)


def _gla(q, k, v, a, wa, ba, batch, ctx_len, seq_len):
    t = q.shape[0]
    c = GLA_CHUNK
    nc, nl = ctx_len // c, seq_len // c
    amf, lmf = _gla_consts(c, False)
    amb, lmb = _gla_consts(c, True)
    r = np.arange(GLA_HEADS * c)[:, None] // c
    hm = (r == np.arange(GLA_QK)[None, :] // GLA_DK).astype(np.float32)
    vm = (r == np.arange(GLA_VW)[None, :] // GLA_DV).astype(np.float32)
    bd = (np.arange(GLA_VW)[:, None] // GLA_DV
          == np.arange(GLA_QK)[None, :] // GLA_DK).astype(np.float32)

    def fwd(b, s):
        return jnp.where(s < nc, nc * b + s, batch * nc + nl * b + (s - nc))

    def bwd(b, s):
        return jnp.where(s < nc, nc * b + (nc - 1 - s), batch * nc + nl * b + (nl - 1 - (s - nc)))

    def chunk(n, fn):
        return pl.BlockSpec((c, n), lambda b, s: (fn(b, s), 0))

    consts = [jnp.asarray(z) for z in (amf, lmf, amb, lmb, hm, vm, bd)]
    ins = [q, k, v, a, q, k, v, a, wa, ba] + consts
    specs = ([chunk(GLA_QK, fwd), chunk(GLA_QK, fwd), chunk(GLA_VW, fwd), chunk(128, fwd),
              chunk(GLA_QK, bwd), chunk(GLA_QK, bwd), chunk(GLA_VW, bwd), chunk(128, bwd)]
             + [_full_spec(z) for z in ins[8:]])
    return pl.pallas_call(
        _gla_body,
        name="gla_scan",
        out_shape=[jax.ShapeDtypeStruct((t, GLA_VW), F32)] * 2,
        grid=(batch, nc + nl),
        in_specs=specs,
        out_specs=[chunk(GLA_VW, fwd), chunk(GLA_VW, bwd)],
        scratch_shapes=[pltpu.VMEM((2, GLA_VW, GLA_QK), F32)],
        compiler_params=_cparams(("parallel", "arbitrary")),
    )(*ins)


def _swa_heads(q, kcat, vcat, valid, sink_ref, h):
    lane_head = lax.broadcasted_iota(jnp.int32, (1, SWA_G * SWA_HD), 1) // SWA_HD
    acc = jnp.zeros((q.shape[0], SWA_G * SWA_HD), F32)
    for g in range(SWA_G):
        hm = lane_head == g
        s = _dot_t(jnp.where(hm, q, jnp.zeros_like(q)), kcat)
        if valid is not None:
            s = jnp.where(valid, s, -jnp.inf)
        sk = sink_ref[h * SWA_G + g]
        m = jnp.maximum(jnp.max(s, axis=-1, keepdims=True), sk)
        p = jnp.exp(s - m)
        den = jnp.sum(p, axis=-1, keepdims=True) + jnp.exp(sk - m)
        og = _dot(p.astype(MX), jnp.where(hm, vcat, jnp.zeros_like(vcat)))
        acc = acc + og * (1.0 / den)
    return acc


def _swa_latent_body(sink_ref, q_ref, kc_ref, vc_ref, kp_ref, k0_ref, kn_ref,
                     vp_ref, v0_ref, vn_ref, o_ref, *, nblk):
    h, n = pl.program_id(1), pl.program_id(2)
    blk = q_ref.shape[0]
    nctx = kc_ref.shape[0]
    kcat = jnp.concatenate([kc_ref[...], kp_ref[...], k0_ref[...], kn_ref[...]], axis=0)
    vcat = jnp.concatenate([vc_ref[...], vp_ref[...], v0_ref[...], vn_ref[...]], axis=0)
    shape = (blk, nctx + 3 * blk)
    qi = lax.broadcasted_iota(jnp.int32, shape, 0)
    col = lax.broadcasted_iota(jnp.int32, shape, 1)
    si = col - nctx
    kpos = (n - 1) * blk + si
    valid = (col < nctx) | ((jnp.abs(si - blk - qi) <= WINDOW) & (kpos >= 0) & (kpos < nblk * blk))
    o_ref[...] = _swa_heads(q_ref[...], kcat, vcat, valid, sink_ref, h).astype(o_ref.dtype)


def _swa_ctx_body(sink_ref, q_ref, kc_ref, vc_ref, o_ref):
    h = pl.program_id(1)
    o_ref[...] = _swa_heads(q_ref[...], kc_ref[...], vc_ref[...], None, sink_ref, h).astype(o_ref.dtype)


def _swa(sq, skr, svr, sink, batch, ctx_len, seq_len):
    t = sq.shape[0]
    blk = WINDOW
    w = SWA_G * SWA_HD
    nb = seq_len // blk
    cb = ctx_len // blk
    lat0 = batch * cb
    smem = pl.BlockSpec(memory_space=pltpu.SMEM)
    ctx_kv = pl.BlockSpec((ctx_len, w), lambda b, h, n: (b, h))

    def win(off):
        return pl.BlockSpec((blk, w), lambda b, h, n: (lat0 + b * nb + jnp.clip(n + off, 0, nb - 1), h))

    lat = pl.pallas_call(
        functools.partial(_swa_latent_body, nblk=nb),
        name="swa_latent",
        out_shape=jax.ShapeDtypeStruct((batch * seq_len, SWA_W), MX),
        grid=(batch, SWA_KV_HEADS, nb),
        in_specs=[smem, pl.BlockSpec((blk, w), lambda b, h, n: (lat0 + b * nb + n, h)),
                  ctx_kv, ctx_kv, win(-1), win(0), win(1), win(-1), win(0), win(1)],
        out_specs=pl.BlockSpec((blk, w), lambda b, h, n: (b * nb + n, h)),
        compiler_params=_cparams(("parallel", "parallel", "arbitrary")),
    )(sink, sq, skr, svr, skr, skr, skr, svr, svr, svr)
    ctx = pl.pallas_call(
        _swa_ctx_body,
        name="swa_ctx",
        out_shape=jax.ShapeDtypeStruct((batch * ctx_len, SWA_W), MX),
        grid=(batch, SWA_KV_HEADS, cb),
        in_specs=[smem, pl.BlockSpec((blk, w), lambda b, h, n: (b * cb + n, h)), ctx_kv, ctx_kv],
        out_specs=pl.BlockSpec((blk, w), lambda b, h, n: (b * cb + n, h)),
        compiler_params=_cparams(("parallel", "parallel", "arbitrary")),
    )(sink, sq, skr, svr)
    return jnp.concatenate([ctx, lat], axis=0)


def _residual(h, y, gpost, gate):
    return h + gate * _rms(y, gpost)


def _even_out_body(h_ref, of_ref, ob_ref, gg_ref, a_ref, gn_ref, w1_ref, w2_ref, gpost_ref,
                   gate_ref, o_ref):
    o = of_ref[...] + ob_ref[...]
    gn = gn_ref[...]
    parts = [_rms(o[:, j * GLA_DV:(j + 1) * GLA_DV], gn) for j in range(GLA_HEADS)]
    gl = jnp.concatenate(parts, axis=-1) * _silu(gg_ref[...])
    y = _dot(gl.astype(MX), w1_ref[...]) + _dot(a_ref[...], w2_ref[...])
    o_ref[...] = _residual(h_ref[...], y, gpost_ref[...], gate_ref[...])


def _even_out(h, o_f, o_b, gg, a_swa, gn, w1, w2, gpost, mod4, seg_fn):
    t, d = h.shape
    tm = ROW_TILE
    row = lambda n: pl.BlockSpec((tm, n), lambda i: (i, 0))
    return pl.pallas_call(
        _even_out_body,
        name="even_out",
        out_shape=jax.ShapeDtypeStruct((t, d), F32),
        grid=(t // tm,),
        in_specs=[row(d), row(GLA_VW), row(GLA_VW), row(GLA_VW), row(SWA_W), _full_spec(gn),
                  _full_spec(w1), _full_spec(w2), _full_spec(gpost), _mod_spec(seg_fn, 2, d)],
        out_specs=row(d),
        compiler_params=_cparams(("parallel",)),
    )(h, o_f, o_b, gg, a_swa, gn, w1, w2, gpost, mod4)


def _odd_out_body(h_ref, o_ref_in, w_ref, gpost_ref, gate_ref, o_ref):
    y = _dot(o_ref_in[...], w_ref[...])
    o_ref[...] = _residual(h_ref[...], y, gpost_ref[...], gate_ref[...])


def _odd_out(h, h_off, o, w, gpost, mod4, seg_fn):
    t, d = o.shape[0], h.shape[1]
    tm = ROW_TILE
    row = lambda n: pl.BlockSpec((tm, n), lambda i: (i, 0))
    return pl.pallas_call(
        _odd_out_body,
        name="odd_out",
        out_shape=jax.ShapeDtypeStruct((t, d), F32),
        grid=(t // tm,),
        in_specs=[pl.BlockSpec((tm, d), lambda i: (i + h_off, 0)), row(o.shape[1]),
                  _full_spec(w), _full_spec(gpost), _mod_spec(seg_fn, 2, d)],
        out_specs=row(d),
        compiler_params=_cparams(("parallel",)),
    )(h, o, w, gpost, mod4)


def _odd_in_body(h_ref, g_ref, sh_ref, sc_ref, w_ref, qn_ref, kvn_ref, cos_ref, sin_ref,
                 cq_ref, ckv_ref, kr_ref):
    u = _modulate(h_ref[...], g_ref[...], sh_ref[...], sc_ref[...]).astype(MX)
    cq_ref[...] = _rms(_dot(u, w_ref[:, 0:MLA_Q_RANK]), qn_ref[...]).astype(cq_ref.dtype)
    c1 = MLA_Q_RANK + MLA_KV_RANK
    ckv_ref[...] = _rms(_dot(u, w_ref[:, MLA_Q_RANK:c1]), kvn_ref[...]).astype(ckv_ref.dtype)
    kr = _dot(u, w_ref[:, c1:c1 + 128])
    kr_ref[...] = _rope(kr, cos_ref[...], sin_ref[...]).astype(kr_ref.dtype)


def _odd_in(h, g_pre, mod4, w_cat, qn, kvn, cos_t, sin_t, seg_fn, pos_fn):
    t, d = h.shape
    tm = ROW_TILE
    row = lambda n: pl.BlockSpec((tm, n), lambda i: (i, 0))
    tab = pl.BlockSpec((tm, 128), lambda i: (pos_fn(i), 1))
    return pl.pallas_call(
        _odd_in_body,
        name="odd_in",
        out_shape=[jax.ShapeDtypeStruct((t, MLA_Q_RANK), MX),
                   jax.ShapeDtypeStruct((t, MLA_KV_RANK), MX),
                   jax.ShapeDtypeStruct((t, 128), MX)],
        grid=(t // tm,),
        in_specs=[row(d), _full_spec(g_pre), _mod_spec(seg_fn, 0, d), _mod_spec(seg_fn, 1, d),
                  _full_spec(w_cat), _full_spec(qn), _full_spec(kvn), tab, tab],
        out_specs=[row(MLA_Q_RANK), row(MLA_KV_RANK), row(128)],
        compiler_params=_cparams(("parallel",)),
    )(h, g_pre, mod4, mod4, w_cat, qn, kvn, cos_t, sin_t)


def _q_up_body(cq_ref, w_ref, cos_ref, sin_ref, o_ref):
    z = _rope(_dot(cq_ref[...], w_ref[...]), cos_ref[...], sin_ref[...])
    o_ref[...] = (z * (MLA_SCALE * math.log2(math.e))).astype(o_ref.dtype)


def _q_up(cq, row_off, nrows, w_pad, cos_t, sin_t, pos_fn):
    tm = ROW_TILE
    return pl.pallas_call(
        _q_up_body,
        name="mla_q_up",
        out_shape=jax.ShapeDtypeStruct((nrows, MLA_HEADS * MLA_QK_PAD), MX),
        grid=(nrows // tm, MLA_HEADS),
        in_specs=[pl.BlockSpec((tm, MLA_Q_RANK), lambda i, h: (i + row_off, 0)),
                  pl.BlockSpec((MLA_Q_RANK, MLA_QK_PAD), lambda i, h: (0, h)),
                  pl.BlockSpec((tm, MLA_QK_PAD), lambda i, h: (pos_fn(i + row_off), 0)),
                  pl.BlockSpec((tm, MLA_QK_PAD), lambda i, h: (pos_fn(i + row_off), 0))],
        out_specs=pl.BlockSpec((tm, MLA_QK_PAD), lambda i, h: (i, h)),
        compiler_params=_cparams(("parallel", "arbitrary")),
    )(cq, w_pad, cos_t, sin_t)


def _kv_up_body(ckv_ref, kr_ref, w_ref, k_ref, v_ref):
    z = _dot(ckv_ref[...], w_ref[...])
    k_ref[...] = jnp.concatenate([z[:, :MLA_NOPE].astype(k_ref.dtype), kr_ref[...]], axis=-1)
    v_ref[...] = z[:, MLA_NOPE:].astype(v_ref.dtype)


def _kv_up(ckv, kr, row_off, nrows, tm, w_ukv):
    return pl.pallas_call(
        _kv_up_body,
        name="mla_kv_up",
        out_shape=[jax.ShapeDtypeStruct((nrows, MLA_HEADS * MLA_QK_PAD), MX),
                   jax.ShapeDtypeStruct((nrows, MLA_HEADS * MLA_V), MX)],
        grid=(nrows // tm, MLA_HEADS),
        in_specs=[pl.BlockSpec((tm, MLA_KV_RANK), lambda i, h: (i + row_off, 0)),
                  pl.BlockSpec((tm, 128), lambda i, h: (i + row_off, 0)),
                  pl.BlockSpec((MLA_KV_RANK, MLA_NOPE + MLA_V), lambda i, h: (0, h))],
        out_specs=[pl.BlockSpec((tm, MLA_QK_PAD), lambda i, h: (i, h)),
                   pl.BlockSpec((tm, MLA_V), lambda i, h: (i, h))],
        compiler_params=_cparams(("parallel", "arbitrary")),
    )(ckv, kr, w_ukv)


def _mla_attn_body(q_ref, kc_ref, vc_ref, k_ref, v_ref, o_ref, *, tk, unroll):
    q = q_ref[...]
    s = _dot_t(q, kc_ref[...])
    m = jnp.max(s, axis=-1, keepdims=True)
    p = jnp.exp2(s - m)
    l = jnp.sum(p, axis=-1, keepdims=True)
    acc = _dot(p.astype(MX), vc_ref[...])

    def step(j, carry):
        m, l, acc = carry
        start = pl.multiple_of(j * tk, tk)
        s = _dot_t(q, k_ref[pl.ds(start, tk), :])
        mn = jnp.maximum(m, jnp.max(s, axis=-1, keepdims=True))
        alpha = jnp.exp2(m - mn)
        p = jnp.exp2(s - mn)
        l = alpha * l + jnp.sum(p, axis=-1, keepdims=True)
        acc = alpha * acc + _dot(p.astype(MX), v_ref[pl.ds(start, tk), :])
        return mn, l, acc

    m, l, acc = lax.fori_loop(0, k_ref.shape[0] // tk, step, (m, l, acc), unroll=unroll)
    o_ref[...] = (acc * (1.0 / l)).astype(o_ref.dtype)


def _mla_attn(q, k_ctx, v_ctx, k_lat, v_lat, batch, ctx_len, seq_len, tq=1024, tk=512, unroll=4):
    nq = seq_len // tq
    return pl.pallas_call(
        functools.partial(_mla_attn_body, tk=tk, unroll=unroll),
        name="mla_attn",
        out_shape=jax.ShapeDtypeStruct((batch * seq_len, MLA_HEADS * MLA_V), MX),
        grid=(batch, MLA_HEADS, nq),
        in_specs=[pl.BlockSpec((tq, MLA_QK_PAD), lambda b, h, i: (b * nq + i, h)),
                  pl.BlockSpec((ctx_len, MLA_QK_PAD), lambda b, h, i: (b, h)),
                  pl.BlockSpec((ctx_len, MLA_V), lambda b, h, i: (b, h)),
                  pl.BlockSpec((seq_len, MLA_QK_PAD), lambda b, h, i: (b, h)),
                  pl.BlockSpec((seq_len, MLA_V), lambda b, h, i: (b, h))],
        out_specs=pl.BlockSpec((tq, MLA_V), lambda b, h, i: (b * nq + i, h)),
        compiler_params=_cparams(("parallel", "parallel", "arbitrary")),
    )(q, k_ctx, v_ctx, k_lat, v_lat)


def _route_weights(scores, sel):
    shape = sel.shape
    lane = lax.broadcasted_iota(jnp.int32, shape, 1)
    grp = lane // GROUP_SIZE
    neg = -jnp.inf
    gsum = jnp.zeros(shape, F32)
    gcols = []
    for gi in range(N_GROUPS):
        xg = jnp.where(grp == gi, sel, neg)
        m1 = jnp.max(xg, axis=-1, keepdims=True)
        i1 = jnp.min(jnp.where(xg == m1, lane, N_EXPERTS), axis=-1, keepdims=True)
        m2 = jnp.max(jnp.where(lane == i1, neg, xg), axis=-1, keepdims=True)
        gcols.append(m1 + m2)
        gsum = jnp.where(grp == gi, m1 + m2, gsum)
    beaten = jnp.zeros(shape, jnp.int32)
    for gi in range(N_GROUPS):
        ahead = (gcols[gi] > gsum) | ((gcols[gi] == gsum) & (gi < grp))
        beaten = beaten + ahead.astype(jnp.int32)
    cur = jnp.where(beaten < TOPK_GROUPS, sel, neg)
    chosen = jnp.zeros(shape, jnp.bool_)
    for _ in range(TOP_K):
        m = jnp.max(cur, axis=-1, keepdims=True)
        i = jnp.min(jnp.where(cur == m, lane, N_EXPERTS), axis=-1, keepdims=True)
        hit = lane == i
        chosen = chosen | hit
        cur = jnp.where(hit, neg, cur)
    w = jnp.where(chosen, scores, 0.0)
    return w / jnp.sum(w, axis=-1, keepdims=True) * ROUTED_SCALE


def _ffn_pre_body(h_ref, g_ref, sh_ref, sc_ref, wr_ref, br_ref, v_ref, wc_ref):
    vl = _modulate(h_ref[...], g_ref[...], sh_ref[...], sc_ref[...])
    v_ref[...] = vl.astype(v_ref.dtype)
    scores = _sigmoid(jnp.dot(vl, wr_ref[...], precision=HI, preferred_element_type=F32))
    wc_ref[...] = _route_weights(scores, scores + br_ref[...])


def _ffn_pre(h, g_pre, mod4, w_router, b_router, seg_fn):
    t, d = h.shape
    tm = ROW_TILE
    row = lambda n: pl.BlockSpec((tm, n), lambda i: (i, 0))
    return pl.pallas_call(
        _ffn_pre_body,
        name="ffn_pre_router",
        out_shape=[jax.ShapeDtypeStruct((t, d), MX), jax.ShapeDtypeStruct((t, N_EXPERTS), F32)],
        grid=(t // tm,),
        in_specs=[row(d), _full_spec(g_pre), _mod_spec(seg_fn, 3, d), _mod_spec(seg_fn, 4, d),
                  _full_spec(w_router), _full_spec(b_router)],
        out_specs=[row(d), row(N_EXPERTS)],
        compiler_params=_cparams(("parallel",)),
    )(h, g_pre, mod4, mod4, w_router, b_router)


def _moe_body(x_ref, wc_ref, wg_ref, wu_ref, wd_ref, sg_ref, su_ref, sd_ref, h_ref, gpost_ref,
              gate_ref, o_ref, acc_ref, *, eb):
    e = pl.program_id(1)
    x = x_ref[...]

    @pl.when(e == 0)
    def _():
        hid = _silu(_dot(x, sg_ref[...])) * _dot(x, su_ref[...])
        acc_ref[...] = _dot(hid.astype(MX), sd_ref[...])

    wc = wc_ref[...]
    lane = lax.broadcasted_iota(jnp.int32, wc.shape, 1)
    for j in range(eb):
        wcol = jnp.sum(jnp.where(lane == e * eb + j, wc, 0.0), axis=-1, keepdims=True)
        hid = _silu(_dot(x, wg_ref[j])) * _dot(x, wu_ref[j]) * wcol
        acc_ref[...] += _dot(hid.astype(MX), wd_ref[j])

    @pl.when(e == pl.num_programs(1) - 1)
    def _():
        o_ref[...] = _residual(h_ref[...], acc_ref[...], gpost_ref[...], gate_ref[...])


def _moe(x, wc, wg, wu, wd, sg, su, sd, h, gpost, mod4, seg_fn, eb=4):
    t, d = x.shape
    tm = ROW_TILE
    row = lambda n: pl.BlockSpec((tm, n), lambda i, e: (i, 0))
    const = lambda arr: pl.BlockSpec(arr.shape, lambda i, e: (0,) * arr.ndim)
    return pl.pallas_call(
        functools.partial(_moe_body, eb=eb),
        name="moe_experts",
        out_shape=jax.ShapeDtypeStruct((t, d), F32),
        grid=(t // tm, N_EXPERTS // eb),
        in_specs=[row(d), row(N_EXPERTS),
                  pl.BlockSpec((eb, d, D_EXPERT), lambda i, e: (e, 0, 0)),
                  pl.BlockSpec((eb, d, D_EXPERT), lambda i, e: (e, 0, 0)),
                  pl.BlockSpec((eb, D_EXPERT, d), lambda i, e: (e, 0, 0)),
                  const(sg), const(su), const(sd), row(d), const(gpost),
                  pl.BlockSpec((None, None, 1, d), lambda i, e: (seg_fn(i), 5, 0, 0))],
        out_specs=row(d),
        scratch_shapes=[pltpu.VMEM((tm, d), F32)],
        compiler_params=_cparams(("parallel", "arbitrary")),
    )(x, wc, wg, wu, wd, sg, su, sd, h, gpost, mod4)


def _rope_tables(seq_len, n_ident):
    rows = seq_len // GRID_W
    row = jnp.repeat(jnp.arange(rows, dtype=F32), GRID_W)
    col = jnp.tile(jnp.arange(GRID_W, dtype=F32), rows)
    axis_dim = 32
    inv_freq = ROPE_THETA ** (-jnp.arange(0, axis_dim, 2, dtype=F32) / axis_dim)
    ang = jnp.concatenate([row[:, None] * inv_freq, col[:, None] * inv_freq], axis=-1)
    cos, sin = jnp.cos(ang), jnp.sin(ang)
    cos64 = jnp.concatenate([cos, cos], axis=-1)
    sin64 = jnp.concatenate([-sin, sin], axis=-1)
    cos64 = jnp.concatenate([jnp.ones((n_ident, 64), F32), cos64], axis=0)
    sin64 = jnp.concatenate([jnp.zeros((n_ident, 64), F32), sin64], axis=0)
    return cos64, sin64


def _even_weights(w_in):
    d = w_in.shape[0]
    cuts = np.cumsum([GLA_QK, GLA_QK, GLA_VW, GLA_VW, GLA_RANK, GLA_RANK, SWA_W,
                      SWA_KV_HEADS * SWA_HD])
    q, k, v, g, af, ab, sq, sk, sv = jnp.split(w_in, [int(c) for c in cuts], axis=1)
    a = jnp.concatenate([af, ab, jnp.zeros((d, 128 - 2 * GLA_RANK), w_in.dtype)], axis=1)

    def rep(wkv):
        return jnp.tile(wkv.reshape(d, SWA_KV_HEADS, 1, SWA_HD), (1, 1, SWA_G, 1)).reshape(d, SWA_W)

    return jnp.concatenate([q, k, v, g, a, sq, rep(sk), rep(sv)], axis=1).astype(MX)


def _layer_tail(h, h_lat_view, l, mod4, seg_fn, p):
    del h_lat_view
    x, wc = _ffn_pre(h, p['g_ffn_pre'][l][None], mod4, p['w_router'][l], p['b_router'][l][None], seg_fn)
    return _moe(x, wc, p['w_exp_gate'][l].astype(MX), p['w_exp_up'][l].astype(MX),
                p['w_exp_down'][l].astype(MX), p['w_sh_gate'][l].astype(MX),
                p['w_sh_up'][l].astype(MX), p['w_sh_down'][l].astype(MX),
                h, p['g_ffn_post'][l][None], mod4, seg_fn)


def kernel(x, c, ctx, c_ctx, w_mod, b_mod, g_mix_pre, g_mix_post, g_ffn_pre, g_ffn_post, w_in_e, gla_wa2, gla_ba, gla_norm, swa_sink, w_out_e, w_in_o, mla_q_norm, mla_kv_norm, w_uq, w_ukv, w_out_o, w_router, b_router, w_exp_gate, w_exp_up, w_exp_down, w_sh_gate, w_sh_up, w_sh_down):
    p = dict(g_ffn_pre=g_ffn_pre, g_ffn_post=g_ffn_post, w_router=w_router, b_router=b_router,
             w_exp_gate=w_exp_gate, w_exp_up=w_exp_up, w_exp_down=w_exp_down,
             w_sh_gate=w_sh_gate, w_sh_up=w_sh_up, w_sh_down=w_sh_down)
    batch, seq_len, d = x.shape
    ctx_len = ctx.shape[1]
    depth = w_mod.shape[0]
    assert depth == 2 and batch * ctx_len == ROW_TILE and seq_len % ROW_TILE == 0
    lat_blocks = seq_len // ROW_TILE

    cvec = jnp.concatenate([c, c_ctx[None], jnp.zeros((8 - batch - 1, d), F32)], axis=0)
    mod = _mod_vectors(cvec, w_mod, b_mod)
    h = jnp.concatenate([ctx.reshape(batch * ctx_len, d), x.reshape(batch * seq_len, d)], axis=0)

    seg_all = lambda i: jnp.where(i == 0, batch, (i - 1) // lat_blocks)
    seg_lat = lambda i: i // lat_blocks
    pos_all = lambda i: jnp.where(i == 0, 0, 1 + (i - 1) % lat_blocks)
    cos64, sin64 = _rope_tables(seq_len, ROW_TILE)

    mod4 = mod[0].reshape(8, 6, 1, d)
    q, k, v, gg, a, sq, skr, svr = _even_in(
        h, g_mix_pre[0][None], mod4, _even_weights(w_in_e[0]),
        jnp.tile(cos64, (1, SWA_HEADS)), jnp.tile(sin64, (1, SWA_HEADS)), seg_all, pos_all)
    wa = jnp.zeros((2, 128, GLA_QK), F32)
    wa = wa.at[0, :GLA_RANK].set(gla_wa2[0, 0]).at[1, GLA_RANK:2 * GLA_RANK].set(gla_wa2[0, 1])
    o_f, o_b = _gla(q, k, v, a, wa, gla_ba[0][:, None, :], batch, ctx_len, seq_len)
    a_swa = _swa(sq, skr, svr, swa_sink[0], batch, ctx_len, seq_len)
    w_out = w_out_e[0].astype(MX)
    h = _even_out(h, o_f, o_b, gg, a_swa, gla_norm[0][None], w_out[:GLA_VW], w_out[GLA_VW:],
                  g_mix_post[0][None], mod4, seg_all)
    h = _layer_tail(h, None, 0, mod4, seg_all, p)

    mod4 = mod[1].reshape(8, 6, 1, d)
    ones = jnp.ones_like(cos64)
    cos_h = jnp.concatenate([ones, ones, cos64, ones], axis=1)
    sin_h = jnp.concatenate([0 * ones, 0 * ones, sin64, 0 * ones], axis=1)
    w_in = jnp.concatenate([w_in_o[0], jnp.zeros((d, 128 - MLA_ROPE), F32)], axis=1).astype(MX)
    cq, ckv, kr = _odd_in(h, g_mix_pre[1][None], mod4, w_in, mla_q_norm[0][None],
                          mla_kv_norm[0][None], cos_h, sin_h, seg_all, pos_all)
    w_q = w_uq[0].reshape(MLA_Q_RANK, MLA_HEADS, MLA_NOPE + MLA_ROPE)
    w_q = jnp.pad(w_q, ((0, 0), (0, 0), (0, MLA_QK_PAD - MLA_NOPE - MLA_ROPE)))
    w_q = w_q.reshape(MLA_Q_RANK, MLA_HEADS * MLA_QK_PAD).astype(MX)
    n_lat = batch * seq_len
    qh = _q_up(cq, 1, n_lat, w_q, cos_h, sin_h, pos_all)
    w_kv = w_ukv[0].astype(MX)
    k_lat, v_lat = _kv_up(ckv, kr, 1, n_lat, ROW_TILE, w_kv)
    k_ctx, v_ctx = _kv_up(ckv, kr, 0, batch * ctx_len, ctx_len, w_kv)
    o = _mla_attn(qh, k_ctx, v_ctx, k_lat, v_lat, batch, ctx_len, seq_len)
    hl = _odd_out(h, 1, o, w_out_o[0].astype(MX), g_mix_post[1][None], mod4, seg_lat)
    hl = _layer_tail(hl, None, 1, mod4, seg_lat, p)
    return hl.reshape(batch, seq_len, d)
```

```python
import functools
import math

import jax
import jax.numpy as jnp
import numpy as np
from jax import lax
from jax.experimental import pallas as pl
from jax.experimental.pallas import tpu as pltpu

F32 = jnp.float32
MX = jnp.bfloat16
HI = lax.Precision.HIGHEST

GRID_W = 64
ROPE_THETA = 10000.0
RMS_EPS = 1e-6

GLA_HEADS, GLA_DK, GLA_DV, GLA_RANK, GLA_TAU = 4, 64, 128, 16, 16.0
GLA_QK = GLA_HEADS * GLA_DK
GLA_VW = GLA_HEADS * GLA_DV
GLA_CHUNK = 128

SWA_HEADS, SWA_KV_HEADS, SWA_HD, WINDOW = 8, 2, 64, 128
SWA_G = SWA_HEADS // SWA_KV_HEADS
SWA_W = SWA_HEADS * SWA_HD

MLA_HEADS, MLA_Q_RANK, MLA_KV_RANK = 8, 512, 256
MLA_NOPE, MLA_ROPE, MLA_V = 128, 64, 128
MLA_QK_PAD = 256
MLA_SCALE = (MLA_NOPE + MLA_ROPE) ** -0.5

N_EXPERTS, TOP_K, N_GROUPS, TOPK_GROUPS = 64, 8, 8, 4
GROUP_SIZE = N_EXPERTS // N_GROUPS
D_EXPERT = 256
ROUTED_SCALE = 2.5

ROW_TILE = 512
VMEM_LIMIT = 56 * 1024 * 1024


def _cparams(sem):
    return pltpu.CompilerParams(dimension_semantics=sem, vmem_limit_bytes=VMEM_LIMIT)


def _sigmoid(x):
    return 1.0 / (1.0 + jnp.exp(-x))


def _silu(x):
    return x * _sigmoid(x)


def _rms(x, g):
    ms = jnp.mean(x * x, axis=-1, keepdims=True)
    return x * lax.rsqrt(ms + RMS_EPS) * g


def _modulate(h, g, shift, scale):
    return _rms(h, g) * (1.0 + scale) + shift


def _dot(a, b):
    return jnp.dot(a, b, preferred_element_type=F32)


def _dot_t(a, b):
    return lax.dot_general(a, b, (((1,), (1,)), ((), ())), preferred_element_type=F32)


def _rope(x, cos, sin_signed):
    n = x.shape[-1]
    lane = lax.broadcasted_iota(jnp.int32, x.shape, 1)
    first = (lane % 64) < 32
    partner = jnp.where(first, pltpu.roll(x, n - 32, 1), pltpu.roll(x, 32, 1))
    return x * cos + partner * sin_signed


def _mod_body(c_ref, w_ref, b_ref, o_ref):
    s = _silu(c_ref[...])
    o_ref[0] = jnp.dot(s, w_ref[0], precision=HI, preferred_element_type=F32) + b_ref[0]


def _mod_vectors(cvec, w_mod, b_mod):
    depth, d, n = w_mod.shape
    tn = 1536
    return pl.pallas_call(
        _mod_body,
        name="mod_vectors",
        out_shape=jax.ShapeDtypeStruct((depth, 8, n), F32),
        grid=(depth, n // tn),
        in_specs=[pl.BlockSpec((8, d), lambda l, j: (0, 0)),
                  pl.BlockSpec((1, d, tn), lambda l, j: (l, 0, j)),
                  pl.BlockSpec((1, 1, tn), lambda l, j: (l, 0, j))],
        out_specs=pl.BlockSpec((1, 8, tn), lambda l, j: (l, 0, j)),
        compiler_params=_cparams(("arbitrary", "arbitrary")),
    )(cvec, w_mod, b_mod.reshape(depth, 1, n))


def _mod_spec(seg_fn, which, d):
    return pl.BlockSpec((None, None, 1, d), lambda i, *_: (seg_fn(i), which, 0, 0))


def _full_spec(arr):
    nd = arr.ndim
    return pl.BlockSpec(arr.shape, lambda *_: (0,) * nd)


_EV_Q, _EV_K, _EV_V, _EV_G, _EV_A, _EV_SQ, _EV_SK, _EV_SV, _EV_END = (
    0, 256, 512, 1024, 1536, 1664, 2176, 2688, 3200)


def _even_in_body(h_ref, g_ref, sh_ref, sc_ref, w_ref, cos_ref, sin_ref,
                  q_ref, k_ref, v_ref, gg_ref, a_ref, sq_ref, sk_ref, sv_ref):
    u = _modulate(h_ref[...], g_ref[...], sh_ref[...], sc_ref[...]).astype(MX)
    q_ref[...] = _dot(u, w_ref[:, _EV_Q:_EV_K]) * (GLA_DK ** -0.5)
    k_ref[...] = _dot(u, w_ref[:, _EV_K:_EV_V])
    v_ref[...] = _dot(u, w_ref[:, _EV_V:_EV_G])
    gg_ref[...] = _dot(u, w_ref[:, _EV_G:_EV_A])
    a_ref[...] = _dot(u, w_ref[:, _EV_A:_EV_SQ])
    cos = cos_ref[...]
    sin = sin_ref[...]
    sq = _rope(_dot(u, w_ref[:, _EV_SQ:_EV_SK]), cos, sin)
    sq_ref[...] = (sq * (SWA_HD ** -0.5)).astype(sq_ref.dtype)
    sk_ref[...] = _rope(_dot(u, w_ref[:, _EV_SK:_EV_SV]), cos, sin).astype(sk_ref.dtype)
    sv_ref[...] = _dot(u, w_ref[:, _EV_SV:_EV_END]).astype(sv_ref.dtype)


def _even_in(h, g_pre, mod4, w_cat, cos_t, sin_t, seg_fn, pos_fn):
    t, d = h.shape
    tm = ROW_TILE
    row = lambda n: pl.BlockSpec((tm, n), lambda i: (i, 0))
    outs = [(GLA_QK, F32), (GLA_QK, F32), (GLA_VW, F32), (GLA_VW, F32), (128, F32),
            (SWA_W, MX), (SWA_W, MX), (SWA_W, MX)]
    return pl.pallas_call(
        _even_in_body,
        name="even_in",
        out_shape=[jax.ShapeDtypeStruct((t, n), dt) for n, dt in outs],
        grid=(t // tm,),
        in_specs=[row(d), _full_spec(g_pre), _mod_spec(seg_fn, 0, d), _mod_spec(seg_fn, 1, d),
                  _full_spec(w_cat),
                  pl.BlockSpec((tm, SWA_W), lambda i: (pos_fn(i), 0)),
                  pl.BlockSpec((tm, SWA_W), lambda i: (pos_fn(i), 0))],
        out_specs=[row(n) for n, _ in outs],
        compiler_params=_cparams(("parallel",)),
    )(h, g_pre, mod4, mod4, w_cat, cos_t, sin_t)


def _gla_consts(c, reverse):
    nlev = int(round(math.log2(c)))
    idx = np.arange(c)
    i, m = idx[:, None], idx[None, :]
    sizes = [c >> (l + 1) for l in range(nlev)]
    mats = [m <= i, m > i]
    for s in sizes:
        mats.append((m >= (i // s) * s) & (m <= i))
    for s in sizes:
        mats.append((m > i) & (m <= (i // s) * s + s - 1))
    a = np.stack(mats).astype(np.float32)
    masks = [np.eye(c, dtype=bool)]
    for s in sizes:
        bi, bj = i // s, m // s
        masks.append((bi % 2 == 1) & (bj == bi - 1))
    msk = np.stack(masks).astype(np.float32)
    if reverse:
        a = a[:, ::-1, ::-1]
        msk = msk[:, ::-1, ::-1]
    return (np.ascontiguousarray(a.reshape(-1, c)),
            np.ascontiguousarray(np.tile(msk, (1, 1, GLA_HEADS))))


def _split3(x):
    hi = x.astype(MX)
    r1 = x - hi.astype(F32)
    mid = r1.astype(MX)
    lo = (r1 - mid.astype(F32)).astype(MX)
    return hi, mid, lo


def _gla_direction(q, k, v, a, wa, ba, amat, lmask, hmask, vmask, bdmask, st_ref, d, last_row):
    c = q.shape[0]
    nlev = int(round(math.log2(c)))
    x = jnp.dot(a, wa, precision=HI, preferred_element_type=F32) + ba
    la = (jnp.minimum(x, 0.0) - jnp.log1p(jnp.exp(-jnp.abs(x)))) * (1.0 / GLA_TAU)
    hi, mid, lo = _split3(la)
    am = amat.astype(MX)
    ex = jnp.exp(_dot(am, hi) + _dot(am, mid) + _dot(am, lo))
    qd = (q * ex[0:c]).astype(MX)
    kd = (k * ex[c:2 * c]).astype(MX)
    st = st_ref[d]
    o = _dot_t(qd, st.astype(MX))
    scat = jnp.zeros((c, GLA_HEADS * c), F32)
    for lev in range(nlev + 1):
        if lev == 0:
            ql, kl = q, k
        else:
            ql = q * ex[(1 + lev) * c:(2 + lev) * c]
            kl = k * ex[(1 + nlev + lev) * c:(2 + nlev + lev) * c]
        kst = (jnp.concatenate([kl] * GLA_HEADS, axis=0) * hmask).astype(MX)
        scat = scat + _dot_t(ql.astype(MX), kst) * lmask[lev]
    vbd = (jnp.concatenate([v] * GLA_HEADS, axis=0) * vmask).astype(MX)
    o = o + _dot(scat.astype(MX), vbd)
    upd = lax.dot_general(v.astype(MX), kd, (((0,), (0,)), ((), ())), preferred_element_type=F32)
    st_ref[d] = st * ex[last_row:last_row + 1] + upd * bdmask
    return o


def _gla_body(qf_ref, kf_ref, vf_ref, af_ref, qb_ref, kb_ref, vb_ref, ab_ref,
              wa_ref, ba_ref, amf_ref, lmf_ref, amb_ref, lmb_ref, hm_ref, vm_ref, bd_ref,
              of_ref, ob_ref, st_ref):
    @pl.when(pl.program_id(1) == 0)
    def _():
        st_ref[...] = jnp.zeros_like(st_ref)

    c = qf_ref.shape[0]
    hm, vm, bd = hm_ref[...], vm_ref[...], bd_ref[...]
    of_ref[...] = _gla_direction(qf_ref[...], kf_ref[...], vf_ref[...], af_ref[...],
                                 wa_ref[0], ba_ref[0], amf_ref[...], lmf_ref, hm, vm, bd,
                                 st_ref, 0, c - 1)
    ob_ref[...] = _gla_direction(qb_ref[...], kb_ref[...], vb_ref[...], ab_ref[...],
                                 wa_ref[1], ba_ref[1], amb_ref[...], lmb_ref, hm, vm, bd,
                                 st_ref, 1, 0)


def _gla(q, k, v, a, wa, ba, batch, ctx_len, seq_len):
    t = q.shape[0]
    c = GLA_CHUNK
    nc, nl = ctx_len // c, seq_len // c
    amf, lmf = _gla_consts(c, False)
    amb, lmb = _gla_consts(c, True)
    r = np.arange(GLA_HEADS * c)[:, None] // c
    hm = (r == np.arange(GLA_QK)[None, :] // GLA_DK).astype(np.float32)
    vm = (r == np.arange(GLA_VW)[None, :] // GLA_DV).astype(np.float32)
    bd = (np.arange(GLA_VW)[:, None] // GLA_DV
          == np.arange(GLA_QK)[None, :] // GLA_DK).astype(np.float32)

    def fwd(b, s):
        return jnp.where(s < nc, nc * b + s, batch * nc + nl * b + (s - nc))

    def bwd(b, s):
        return jnp.where(s < nc, nc * b + (nc - 1 - s), batch * nc + nl * b + (nl - 1 - (s - nc)))

    def chunk(n, fn):
        return pl.BlockSpec((c, n), lambda b, s: (fn(b, s), 0))

    consts = [jnp.asarray(z) for z in (amf, lmf, amb, lmb, hm, vm, bd)]
    ins = [q, k, v, a, q, k, v, a, wa, ba] + consts
    specs = ([chunk(GLA_QK, fwd), chunk(GLA_QK, fwd), chunk(GLA_VW, fwd), chunk(128, fwd),
              chunk(GLA_QK, bwd), chunk(GLA_QK, bwd), chunk(GLA_VW, bwd), chunk(128, bwd)]
             + [_full_spec(z) for z in ins[8:]])
    return pl.pallas_call(
        _gla_body,
        name="gla_scan",
        out_shape=[jax.ShapeDtypeStruct((t, GLA_VW), F32)] * 2,
        grid=(batch, nc + nl),
        in_specs=specs,
        out_specs=[chunk(GLA_VW, fwd), chunk(GLA_VW, bwd)],
        scratch_shapes=[pltpu.VMEM((2, GLA_VW, GLA_QK), F32)],
        compiler_params=_cparams(("parallel", "arbitrary")),
    )(*ins)


def _swa_heads(q, kcat, vcat, valid, sink_ref, h):
    lane_head = lax.broadcasted_iota(jnp.int32, (1, SWA_G * SWA_HD), 1) // SWA_HD
    acc = jnp.zeros((q.shape[0], SWA_G * SWA_HD), F32)
    for g in range(SWA_G):
        hm = lane_head == g
        s = _dot_t(jnp.where(hm, q, jnp.zeros_like(q)), kcat)
        if valid is not None:
            s = jnp.where(valid, s, -jnp.inf)
        sk = sink_ref[h * SWA_G + g]
        m = jnp.maximum(jnp.max(s, axis=-1, keepdims=True), sk)
        p = jnp.exp(s - m)
        den = jnp.sum(p, axis=-1, keepdims=True) + jnp.exp(sk - m)
        og = _dot(p.astype(MX), jnp.where(hm, vcat, jnp.zeros_like(vcat)))
        acc = acc + og * (1.0 / den)
    return acc


def _swa_latent_body(sink_ref, q_ref, kc_ref, vc_ref, kp_ref, k0_ref, kn_ref,
                     vp_ref, v0_ref, vn_ref, o_ref, *, nblk):
    h, n = pl.program_id(1), pl.program_id(2)
    blk = q_ref.shape[0]
    nctx = kc_ref.shape[0]
    kcat = jnp.concatenate([kc_ref[...], kp_ref[...], k0_ref[...], kn_ref[...]], axis=0)
    vcat = jnp.concatenate([vc_ref[...], vp_ref[...], v0_ref[...], vn_ref[...]], axis=0)
    shape = (blk, nctx + 3 * blk)
    qi = lax.broadcasted_iota(jnp.int32, shape, 0)
    col = lax.broadcasted_iota(jnp.int32, shape, 1)
    si = col - nctx
    kpos = (n - 1) * blk + si
    valid = (col < nctx) | ((jnp.abs(si - blk - qi) <= WINDOW) & (kpos >= 0) & (kpos < nblk * blk))
    o_ref[...] = _swa_heads(q_ref[...], kcat, vcat, valid, sink_ref, h).astype(o_ref.dtype)


def _swa_ctx_body(sink_ref, q_ref, kc_ref, vc_ref, o_ref):
    h = pl.program_id(1)
    o_ref[...] = _swa_heads(q_ref[...], kc_ref[...], vc_ref[...], None, sink_ref, h).astype(o_ref.dtype)


def _swa(sq, skr, svr, sink, batch, ctx_len, seq_len):
    t = sq.shape[0]
    blk = WINDOW
    w = SWA_G * SWA_HD
    nb = seq_len // blk
    cb = ctx_len // blk
    lat0 = batch * cb
    smem = pl.BlockSpec(memory_space=pltpu.SMEM)
    ctx_kv = pl.BlockSpec((ctx_len, w), lambda b, h, n: (b, h))

    def win(off):
        return pl.BlockSpec((blk, w), lambda b, h, n: (lat0 + b * nb + jnp.clip(n + off, 0, nb - 1), h))

    lat = pl.pallas_call(
        functools.partial(_swa_latent_body, nblk=nb),
        name="swa_latent",
        out_shape=jax.ShapeDtypeStruct((batch * seq_len, SWA_W), MX),
        grid=(batch, SWA_KV_HEADS, nb),
        in_specs=[smem, pl.BlockSpec((blk, w), lambda b, h, n: (lat0 + b * nb + n, h)),
                  ctx_kv, ctx_kv, win(-1), win(0), win(1), win(-1), win(0), win(1)],
        out_specs=pl.BlockSpec((blk, w), lambda b, h, n: (b * nb + n, h)),
        compiler_params=_cparams(("parallel", "parallel", "arbitrary")),
    )(sink, sq, skr, svr, skr, skr, skr, svr, svr, svr)
    ctx = pl.pallas_call(
        _swa_ctx_body,
        name="swa_ctx",
        out_shape=jax.ShapeDtypeStruct((batch * ctx_len, SWA_W), MX),
        grid=(batch, SWA_KV_HEADS, cb),
        in_specs=[smem, pl.BlockSpec((blk, w), lambda b, h, n: (b * cb + n, h)), ctx_kv, ctx_kv],
        out_specs=pl.BlockSpec((blk, w), lambda b, h, n: (b * cb + n, h)),
        compiler_params=_cparams(("parallel", "parallel", "arbitrary")),
    )(sink, sq, skr, svr)
    return jnp.concatenate([ctx, lat], axis=0)


def _residual(h, y, gpost, gate):
    return h + gate * _rms(y, gpost)


def _even_out_body(h_ref, of_ref, ob_ref, gg_ref, a_ref, gn_ref, w1_ref, w2_ref, gpost_ref,
                   gate_ref, o_ref):
    o = of_ref[...] + ob_ref[...]
    gn = gn_ref[...]
    parts = [_rms(o[:, j * GLA_DV:(j + 1) * GLA_DV], gn) for j in range(GLA_HEADS)]
    gl = jnp.concatenate(parts, axis=-1) * _silu(gg_ref[...])
    y = _dot(gl.astype(MX), w1_ref[...]) + _dot(a_ref[...], w2_ref[...])
    o_ref[...] = _residual(h_ref[...], y, gpost_ref[...], gate_ref[...])


def _even_out(h, o_f, o_b, gg, a_swa, gn, w1, w2, gpost, mod4, seg_fn):
    t, d = h.shape
    tm = ROW_TILE
    row = lambda n: pl.BlockSpec((tm, n), lambda i: (i, 0))
    return pl.pallas_call(
        _even_out_body,
        name="even_out",
        out_shape=jax.ShapeDtypeStruct((t, d), F32),
        grid=(t // tm,),
        in_specs=[row(d), row(GLA_VW), row(GLA_VW), row(GLA_VW), row(SWA_W), _full_spec(gn),
                  _full_spec(w1), _full_spec(w2), _full_spec(gpost), _mod_spec(seg_fn, 2, d)],
        out_specs=row(d),
        compiler_params=_cparams(("parallel",)),
    )(h, o_f, o_b, gg, a_swa, gn, w1, w2, gpost, mod4)


def _odd_out_body(h_ref, o_ref_in, w_ref, gpost_ref, gate_ref, o_ref):
    y = _dot(o_ref_in[...], w_ref[...])
    o_ref[...] = _residual(h_ref[...], y, gpost_ref[...], gate_ref[...])


def _odd_out(h, h_off, o, w, gpost, mod4, seg_fn):
    t, d = o.shape[0], h.shape[1]
    tm = ROW_TILE
    row = lambda n: pl.BlockSpec((tm, n), lambda i: (i, 0))
    return pl.pallas_call(
        _odd_out_body,
        name="odd_out",
        out_shape=jax.ShapeDtypeStruct((t, d), F32),
        grid=(t // tm,),
        in_specs=[pl.BlockSpec((tm, d), lambda i: (i + h_off, 0)), row(o.shape[1]),
                  _full_spec(w), _full_spec(gpost), _mod_spec(seg_fn, 2, d)],
        out_specs=row(d),
        compiler_params=_cparams(("parallel",)),
    )(h, o, w, gpost, mod4)


def _odd_in_body(h_ref, g_ref, sh_ref, sc_ref, w_ref, qn_ref, kvn_ref, cos_ref, sin_ref,
                 cq_ref, ckv_ref, kr_ref):
    u = _modulate(h_ref[...], g_ref[...], sh_ref[...], sc_ref[...]).astype(MX)
    cq_ref[...] = _rms(_dot(u, w_ref[:, 0:MLA_Q_RANK]), qn_ref[...]).astype(cq_ref.dtype)
    c1 = MLA_Q_RANK + MLA_KV_RANK
    ckv_ref[...] = _rms(_dot(u, w_ref[:, MLA_Q_RANK:c1]), kvn_ref[...]).astype(ckv_ref.dtype)
    kr = _dot(u, w_ref[:, c1:c1 + 128])
    kr_ref[...] = _rope(kr, cos_ref[...], sin_ref[...]).astype(kr_ref.dtype)


def _odd_in(h, g_pre, mod4, w_cat, qn, kvn, cos_t, sin_t, seg_fn, pos_fn):
    t, d = h.shape
    tm = ROW_TILE
    row = lambda n: pl.BlockSpec((tm, n), lambda i: (i, 0))
    tab = pl.BlockSpec((tm, 128), lambda i: (pos_fn(i), 1))
    return pl.pallas_call(
        _odd_in_body,
        name="odd_in",
        out_shape=[jax.ShapeDtypeStruct((t, MLA_Q_RANK), MX),
                   jax.ShapeDtypeStruct((t, MLA_KV_RANK), MX),
                   jax.ShapeDtypeStruct((t, 128), MX)],
        grid=(t // tm,),
        in_specs=[row(d), _full_spec(g_pre), _mod_spec(seg_fn, 0, d), _mod_spec(seg_fn, 1, d),
                  _full_spec(w_cat), _full_spec(qn), _full_spec(kvn), tab, tab],
        out_specs=[row(MLA_Q_RANK), row(MLA_KV_RANK), row(128)],
        compiler_params=_cparams(("parallel",)),
    )(h, g_pre, mod4, mod4, w_cat, qn, kvn, cos_t, sin_t)


def _q_up_body(cq_ref, w_ref, cos_ref, sin_ref, o_ref):
    z = _rope(_dot(cq_ref[...], w_ref[...]), cos_ref[...], sin_ref[...])
    o_ref[...] = (z * (MLA_SCALE * math.log2(math.e))).astype(o_ref.dtype)


def _q_up(cq, row_off, nrows, w_pad, cos_t, sin_t, pos_fn):
    tm = ROW_TILE
    return pl.pallas_call(
        _q_up_body,
        name="mla_q_up",
        out_shape=jax.ShapeDtypeStruct((nrows, MLA_HEADS * MLA_QK_PAD), MX),
        grid=(nrows // tm, MLA_HEADS),
        in_specs=[pl.BlockSpec((tm, MLA_Q_RANK), lambda i, h: (i + row_off, 0)),
                  pl.BlockSpec((MLA_Q_RANK, MLA_QK_PAD), lambda i, h: (0, h)),
                  pl.BlockSpec((tm, MLA_QK_PAD), lambda i, h: (pos_fn(i + row_off), 0)),
                  pl.BlockSpec((tm, MLA_QK_PAD), lambda i, h: (pos_fn(i + row_off), 0))],
        out_specs=pl.BlockSpec((tm, MLA_QK_PAD), lambda i, h: (i, h)),
        compiler_params=_cparams(("parallel", "arbitrary")),
    )(cq, w_pad, cos_t, sin_t)


def _kv_up_body(ckv_ref, kr_ref, w_ref, k_ref, v_ref):
    z = _dot(ckv_ref[...], w_ref[...])
    k_ref[...] = jnp.concatenate([z[:, :MLA_NOPE].astype(k_ref.dtype), kr_ref[...]], axis=-1)
    v_ref[...] = z[:, MLA_NOPE:].astype(v_ref.dtype)


def _kv_up(ckv, kr, row_off, nrows, tm, w_ukv):
    return pl.pallas_call(
        _kv_up_body,
        name="mla_kv_up",
        out_shape=[jax.ShapeDtypeStruct((nrows, MLA_HEADS * MLA_QK_PAD), MX),
                   jax.ShapeDtypeStruct((nrows, MLA_HEADS * MLA_V), MX)],
        grid=(nrows // tm, MLA_HEADS),
        in_specs=[pl.BlockSpec((tm, MLA_KV_RANK), lambda i, h: (i + row_off, 0)),
                  pl.BlockSpec((tm, 128), lambda i, h: (i + row_off, 0)),
                  pl.BlockSpec((MLA_KV_RANK, MLA_NOPE + MLA_V), lambda i, h: (0, h))],
        out_specs=[pl.BlockSpec((tm, MLA_QK_PAD), lambda i, h: (i, h)),
                   pl.BlockSpec((tm, MLA_V), lambda i, h: (i, h))],
        compiler_params=_cparams(("parallel", "arbitrary")),
    )(ckv, kr, w_ukv)


def _mla_attn_body(q_ref, kc_ref, vc_ref, k_ref, v_ref, o_ref, *, tk, unroll):
    q = q_ref[...]
    s = _dot_t(q, kc_ref[...])
    m = jnp.max(s, axis=-1, keepdims=True)
    p = jnp.exp2(s - m)
    l = jnp.sum(p, axis=-1, keepdims=True)
    acc = _dot(p.astype(MX), vc_ref[...])

    def step(j, carry):
        m, l, acc = carry
        start = pl.multiple_of(j * tk, tk)
        s = _dot_t(q, k_ref[pl.ds(start, tk), :])
        mn = jnp.maximum(m, jnp.max(s, axis=-1, keepdims=True))
        alpha = jnp.exp2(m - mn)
        p = jnp.exp2(s - mn)
        l = alpha * l + jnp.sum(p, axis=-1, keepdims=True)
        acc = alpha * acc + _dot(p.astype(MX), v_ref[pl.ds(start, tk), :])
        return mn, l, acc

    m, l, acc = lax.fori_loop(0, k_ref.shape[0] // tk, step, (m, l, acc), unroll=unroll)
    o_ref[...] = (acc * (1.0 / l)).astype(o_ref.dtype)


def _mla_attn(q, k_ctx, v_ctx, k_lat, v_lat, batch, ctx_len, seq_len, tq=1024, tk=512, unroll=4):
    nq = seq_len // tq
    return pl.pallas_call(
        functools.partial(_mla_attn_body, tk=tk, unroll=unroll),
        name="mla_attn",
        out_shape=jax.ShapeDtypeStruct((batch * seq_len, MLA_HEADS * MLA_V), MX),
        grid=(batch, MLA_HEADS, nq),
        in_specs=[pl.BlockSpec((tq, MLA_QK_PAD), lambda b, h, i: (b * nq + i, h)),
                  pl.BlockSpec((ctx_len, MLA_QK_PAD), lambda b, h, i: (b, h)),
                  pl.BlockSpec((ctx_len, MLA_V), lambda b, h, i: (b, h)),
                  pl.BlockSpec((seq_len, MLA_QK_PAD), lambda b, h, i: (b, h)),
                  pl.BlockSpec((seq_len, MLA_V), lambda b, h, i: (b, h))],
        out_specs=pl.BlockSpec((tq, MLA_V), lambda b, h, i: (b * nq + i, h)),
        compiler_params=_cparams(("parallel", "parallel", "arbitrary")),
    )(q, k_ctx, v_ctx, k_lat, v_lat)


def _route_weights(scores, sel):
    shape = sel.shape
    lane = lax.broadcasted_iota(jnp.int32, shape, 1)
    grp = lane // GROUP_SIZE
    neg = -jnp.inf
    gsum = jnp.zeros(shape, F32)
    gcols = []
    for gi in range(N_GROUPS):
        xg = jnp.where(grp == gi, sel, neg)
        m1 = jnp.max(xg, axis=-1, keepdims=True)
        i1 = jnp.min(jnp.where(xg == m1, lane, N_EXPERTS), axis=-1, keepdims=True)
        m2 = jnp.max(jnp.where(lane == i1, neg, xg), axis=-1, keepdims=True)
        gcols.append(m1 + m2)
        gsum = jnp.where(grp == gi, m1 + m2, gsum)
    beaten = jnp.zeros(shape, jnp.int32)
    for gi in range(N_GROUPS):
        ahead = (gcols[gi] > gsum) | ((gcols[gi] == gsum) & (gi < grp))
        beaten = beaten + ahead.astype(jnp.int32)
    cur = jnp.where(beaten < TOPK_GROUPS, sel, neg)
    slot = lax.broadcasted_iota(jnp.int32, (shape[0], TOP_K), 1)
    idx = jnp.zeros((shape[0], TOP_K), jnp.int32)
    wts = jnp.zeros((shape[0], TOP_K), F32)
    for r in range(TOP_K):
        m = jnp.max(cur, axis=-1, keepdims=True)
        i = jnp.min(jnp.where(cur == m, lane, N_EXPERTS), axis=-1, keepdims=True)
        hit = lane == i
        val = jnp.sum(jnp.where(hit, scores, 0.0), axis=-1, keepdims=True)
        idx = jnp.where(slot == r, i, idx)
        wts = jnp.where(slot == r, val, wts)
        cur = jnp.where(hit, neg, cur)
    return idx, wts / jnp.sum(wts, axis=-1, keepdims=True) * ROUTED_SCALE


def _ffn_pre_body(h_ref, g_ref, sh_ref, sc_ref, wr_ref, br_ref, v_ref, idx_ref, wts_ref):
    vl = _modulate(h_ref[...], g_ref[...], sh_ref[...], sc_ref[...])
    v_ref[...] = vl.astype(v_ref.dtype)
    scores = _sigmoid(jnp.dot(vl, wr_ref[...], precision=HI, preferred_element_type=F32))
    idx_ref[...], wts_ref[...] = _route_weights(scores, scores + br_ref[...])


def _ffn_pre(h, g_pre, mod4, w_router, b_router, seg_fn):
    t, d = h.shape
    tm = ROW_TILE
    row = lambda n: pl.BlockSpec((tm, n), lambda i: (i, 0))
    return pl.pallas_call(
        _ffn_pre_body,
        name="ffn_pre_router",
        out_shape=[jax.ShapeDtypeStruct((t, d), MX), jax.ShapeDtypeStruct((t, TOP_K), jnp.int32),
                   jax.ShapeDtypeStruct((t, TOP_K), F32)],
        grid=(t // tm,),
        in_specs=[row(d), _full_spec(g_pre), _mod_spec(seg_fn, 3, d), _mod_spec(seg_fn, 4, d),
                  _full_spec(w_router), _full_spec(b_router)],
        out_specs=[row(d), row(TOP_K), row(TOP_K)],
        compiler_params=_cparams(("parallel",)),
    )(h, g_pre, mod4, mod4, w_router, b_router)


MOE_TB = ROW_TILE
MOE_PIECE = 16
MOE_CAP = TOP_K * MOE_TB + N_EXPERTS * MOE_PIECE
MOE_CHUNK = 512
MOE_RB = 256


def _dispatch_plan(idx, t):
    nb = t // MOE_TB
    sel = jnp.sum(jax.nn.one_hot(idx, N_EXPERTS, dtype=jnp.int32), axis=1)
    s3 = sel.reshape(nb, MOE_TB, N_EXPERTS)
    rank = jnp.cumsum(s3, axis=1) - s3
    cpad = (jnp.sum(s3, axis=1) + MOE_PIECE - 1) // MOE_PIECE * MOE_PIECE
    loff = jnp.cumsum(cpad, axis=1) - cpad
    tot = jnp.sum(cpad, axis=0)
    reg = (tot + MOE_RB - 1) // MOE_RB * MOE_RB
    gend = jnp.cumsum(reg)
    gbase = gend - reg
    goff = gbase[None] + jnp.cumsum(cpad, axis=0) - cpad
    dest = jnp.take_along_axis((loff[:, None, :] + rank).reshape(t, N_EXPERTS), idx, axis=1)
    ng = t * TOP_K + nb * N_EXPERTS * (MOE_PIECE - 1) + N_EXPERTS * (MOE_RB - 1)
    ng = -(-ng // MOE_RB) * MOE_RB
    start = jnp.arange(ng // MOE_RB, dtype=jnp.int32) * MOE_RB
    blk_e = jnp.sum((start[:, None] >= gend[None, :]).astype(jnp.int32), axis=1)
    blk_e = jnp.minimum(blk_e, N_EXPERTS - 1)
    active = (start < (gbase + tot)[blk_e]).astype(jnp.int32)
    i32 = lambda z: z.astype(jnp.int32)
    return dict(loff=i32(loff), goff=i32(goff), npc=i32(cpad // MOE_PIECE),
                ntot=i32(jnp.sum(cpad, axis=1) // MOE_PIECE), toff=i32(gbase + tot),
                tnp=i32((reg - tot) // MOE_PIECE), nused=i32(gend[-1:] // MOE_RB), dest=i32(dest),
                blk_e=blk_e, active=active, ng=ng)


def _for_each_piece(loff_ref, goff_ref, npc_ref, blk, fn):
    def seg(e, carry):
        lo, go = loff_ref[blk, e], goff_ref[blk, e]

        def piece(p, c):
            fn(pl.multiple_of(lo + p * MOE_PIECE, MOE_PIECE), pl.multiple_of(go + p * MOE_PIECE, MOE_PIECE))
            return c

        return lax.fori_loop(0, npc_ref[blk, e], piece, carry)

    lax.fori_loop(0, N_EXPERTS, seg, 0)


def _repeat(n, fn):
    def body(_, c):
        fn()
        return c

    lax.fori_loop(0, n, body, 0)


def _moe_sort_body(loff_ref, goff_ref, npc_ref, ntot_ref, toff_ref, tnp_ref, nused_ref, x_ref, dt_ref,
                   xg_ref, xs_ref, zb_ref, sem, zsem):
    i, nb = pl.program_id(0), pl.num_programs(0)
    slot = i % 2

    def piece_copy(s, lo, go):
        return pltpu.make_async_copy(xs_ref.at[s, pl.ds(lo, MOE_PIECE)],
                                     xg_ref.at[pl.ds(go, MOE_PIECE)], sem.at[s])

    def zero_copy(go):
        return pltpu.make_async_copy(zb_ref.at[pl.ds(0, MOE_PIECE)], xg_ref.at[pl.ds(go, MOE_PIECE)],
                                     zsem.at[0])

    def zero_block(b):
        return pltpu.make_async_copy(zb_ref, xg_ref.at[pl.ds(pl.multiple_of(b * MOE_RB, MOE_RB), MOE_RB)],
                                     zsem.at[1])

    @pl.when(i >= 2)
    def _():
        _repeat(ntot_ref[i - 2], lambda: piece_copy(slot, 0, 0).wait())

    x = x_ref[...]
    dt = dt_ref[...]
    for c in range(MOE_CAP // MOE_CHUNK):
        @pl.when(c * MOE_CHUNK < ntot_ref[i] * MOE_PIECE)
        def _():
            r = lax.broadcasted_iota(jnp.int32, (MOE_CHUNK, MOE_TB), 0) + c * MOE_CHUNK
            hit = r == dt[0:1, :]
            for k in range(1, TOP_K):
                hit = hit | (r == dt[k:k + 1, :])
            pi = jnp.where(hit, 1.0, 0.0).astype(MX)
            xs_ref[slot, c * MOE_CHUNK:(c + 1) * MOE_CHUNK, :] = _dot(pi, x).astype(xs_ref.dtype)

    _for_each_piece(loff_ref, goff_ref, npc_ref, i, lambda lo, go: piece_copy(slot, lo, go).start())

    @pl.when(i == nb - 1)
    def _():
        zb_ref[...] = jnp.zeros_like(zb_ref)

        def tail(e, n):
            def piece(p, c):
                zero_copy(pl.multiple_of(toff_ref[e] + p * MOE_PIECE, MOE_PIECE)).start()
                return c

            lax.fori_loop(0, tnp_ref[e], piece, 0)
            return n + tnp_ref[e]

        nz = lax.fori_loop(0, N_EXPERTS, tail, 0)
        nblk = xg_ref.shape[0] // MOE_RB

        def unused(b, c):
            zero_block(b).start()
            return c

        lax.fori_loop(nused_ref[0], nblk, unused, 0)
        _repeat(nz, lambda: zero_copy(0).wait())
        _repeat(nblk - nused_ref[0], lambda: zero_block(0).wait())
        _repeat(ntot_ref[i], lambda: piece_copy(slot, 0, 0).wait())

    @pl.when((i == nb - 1) & (i >= 1))
    def _():
        _repeat(ntot_ref[i - 1], lambda: piece_copy(1 - slot, 0, 0).wait())


def _moe_expert_body(be_ref, act_ref, x_ref, wg_ref, wu_ref, wd_ref, o_ref):
    del be_ref
    i = pl.program_id(0)

    @pl.when(act_ref[i] != 0)
    def _():
        x = x_ref[...]
        hid = _silu(_dot(x, wg_ref[...])) * _dot(x, wu_ref[...])
        o_ref[...] = _dot(hid.astype(MX), wd_ref[...]).astype(o_ref.dtype)

    @pl.when(act_ref[i] == 0)
    def _():
        o_ref[...] = jnp.zeros_like(o_ref)


def _moe_combine_body(loff_ref, goff_ref, npc_ref, ntot_ref, yg_ref, dest_ref, wts_ref, x_ref,
                      sg_ref, su_ref, sd_ref, h_ref, gpost_ref, gate_ref, o_ref, ys_ref, sem):
    i, nb = pl.program_id(0), pl.num_programs(0)
    slot = i % 2

    def piece_copy(s, lo, go):
        return pltpu.make_async_copy(yg_ref.at[pl.ds(go, MOE_PIECE)],
                                     ys_ref.at[s, pl.ds(lo, MOE_PIECE)], sem.at[s])

    def fetch(blk, s):
        ys_ref[s] = jnp.zeros(ys_ref.shape[1:], ys_ref.dtype)
        _for_each_piece(loff_ref, goff_ref, npc_ref, blk, lambda lo, go: piece_copy(s, lo, go).start())

    @pl.when(i == 0)
    def _():
        fetch(0, 0)

    @pl.when(i + 1 < nb)
    def _():
        fetch(i + 1, 1 - slot)

    x = x_ref[...]
    acc = _dot((_silu(_dot(x, sg_ref[...])) * _dot(x, su_ref[...])).astype(MX), sd_ref[...])
    _repeat(ntot_ref[i], lambda: piece_copy(slot, 0, 0).wait())
    dest, wts = dest_ref[...], wts_ref[...]
    for c in range(MOE_CAP // MOE_CHUNK):
        r = lax.broadcasted_iota(jnp.int32, (MOE_TB, MOE_CHUNK), 1) + c * MOE_CHUNK
        pw = jnp.zeros((MOE_TB, MOE_CHUNK), F32)
        for k in range(TOP_K):
            pw = jnp.where(dest[:, k:k + 1] == r, wts[:, k:k + 1], pw)
        acc = acc + _dot(pw.astype(MX), ys_ref[slot, c * MOE_CHUNK:(c + 1) * MOE_CHUNK, :])
    o_ref[...] = _residual(h_ref[...], acc, gpost_ref[...], gate_ref[...])


def _moe(x, idx, wts, wg, wu, wd, sg, su, sd, h, gpost, mod4, seg_fn):
    t, d = x.shape
    nb = t // MOE_TB
    plan = _dispatch_plan(idx, t)
    ng = plan['ng']
    tables = (plan['loff'], plan['goff'], plan['npc'], plan['ntot'])
    dest_t = plan['dest'].reshape(nb, MOE_TB, TOP_K).transpose(0, 2, 1)
    xg = pl.pallas_call(
        _moe_sort_body,
        name="moe_sort",
        out_shape=jax.ShapeDtypeStruct((ng, d), MX),
        grid_spec=pltpu.PrefetchScalarGridSpec(
            num_scalar_prefetch=7, grid=(nb,),
            in_specs=[pl.BlockSpec((MOE_TB, d), lambda i, *_: (i, 0)),
                      pl.BlockSpec((None, TOP_K, MOE_TB), lambda i, *_: (i, 0, 0))],
            out_specs=pl.BlockSpec(memory_space=pl.ANY),
            scratch_shapes=[pltpu.VMEM((2, MOE_CAP, d), MX), pltpu.VMEM((MOE_RB, d), MX),
                            pltpu.SemaphoreType.DMA((2,)), pltpu.SemaphoreType.DMA((2,))]),
        compiler_params=_cparams(("arbitrary",)),
    )(*tables, plan['toff'], plan['tnp'], plan['nused'], x, dest_t)
    yg = pl.pallas_call(
        _moe_expert_body,
        name="moe_experts",
        out_shape=jax.ShapeDtypeStruct((ng, d), MX),
        grid_spec=pltpu.PrefetchScalarGridSpec(
            num_scalar_prefetch=2, grid=(ng // MOE_RB,),
            in_specs=[pl.BlockSpec((MOE_RB, d), lambda i, be, act: (jnp.where(act[i] != 0, i, 0), 0)),
                      pl.BlockSpec((None, d, D_EXPERT), lambda i, be, act: (be[i], 0, 0)),
                      pl.BlockSpec((None, d, D_EXPERT), lambda i, be, act: (be[i], 0, 0)),
                      pl.BlockSpec((None, D_EXPERT, d), lambda i, be, act: (be[i], 0, 0))],
            out_specs=pl.BlockSpec((MOE_RB, d), lambda i, be, act: (i, 0))),
        compiler_params=_cparams(("arbitrary",)),
    )(plan['blk_e'], plan['active'], xg, wg, wu, wd)
    row = lambda n: pl.BlockSpec((MOE_TB, n), lambda i, *_: (i, 0))
    const = lambda arr: pl.BlockSpec(arr.shape, lambda i, *_: (0,) * arr.ndim)
    return pl.pallas_call(
        _moe_combine_body,
        name="moe_combine",
        out_shape=jax.ShapeDtypeStruct((t, d), F32),
        grid_spec=pltpu.PrefetchScalarGridSpec(
            num_scalar_prefetch=4, grid=(nb,),
            in_specs=[pl.BlockSpec(memory_space=pl.ANY), row(TOP_K), row(TOP_K), row(d),
                      const(sg), const(su), const(sd), row(d), const(gpost),
                      pl.BlockSpec((None, None, 1, d), lambda i, *_: (seg_fn(i), 5, 0, 0))],
            out_specs=row(d),
            scratch_shapes=[pltpu.VMEM((2, MOE_CAP, d), MX), pltpu.SemaphoreType.DMA((2,))]),
        compiler_params=_cparams(("arbitrary",)),
    )(*tables, yg, plan['dest'], wts, x, sg, su, sd, h, gpost, mod4)


def _rope_tables(seq_len, n_ident):
    rows = seq_len // GRID_W
    row = jnp.repeat(jnp.arange(rows, dtype=F32), GRID_W)
    col = jnp.tile(jnp.arange(GRID_W, dtype=F32), rows)
    axis_dim = 32
    inv_freq = ROPE_THETA ** (-jnp.arange(0, axis_dim, 2, dtype=F32) / axis_dim)
    ang = jnp.concatenate([row[:, None] * inv_freq, col[:, None] * inv_freq], axis=-1)
    cos, sin = jnp.cos(ang), jnp.sin(ang)
    cos64 = jnp.concatenate([cos, cos], axis=-1)
    sin64 = jnp.concatenate([-sin, sin], axis=-1)
    cos64 = jnp.concatenate([jnp.ones((n_ident, 64), F32), cos64], axis=0)
    sin64 = jnp.concatenate([jnp.zeros((n_ident, 64), F32), sin64], axis=0)
    return cos64, sin64


def _even_weights(w_in):
    d = w_in.shape[0]
    cuts = np.cumsum([GLA_QK, GLA_QK, GLA_VW, GLA_VW, GLA_RANK, GLA_RANK, SWA_W,
                      SWA_KV_HEADS * SWA_HD])
    q, k, v, g, af, ab, sq, sk, sv = jnp.split(w_in, [int(c) for c in cuts], axis=1)
    a = jnp.concatenate([af, ab, jnp.zeros((d, 128 - 2 * GLA_RANK), w_in.dtype)], axis=1)

    def rep(wkv):
        return jnp.tile(wkv.reshape(d, SWA_KV_HEADS, 1, SWA_HD), (1, 1, SWA_G, 1)).reshape(d, SWA_W)

    return jnp.concatenate([q, k, v, g, a, sq, rep(sk), rep(sv)], axis=1).astype(MX)


def _layer_tail(h, l, mod4, seg_fn, p):
    x, idx, wts = _ffn_pre(h, p['g_ffn_pre'][l][None], mod4, p['w_router'][l], p['b_router'][l][None], seg_fn)
    return _moe(x, idx, wts, p['w_exp_gate'][l].astype(MX), p['w_exp_up'][l].astype(MX),
                p['w_exp_down'][l].astype(MX), p['w_sh_gate'][l].astype(MX),
                p['w_sh_up'][l].astype(MX), p['w_sh_down'][l].astype(MX),
                h, p['g_ffn_post'][l][None], mod4, seg_fn)


def kernel(x, c, ctx, c_ctx, w_mod, b_mod, g_mix_pre, g_mix_post, g_ffn_pre, g_ffn_post, w_in_e, gla_wa2, gla_ba, gla_norm, swa_sink, w_out_e, w_in_o, mla_q_norm, mla_kv_norm, w_uq, w_ukv, w_out_o, w_router, b_router, w_exp_gate, w_exp_up, w_exp_down, w_sh_gate, w_sh_up, w_sh_down):
    p = dict(g_ffn_pre=g_ffn_pre, g_ffn_post=g_ffn_post, w_router=w_router, b_router=b_router,
             w_exp_gate=w_exp_gate, w_exp_up=w_exp_up, w_exp_down=w_exp_down,
             w_sh_gate=w_sh_gate, w_sh_up=w_sh_up, w_sh_down=w_sh_down)
    batch, seq_len, d = x.shape
    ctx_len = ctx.shape[1]
    depth = w_mod.shape[0]
    assert depth == 2 and batch * ctx_len == ROW_TILE and seq_len % ROW_TILE == 0
    lat_blocks = seq_len // ROW_TILE

    cvec = jnp.concatenate([c, c_ctx[None], jnp.zeros((8 - batch - 1, d), F32)], axis=0)
    mod = _mod_vectors(cvec, w_mod, b_mod)
    h = jnp.concatenate([ctx.reshape(batch * ctx_len, d), x.reshape(batch * seq_len, d)], axis=0)

    seg_all = lambda i: jnp.where(i == 0, batch, (i - 1) // lat_blocks)
    seg_lat = lambda i: i // lat_blocks
    pos_all = lambda i: jnp.where(i == 0, 0, 1 + (i - 1) % lat_blocks)
    cos64, sin64 = _rope_tables(seq_len, ROW_TILE)

    mod4 = mod[0].reshape(8, 6, 1, d)
    q, k, v, gg, a, sq, skr, svr = _even_in(
        h, g_mix_pre[0][None], mod4, _even_weights(w_in_e[0]),
        jnp.tile(cos64, (1, SWA_HEADS)), jnp.tile(sin64, (1, SWA_HEADS)), seg_all, pos_all)
    wa = jnp.zeros((2, 128, GLA_QK), F32)
    wa = wa.at[0, :GLA_RANK].set(gla_wa2[0, 0]).at[1, GLA_RANK:2 * GLA_RANK].set(gla_wa2[0, 1])
    o_f, o_b = _gla(q, k, v, a, wa, gla_ba[0][:, None, :], batch, ctx_len, seq_len)
    a_swa = _swa(sq, skr, svr, swa_sink[0], batch, ctx_len, seq_len)
    w_out = w_out_e[0].astype(MX)
    h = _even_out(h, o_f, o_b, gg, a_swa, gla_norm[0][None], w_out[:GLA_VW], w_out[GLA_VW:],
                  g_mix_post[0][None], mod4, seg_all)
    h = _layer_tail(h, 0, mod4, seg_all, p)

    mod4 = mod[1].reshape(8, 6, 1, d)
    ones = jnp.ones_like(cos64)
    cos_h = jnp.concatenate([ones, ones, cos64, ones], axis=1)
    sin_h = jnp.concatenate([0 * ones, 0 * ones, sin64, 0 * ones], axis=1)
    w_in = jnp.concatenate([w_in_o[0], jnp.zeros((d, 128 - MLA_ROPE), F32)], axis=1).astype(MX)
    cq, ckv, kr = _odd_in(h, g_mix_pre[1][None], mod4, w_in, mla_q_norm[0][None],
                          mla_kv_norm[0][None], cos_h, sin_h, seg_all, pos_all)
    w_q = w_uq[0].reshape(MLA_Q_RANK, MLA_HEADS, MLA_NOPE + MLA_ROPE)
    w_q = jnp.pad(w_q, ((0, 0), (0, 0), (0, MLA_QK_PAD - MLA_NOPE - MLA_ROPE)))
    w_q = w_q.reshape(MLA_Q_RANK, MLA_HEADS * MLA_QK_PAD).astype(MX)
    n_lat = batch * seq_len
    qh = _q_up(cq, 1, n_lat, w_q, cos_h, sin_h, pos_all)
    w_kv = w_ukv[0].astype(MX)
    k_lat, v_lat = _kv_up(ckv, kr, 1, n_lat, ROW_TILE, w_kv)
    k_ctx, v_ctx = _kv_up(ckv, kr, 0, batch * ctx_len, ctx_len, w_kv)
    o = _mla_attn(qh, k_ctx, v_ctx, k_lat, v_lat, batch, ctx_len, seq_len)
    hl = _odd_out(h, 1, o, w_out_o[0].astype(MX), g_mix_post[1][None], mod4, seg_lat)
    hl = _layer_tail(hl, 1, mod4, seg_lat, p)
    return hl.reshape(batch, seq_len, d)
```

```python
import functools
import math

import jax
import jax.numpy as jnp
import numpy as np
from jax import lax
from jax.experimental import pallas as pl
from jax.experimental.pallas import tpu as pltpu

F32 = jnp.float32
MX = jnp.bfloat16
HI = lax.Precision.HIGHEST

GRID_W = 64
ROPE_THETA = 10000.0
RMS_EPS = 1e-6

GLA_HEADS, GLA_DK, GLA_DV, GLA_RANK, GLA_TAU = 4, 64, 128, 16, 16.0
GLA_QK = GLA_HEADS * GLA_DK
GLA_VW = GLA_HEADS * GLA_DV
GLA_CHUNK = 128

SWA_HEADS, SWA_KV_HEADS, SWA_HD, WINDOW = 8, 2, 64, 128
SWA_G = SWA_HEADS // SWA_KV_HEADS
SWA_W = SWA_HEADS * SWA_HD

MLA_HEADS, MLA_Q_RANK, MLA_KV_RANK = 8, 512, 256
MLA_NOPE, MLA_ROPE, MLA_V = 128, 64, 128
MLA_QK_PAD = 256
MLA_SCALE = (MLA_NOPE + MLA_ROPE) ** -0.5

N_EXPERTS, TOP_K, N_GROUPS, TOPK_GROUPS = 64, 8, 8, 4
GROUP_SIZE = N_EXPERTS // N_GROUPS
D_EXPERT = 256
ROUTED_SCALE = 2.5

ROW_TILE = 512
VMEM_LIMIT = 56 * 1024 * 1024


def _cparams(sem):
    return pltpu.CompilerParams(dimension_semantics=sem, vmem_limit_bytes=VMEM_LIMIT)


def _sigmoid(x):
    return 1.0 / (1.0 + jnp.exp(-x))


def _silu(x):
    return x * _sigmoid(x)


def _rms(x, g):
    ms = jnp.mean(x * x, axis=-1, keepdims=True)
    return x * lax.rsqrt(ms + RMS_EPS) * g


def _modulate(h, g, shift, scale):
    return _rms(h, g) * (1.0 + scale) + shift


def _dot(a, b):
    return jnp.dot(a, b, preferred_element_type=F32)


def _dot_t(a, b):
    return lax.dot_general(a, b, (((1,), (1,)), ((), ())), preferred_element_type=F32)


def _rope(x, cos, sin_signed):
    n = x.shape[-1]
    lane = lax.broadcasted_iota(jnp.int32, x.shape, 1)
    first = (lane % 64) < 32
    partner = jnp.where(first, pltpu.roll(x, n - 32, 1), pltpu.roll(x, 32, 1))
    return x * cos + partner * sin_signed


def _mod_body(c_ref, w_ref, b_ref, o_ref):
    s = _silu(c_ref[...])
    o_ref[0] = jnp.dot(s, w_ref[0], precision=HI, preferred_element_type=F32) + b_ref[0]


def _mod_vectors(cvec, w_mod, b_mod):
    depth, d, n = w_mod.shape
    tn = 1536
    return pl.pallas_call(
        _mod_body,
        name="mod_vectors",
        out_shape=jax.ShapeDtypeStruct((depth, 8, n), F32),
        grid=(depth, n // tn),
        in_specs=[pl.BlockSpec((8, d), lambda l, j: (0, 0)),
                  pl.BlockSpec((1, d, tn), lambda l, j: (l, 0, j)),
                  pl.BlockSpec((1, 1, tn), lambda l, j: (l, 0, j))],
        out_specs=pl.BlockSpec((1, 8, tn), lambda l, j: (l, 0, j)),
        compiler_params=_cparams(("arbitrary", "arbitrary")),
    )(cvec, w_mod, b_mod.reshape(depth, 1, n))


def _mod_spec(seg_fn, which, d):
    return pl.BlockSpec((None, None, 1, d), lambda i, *_: (seg_fn(i), which, 0, 0))


def _full_spec(arr):
    nd = arr.ndim
    return pl.BlockSpec(arr.shape, lambda *_: (0,) * nd)


_EV_Q, _EV_K, _EV_V, _EV_G, _EV_A, _EV_SQ, _EV_SK, _EV_SV, _EV_END = (
    0, 256, 512, 1024, 1536, 1664, 2176, 2688, 3200)


def _even_in_body(h_ref, g_ref, sh_ref, sc_ref, w_ref, cos_ref, sin_ref,
                  q_ref, k_ref, v_ref, gg_ref, a_ref, sq_ref, sk_ref, sv_ref):
    u = _modulate(h_ref[...], g_ref[...], sh_ref[...], sc_ref[...]).astype(MX)
    q_ref[...] = _dot(u, w_ref[:, _EV_Q:_EV_K]) * (GLA_DK ** -0.5)
    k_ref[...] = _dot(u, w_ref[:, _EV_K:_EV_V])
    v_ref[...] = _dot(u, w_ref[:, _EV_V:_EV_G])
    gg_ref[...] = _dot(u, w_ref[:, _EV_G:_EV_A])
    a_ref[...] = _dot(u, w_ref[:, _EV_A:_EV_SQ])
    cos = cos_ref[...]
    sin = sin_ref[...]
    sq = _rope(_dot(u, w_ref[:, _EV_SQ:_EV_SK]), cos, sin)
    sq_ref[...] = (sq * (SWA_HD ** -0.5)).astype(sq_ref.dtype)
    sk_ref[...] = _rope(_dot(u, w_ref[:, _EV_SK:_EV_SV]), cos, sin).astype(sk_ref.dtype)
    sv_ref[...] = _dot(u, w_ref[:, _EV_SV:_EV_END]).astype(sv_ref.dtype)


def _even_in(h, g_pre, mod4, w_cat, cos_t, sin_t, seg_fn, pos_fn):
    t, d = h.shape
    tm = ROW_TILE
    row = lambda n: pl.BlockSpec((tm, n), lambda i: (i, 0))
    outs = [(GLA_QK, F32), (GLA_QK, F32), (GLA_VW, F32), (GLA_VW, F32), (128, F32),
            (SWA_W, MX), (SWA_W, MX), (SWA_W, MX)]
    return pl.pallas_call(
        _even_in_body,
        name="even_in",
        out_shape=[jax.ShapeDtypeStruct((t, n), dt) for n, dt in outs],
        grid=(t // tm,),
        in_specs=[row(d), _full_spec(g_pre), _mod_spec(seg_fn, 0, d), _mod_spec(seg_fn, 1, d),
                  _full_spec(w_cat),
                  pl.BlockSpec((tm, SWA_W), lambda i: (pos_fn(i), 0)),
                  pl.BlockSpec((tm, SWA_W), lambda i: (pos_fn(i), 0))],
        out_specs=[row(n) for n, _ in outs],
        compiler_params=_cparams(("parallel",)),
    )(h, g_pre, mod4, mod4, w_cat, cos_t, sin_t)


def _gla_consts(c, reverse):
    nlev = int(round(math.log2(c)))
    idx = np.arange(c)
    i, m = idx[:, None], idx[None, :]
    sizes = [c >> (l + 1) for l in range(nlev)]
    mats = [m <= i, m > i]
    for s in sizes:
        mats.append((m >= (i // s) * s) & (m <= i))
    for s in sizes:
        mats.append((m > i) & (m <= (i // s) * s + s - 1))
    a = np.stack(mats).astype(np.float32)
    masks = [np.eye(c, dtype=bool)]
    for s in sizes:
        bi, bj = i // s, m // s
        masks.append((bi % 2 == 1) & (bj == bi - 1))
    msk = np.stack(masks).astype(np.float32)
    if reverse:
        a = a[:, ::-1, ::-1]
        msk = msk[:, ::-1, ::-1]
    return (np.ascontiguousarray(a.reshape(-1, c)),
            np.ascontiguousarray(np.tile(msk, (1, 1, GLA_HEADS))))


def _split3(x):
    hi = x.astype(MX)
    r1 = x - hi.astype(F32)
    mid = r1.astype(MX)
    lo = (r1 - mid.astype(F32)).astype(MX)
    return hi, mid, lo


def _gla_direction(q, k, v, a, wa, ba, amat, lmask, hmask, vmask, bdmask, st_ref, d, last_row):
    c = q.shape[0]
    nlev = int(round(math.log2(c)))
    x = jnp.dot(a, wa, precision=HI, preferred_element_type=F32) + ba
    la = (jnp.minimum(x, 0.0) - jnp.log1p(jnp.exp(-jnp.abs(x)))) * (1.0 / GLA_TAU)
    hi, mid, lo = _split3(la)
    am = amat.astype(MX)
    ex = jnp.exp(_dot(am, hi) + _dot(am, mid) + _dot(am, lo))
    qd = (q * ex[0:c]).astype(MX)
    kd = (k * ex[c:2 * c]).astype(MX)
    st = st_ref[d]
    o = _dot_t(qd, st.astype(MX))
    scat = jnp.zeros((c, GLA_HEADS * c), F32)
    for lev in range(nlev + 1):
        if lev == 0:
            ql, kl = q, k
        else:
            ql = q * ex[(1 + lev) * c:(2 + lev) * c]
            kl = k * ex[(1 + nlev + lev) * c:(2 + nlev + lev) * c]
        kst = (jnp.concatenate([kl] * GLA_HEADS, axis=0) * hmask).astype(MX)
        scat = scat + _dot_t(ql.astype(MX), kst) * lmask[lev]
    vbd = (jnp.concatenate([v] * GLA_HEADS, axis=0) * vmask).astype(MX)
    o = o + _dot(scat.astype(MX), vbd)
    upd = lax.dot_general(v.astype(MX), kd, (((0,), (0,)), ((), ())), preferred_element_type=F32)
    st_ref[d] = st * ex[last_row:last_row + 1] + upd * bdmask
    return o


def _gla_body(qf_ref, kf_ref, vf_ref, af_ref, qb_ref, kb_ref, vb_ref, ab_ref,
              wa_ref, ba_ref, amf_ref, lmf_ref, amb_ref, lmb_ref, hm_ref, vm_ref, bd_ref,
              of_ref, ob_ref, st_ref):
    @pl.when(pl.program_id(1) == 0)
    def _():
        st_ref[...] = jnp.zeros_like(st_ref)

    c = qf_ref.shape[0]
    hm, vm, bd = hm_ref[...], vm_ref[...], bd_ref[...]
    of_ref[...] = _gla_direction(qf_ref[...], kf_ref[...], vf_ref[...], af_ref[...],
                                 wa_ref[0], ba_ref[0], amf_ref[...], lmf_ref, hm, vm, bd,
                                 st_ref, 0, c - 1)
    ob_ref[...] = _gla_direction(qb_ref[...], kb_ref[...], vb_ref[...], ab_ref[...],
                                 wa_ref[1], ba_ref[1], amb_ref[...], lmb_ref, hm, vm, bd,
                                 st_ref, 1, 0)


def _gla(q, k, v, a, wa, ba, batch, ctx_len, seq_len):
    t = q.shape[0]
    c = GLA_CHUNK
    nc, nl = ctx_len // c, seq_len // c
    amf, lmf = _gla_consts(c, False)
    amb, lmb = _gla_consts(c, True)
    r = np.arange(GLA_HEADS * c)[:, None] // c
    hm = (r == np.arange(GLA_QK)[None, :] // GLA_DK).astype(np.float32)
    vm = (r == np.arange(GLA_VW)[None, :] // GLA_DV).astype(np.float32)
    bd = (np.arange(GLA_VW)[:, None] // GLA_DV
          == np.arange(GLA_QK)[None, :] // GLA_DK).astype(np.float32)

    def fwd(b, s):
        return jnp.where(s < nc, nc * b + s, batch * nc + nl * b + (s - nc))

    def bwd(b, s):
        return jnp.where(s < nc, nc * b + (nc - 1 - s), batch * nc + nl * b + (nl - 1 - (s - nc)))

    def chunk(n, fn):
        return pl.BlockSpec((c, n), lambda b, s: (fn(b, s), 0))

    consts = [jnp.asarray(z) for z in (amf, lmf, amb, lmb, hm, vm, bd)]
    ins = [q, k, v, a, q, k, v, a, wa, ba] + consts
    specs = ([chunk(GLA_QK, fwd), chunk(GLA_QK, fwd), chunk(GLA_VW, fwd), chunk(128, fwd),
              chunk(GLA_QK, bwd), chunk(GLA_QK, bwd), chunk(GLA_VW, bwd), chunk(128, bwd)]
             + [_full_spec(z) for z in ins[8:]])
    return pl.pallas_call(
        _gla_body,
        name="gla_scan",
        out_shape=[jax.ShapeDtypeStruct((t, GLA_VW), F32)] * 2,
        grid=(batch, nc + nl),
        in_specs=specs,
        out_specs=[chunk(GLA_VW, fwd), chunk(GLA_VW, bwd)],
        scratch_shapes=[pltpu.VMEM((2, GLA_VW, GLA_QK), F32)],
        compiler_params=_cparams(("parallel", "arbitrary")),
    )(*ins)


def _swa_heads(q, kcat, vcat, valid, sink_ref, h):
    lane_head = lax.broadcasted_iota(jnp.int32, (1, SWA_G * SWA_HD), 1) // SWA_HD
    acc = jnp.zeros((q.shape[0], SWA_G * SWA_HD), F32)
    for g in range(SWA_G):
        hm = lane_head == g
        s = _dot_t(jnp.where(hm, q, jnp.zeros_like(q)), kcat)
        if valid is not None:
            s = jnp.where(valid, s, -jnp.inf)
        sk = sink_ref[h * SWA_G + g]
        m = jnp.maximum(jnp.max(s, axis=-1, keepdims=True), sk)
        p = jnp.exp(s - m)
        den = jnp.sum(p, axis=-1, keepdims=True) + jnp.exp(sk - m)
        og = _dot(p.astype(MX), jnp.where(hm, vcat, jnp.zeros_like(vcat)))
        acc = acc + og * (1.0 / den)
    return acc


def _swa_latent_body(sink_ref, q_ref, kc_ref, vc_ref, kp_ref, k0_ref, kn_ref,
                     vp_ref, v0_ref, vn_ref, o_ref, *, nblk):
    h, n = pl.program_id(1), pl.program_id(2)
    blk = q_ref.shape[0]
    nctx = kc_ref.shape[0]
    kcat = jnp.concatenate([kc_ref[...], kp_ref[...], k0_ref[...], kn_ref[...]], axis=0)
    vcat = jnp.concatenate([vc_ref[...], vp_ref[...], v0_ref[...], vn_ref[...]], axis=0)
    shape = (blk, nctx + 3 * blk)
    qi = lax.broadcasted_iota(jnp.int32, shape, 0)
    col = lax.broadcasted_iota(jnp.int32, shape, 1)
    si = col - nctx
    kpos = (n - 1) * blk + si
    valid = (col < nctx) | ((jnp.abs(si - blk - qi) <= WINDOW) & (kpos >= 0) & (kpos < nblk * blk))
    o_ref[...] = _swa_heads(q_ref[...], kcat, vcat, valid, sink_ref, h).astype(o_ref.dtype)


def _swa_ctx_body(sink_ref, q_ref, kc_ref, vc_ref, o_ref):
    h = pl.program_id(1)
    o_ref[...] = _swa_heads(q_ref[...], kc_ref[...], vc_ref[...], None, sink_ref, h).astype(o_ref.dtype)


def _swa(sq, skr, svr, sink, batch, ctx_len, seq_len):
    t = sq.shape[0]
    blk = WINDOW
    w = SWA_G * SWA_HD
    nb = seq_len // blk
    cb = ctx_len // blk
    lat0 = batch * cb
    smem = pl.BlockSpec(memory_space=pltpu.SMEM)
    ctx_kv = pl.BlockSpec((ctx_len, w), lambda b, h, n: (b, h))

    def win(off):
        return pl.BlockSpec((blk, w), lambda b, h, n: (lat0 + b * nb + jnp.clip(n + off, 0, nb - 1), h))

    lat = pl.pallas_call(
        functools.partial(_swa_latent_body, nblk=nb),
        name="swa_latent",
        out_shape=jax.ShapeDtypeStruct((batch * seq_len, SWA_W), MX),
        grid=(batch, SWA_KV_HEADS, nb),
        in_specs=[smem, pl.BlockSpec((blk, w), lambda b, h, n: (lat0 + b * nb + n, h)),
                  ctx_kv, ctx_kv, win(-1), win(0), win(1), win(-1), win(0), win(1)],
        out_specs=pl.BlockSpec((blk, w), lambda b, h, n: (b * nb + n, h)),
        compiler_params=_cparams(("parallel", "parallel", "arbitrary")),
    )(sink, sq, skr, svr, skr, skr, skr, svr, svr, svr)
    ctx = pl.pallas_call(
        _swa_ctx_body,
        name="swa_ctx",
        out_shape=jax.ShapeDtypeStruct((batch * ctx_len, SWA_W), MX),
        grid=(batch, SWA_KV_HEADS, cb),
        in_specs=[smem, pl.BlockSpec((blk, w), lambda b, h, n: (b * cb + n, h)), ctx_kv, ctx_kv],
        out_specs=pl.BlockSpec((blk, w), lambda b, h, n: (b * cb + n, h)),
        compiler_params=_cparams(("parallel", "parallel", "arbitrary")),
    )(sink, sq, skr, svr)
    return jnp.concatenate([ctx, lat], axis=0)


def _residual(h, y, gpost, gate):
    return h + gate * _rms(y, gpost)


def _even_out_body(h_ref, of_ref, ob_ref, gg_ref, a_ref, gn_ref, w1_ref, w2_ref, gpost_ref,
                   gate_ref, o_ref):
    o = of_ref[...] + ob_ref[...]
    gn = gn_ref[...]
    parts = [_rms(o[:, j * GLA_DV:(j + 1) * GLA_DV], gn) for j in range(GLA_HEADS)]
    gl = jnp.concatenate(parts, axis=-1) * _silu(gg_ref[...])
    y = _dot(gl.astype(MX), w1_ref[...]) + _dot(a_ref[...], w2_ref[...])
    o_ref[...] = _residual(h_ref[...], y, gpost_ref[...], gate_ref[...])


def _even_out(h, o_f, o_b, gg, a_swa, gn, w1, w2, gpost, mod4, seg_fn):
    t, d = h.shape
    tm = ROW_TILE
    row = lambda n: pl.BlockSpec((tm, n), lambda i: (i, 0))
    return pl.pallas_call(
        _even_out_body,
        name="even_out",
        out_shape=jax.ShapeDtypeStruct((t, d), F32),
        grid=(t // tm,),
        in_specs=[row(d), row(GLA_VW), row(GLA_VW), row(GLA_VW), row(SWA_W), _full_spec(gn),
                  _full_spec(w1), _full_spec(w2), _full_spec(gpost), _mod_spec(seg_fn, 2, d)],
        out_specs=row(d),
        compiler_params=_cparams(("parallel",)),
    )(h, o_f, o_b, gg, a_swa, gn, w1, w2, gpost, mod4)


def _odd_out_body(h_ref, o_ref_in, w_ref, gpost_ref, gate_ref, o_ref):
    y = _dot(o_ref_in[...], w_ref[...])
    o_ref[...] = _residual(h_ref[...], y, gpost_ref[...], gate_ref[...])


def _odd_out(h, h_off, o, w, gpost, mod4, seg_fn):
    t, d = o.shape[0], h.shape[1]
    tm = ROW_TILE
    row = lambda n: pl.BlockSpec((tm, n), lambda i: (i, 0))
    return pl.pallas_call(
        _odd_out_body,
        name="odd_out",
        out_shape=jax.ShapeDtypeStruct((t, d), F32),
        grid=(t // tm,),
        in_specs=[pl.BlockSpec((tm, d), lambda i: (i + h_off, 0)), row(o.shape[1]),
                  _full_spec(w), _full_spec(gpost), _mod_spec(seg_fn, 2, d)],
        out_specs=row(d),
        compiler_params=_cparams(("parallel",)),
    )(h, o, w, gpost, mod4)


def _odd_in_body(h_ref, g_ref, sh_ref, sc_ref, w_ref, qn_ref, kvn_ref, cos_ref, sin_ref,
                 cq_ref, ckv_ref, kr_ref):
    u = _modulate(h_ref[...], g_ref[...], sh_ref[...], sc_ref[...]).astype(MX)
    cq_ref[...] = _rms(_dot(u, w_ref[:, 0:MLA_Q_RANK]), qn_ref[...]).astype(cq_ref.dtype)
    c1 = MLA_Q_RANK + MLA_KV_RANK
    ckv_ref[...] = _rms(_dot(u, w_ref[:, MLA_Q_RANK:c1]), kvn_ref[...]).astype(ckv_ref.dtype)
    kr = _dot(u, w_ref[:, c1:c1 + 128])
    kr_ref[...] = _rope(kr, cos_ref[...], sin_ref[...]).astype(kr_ref.dtype)


def _odd_in(h, g_pre, mod4, w_cat, qn, kvn, cos_t, sin_t, seg_fn, pos_fn):
    t, d = h.shape
    tm = ROW_TILE
    row = lambda n: pl.BlockSpec((tm, n), lambda i: (i, 0))
    tab = pl.BlockSpec((tm, 128), lambda i: (pos_fn(i), 1))
    return pl.pallas_call(
        _odd_in_body,
        name="odd_in",
        out_shape=[jax.ShapeDtypeStruct((t, MLA_Q_RANK), MX),
                   jax.ShapeDtypeStruct((t, MLA_KV_RANK), MX),
                   jax.ShapeDtypeStruct((t, 128), MX)],
        grid=(t // tm,),
        in_specs=[row(d), _full_spec(g_pre), _mod_spec(seg_fn, 0, d), _mod_spec(seg_fn, 1, d),
                  _full_spec(w_cat), _full_spec(qn), _full_spec(kvn), tab, tab],
        out_specs=[row(MLA_Q_RANK), row(MLA_KV_RANK), row(128)],
        compiler_params=_cparams(("parallel",)),
    )(h, g_pre, mod4, mod4, w_cat, qn, kvn, cos_t, sin_t)


def _q_up_body(cq_ref, w_ref, cos_ref, sin_ref, o_ref):
    z = _rope(_dot(cq_ref[...], w_ref[...]), cos_ref[...], sin_ref[...])
    o_ref[...] = (z * (MLA_SCALE * math.log2(math.e))).astype(o_ref.dtype)


def _q_up(cq, row_off, nrows, w_pad, cos_t, sin_t, pos_fn):
    tm = ROW_TILE
    return pl.pallas_call(
        _q_up_body,
        name="mla_q_up",
        out_shape=jax.ShapeDtypeStruct((nrows, MLA_HEADS * MLA_QK_PAD), MX),
        grid=(nrows // tm, MLA_HEADS),
        in_specs=[pl.BlockSpec((tm, MLA_Q_RANK), lambda i, h: (i + row_off, 0)),
                  pl.BlockSpec((MLA_Q_RANK, MLA_QK_PAD), lambda i, h: (0, h)),
                  pl.BlockSpec((tm, MLA_QK_PAD), lambda i, h: (pos_fn(i + row_off), 0)),
                  pl.BlockSpec((tm, MLA_QK_PAD), lambda i, h: (pos_fn(i + row_off), 0))],
        out_specs=pl.BlockSpec((tm, MLA_QK_PAD), lambda i, h: (i, h)),
        compiler_params=_cparams(("parallel", "arbitrary")),
    )(cq, w_pad, cos_t, sin_t)


def _kv_up_body(ckv_ref, kr_ref, w_ref, k_ref, v_ref):
    z = _dot(ckv_ref[...], w_ref[...])
    k_ref[...] = jnp.concatenate([z[:, :MLA_NOPE].astype(k_ref.dtype), kr_ref[...]], axis=-1)
    v_ref[...] = z[:, MLA_NOPE:].astype(v_ref.dtype)


def _kv_up(ckv, kr, row_off, nrows, tm, w_ukv):
    return pl.pallas_call(
        _kv_up_body,
        name="mla_kv_up",
        out_shape=[jax.ShapeDtypeStruct((nrows, MLA_HEADS * MLA_QK_PAD), MX),
                   jax.ShapeDtypeStruct((nrows, MLA_HEADS * MLA_V), MX)],
        grid=(nrows // tm, MLA_HEADS),
        in_specs=[pl.BlockSpec((tm, MLA_KV_RANK), lambda i, h: (i + row_off, 0)),
                  pl.BlockSpec((tm, 128), lambda i, h: (i + row_off, 0)),
                  pl.BlockSpec((MLA_KV_RANK, MLA_NOPE + MLA_V), lambda i, h: (0, h))],
        out_specs=[pl.BlockSpec((tm, MLA_QK_PAD), lambda i, h: (i, h)),
                   pl.BlockSpec((tm, MLA_V), lambda i, h: (i, h))],
        compiler_params=_cparams(("parallel", "arbitrary")),
    )(ckv, kr, w_ukv)


def _mla_attn_body(q_ref, kc_ref, vc_ref, k_ref, v_ref, o_ref, *, tk, unroll):
    q = q_ref[...]
    s = _dot_t(q, kc_ref[...])
    m = jnp.max(s, axis=-1, keepdims=True)
    p = jnp.exp2(s - m)
    l = jnp.sum(p, axis=-1, keepdims=True)
    acc = _dot(p.astype(MX), vc_ref[...])

    def step(j, carry):
        m, l, acc = carry
        start = pl.multiple_of(j * tk, tk)
        s = _dot_t(q, k_ref[pl.ds(start, tk), :])
        mn = jnp.maximum(m, jnp.max(s, axis=-1, keepdims=True))
        alpha = jnp.exp2(m - mn)
        p = jnp.exp2(s - mn)
        l = alpha * l + jnp.sum(p, axis=-1, keepdims=True)
        acc = alpha * acc + _dot(p.astype(MX), v_ref[pl.ds(start, tk), :])
        return mn, l, acc

    m, l, acc = lax.fori_loop(0, k_ref.shape[0] // tk, step, (m, l, acc), unroll=unroll)
    o_ref[...] = (acc * (1.0 / l)).astype(o_ref.dtype)


def _mla_attn(q, k_ctx, v_ctx, k_lat, v_lat, batch, ctx_len, seq_len, tq=1024, tk=512, unroll=4):
    nq = seq_len // tq
    return pl.pallas_call(
        functools.partial(_mla_attn_body, tk=tk, unroll=unroll),
        name="mla_attn",
        out_shape=jax.ShapeDtypeStruct((batch * seq_len, MLA_HEADS * MLA_V), MX),
        grid=(batch, MLA_HEADS, nq),
        in_specs=[pl.BlockSpec((tq, MLA_QK_PAD), lambda b, h, i: (b * nq + i, h)),
                  pl.BlockSpec((ctx_len, MLA_QK_PAD), lambda b, h, i: (b, h)),
                  pl.BlockSpec((ctx_len, MLA_V), lambda b, h, i: (b, h)),
                  pl.BlockSpec((seq_len, MLA_QK_PAD), lambda b, h, i: (b, h)),
                  pl.BlockSpec((seq_len, MLA_V), lambda b, h, i: (b, h))],
        out_specs=pl.BlockSpec((tq, MLA_V), lambda b, h, i: (b * nq + i, h)),
        compiler_params=_cparams(("parallel", "parallel", "arbitrary")),
    )(q, k_ctx, v_ctx, k_lat, v_lat)


def _route(scores, sel):
    e, t = sel.shape
    neg = -jnp.inf
    x3 = sel.reshape(N_GROUPS, GROUP_SIZE, t)
    pos = lax.broadcasted_iota(jnp.int32, x3.shape, 1)
    m1 = jnp.max(x3, axis=1, keepdims=True)
    i1 = jnp.min(jnp.where(x3 == m1, pos, GROUP_SIZE), axis=1, keepdims=True)
    m2 = jnp.max(jnp.where(pos == i1, neg, x3), axis=1, keepdims=True)
    gs = m1 + m2
    gid = lax.broadcasted_iota(jnp.int32, gs.shape, 0)
    beaten = jnp.zeros(gs.shape, jnp.int32)
    for g in range(N_GROUPS):
        other = gs[g:g + 1]
        beaten = beaten + jnp.where((other > gs) | ((other == gs) & (g < gid)), 1, 0)
    cur = jnp.where(beaten < TOPK_GROUPS, x3, neg).reshape(e, t)
    row = lax.broadcasted_iota(jnp.int32, (e, t), 0)
    chosen = jnp.zeros((e, t), F32)
    for _ in range(TOP_K):
        m = jnp.max(cur, axis=0, keepdims=True)
        i = jnp.min(jnp.where(cur == m, row, e), axis=0, keepdims=True)
        hit = row == i
        chosen = jnp.where(hit, 1.0, chosen)
        cur = jnp.where(hit, neg, cur)
    w = chosen * scores
    return chosen, w / jnp.sum(w, axis=0, keepdims=True) * ROUTED_SCALE


def _ffn_pre_body(h_ref, g_ref, sh_ref, sc_ref, wr_ref, br_ref, us_ref, ls_ref,
                  v_ref, dst_ref, wt_ref, cpad_ref):
    vl = _modulate(h_ref[...], g_ref[...], sh_ref[...], sc_ref[...])
    v_ref[...] = vl.astype(v_ref.dtype)
    logits = lax.dot_general(wr_ref[...], vl, (((1,), (1,)), ((), ())), precision=HI,
                             preferred_element_type=F32)
    scores = _sigmoid(logits)
    chosen, w = _route(scores, scores + br_ref[...])
    rank = _dot(chosen.astype(MX), us_ref[...])
    cnt = jnp.sum(chosen, axis=1, keepdims=True)
    cpad = jnp.ceil(cnt * (1.0 / MOE_PIECE)) * MOE_PIECE
    cpad_b = jnp.broadcast_to(cpad, (cpad.shape[0], 128))
    loff = _dot(ls_ref[...], cpad_b.astype(MX))
    dst_ref[...] = jnp.where(chosen > 0.0, loff[:, 0:1] + rank, -1.0).astype(jnp.int32)
    wt_ref[...] = w
    cpad_ref[...] = cpad_b.astype(jnp.int32)


def _ffn_pre(h, g_pre, mod4, w_router, b_router, seg_fn):
    t, d = h.shape
    tm = MOE_TB
    nb = t // tm
    row = lambda n: pl.BlockSpec((tm, n), lambda i: (i, 0))
    blk = lambda n: pl.BlockSpec((None, N_EXPERTS, n), lambda i: (i, 0, 0))
    us = jnp.asarray(np.triu(np.ones((tm, tm), np.float32), 1), MX)
    ls = jnp.asarray(np.tril(np.ones((N_EXPERTS, N_EXPERTS), np.float32), -1), MX)
    wr_t = w_router.T
    br_t = b_router.reshape(N_EXPERTS, 1)
    return pl.pallas_call(
        _ffn_pre_body,
        name="ffn_pre_router",
        out_shape=[jax.ShapeDtypeStruct((t, d), MX),
                   jax.ShapeDtypeStruct((nb, N_EXPERTS, tm), jnp.int32),
                   jax.ShapeDtypeStruct((nb, N_EXPERTS, tm), F32),
                   jax.ShapeDtypeStruct((nb, N_EXPERTS, 128), jnp.int32)],
        grid=(nb,),
        in_specs=[row(d), _full_spec(g_pre), _mod_spec(seg_fn, 3, d), _mod_spec(seg_fn, 4, d),
                  _full_spec(wr_t), _full_spec(br_t), _full_spec(us), _full_spec(ls)],
        out_specs=[row(d), blk(tm), blk(tm), blk(128)],
        compiler_params=_cparams(("parallel",)),
    )(h, g_pre, mod4, mod4, wr_t, br_t, us, ls)


MOE_TB = ROW_TILE
MOE_PIECE = 16
MOE_CAP = TOP_K * MOE_TB + N_EXPERTS * MOE_PIECE
MOE_NPIECE = MOE_CAP // MOE_PIECE
MOE_CHUNK = 1024
MOE_PPC = MOE_CHUNK // MOE_PIECE
MOE_RB = 512


def _dispatch_plan(cpad, t):
    nb = cpad.shape[0]
    loff = jnp.cumsum(cpad, axis=1) - cpad
    tot = jnp.sum(cpad, axis=0)
    reg = (tot + MOE_RB - 1) // MOE_RB * MOE_RB
    gend = jnp.cumsum(reg)
    gbase = gend - reg
    goff = gbase[None] + jnp.cumsum(cpad, axis=0) - cpad
    prow = jnp.arange(MOE_NPIECE, dtype=jnp.int32) * MOE_PIECE
    pexp = jnp.sum((prow[None, :, None] >= (loff + cpad)[:, None, :]).astype(jnp.int32), axis=2)
    pexp = jnp.minimum(pexp, N_EXPERTS - 1)
    gdst = jnp.take_along_axis(goff - loff, pexp, axis=1) + prow[None]
    ng = t * TOP_K + nb * N_EXPERTS * (MOE_PIECE - 1) + N_EXPERTS * (MOE_RB - 1)
    ng = -(-ng // MOE_RB) * MOE_RB
    start = jnp.arange(ng // MOE_RB, dtype=jnp.int32) * MOE_RB
    blk_e = jnp.sum((start[:, None] >= gend[None, :]).astype(jnp.int32), axis=1)
    blk_e = jnp.minimum(blk_e, N_EXPERTS - 1)
    active = (start < (gbase + tot)[blk_e]).astype(jnp.int32)
    i32 = lambda z: z.astype(jnp.int32)
    return dict(pexp=i32(pexp), gdst=i32(gdst), ntot=i32(jnp.sum(cpad, axis=1) // MOE_PIECE),
                toff=i32(gbase + tot), tnp=i32((reg - tot) // MOE_PIECE),
                nused=i32(gend[-1:] // MOE_RB), blk_e=blk_e, active=active, ng=ng)


def _onehot_rows(pexp_ref, dst_ref, blk, c, val_of, out_ref):
    rows = lax.broadcasted_iota(jnp.int32, (MOE_PIECE, MOE_TB), 0)
    for p in range(MOE_PPC):
        e = pexp_ref[blk, c * MOE_PPC + p]
        hit = dst_ref[pl.ds(e, 1), :] == rows + (c * MOE_CHUNK + p * MOE_PIECE)
        out_ref[p * MOE_PIECE:(p + 1) * MOE_PIECE, :] = jnp.where(hit, val_of(e), 0.0).astype(out_ref.dtype)


def _repeat(n, fn):
    def body(_, c):
        fn()
        return c

    lax.fori_loop(0, n, body, 0)


def _moe_sort_body(pexp_ref, gdst_ref, ntot_ref, toff_ref, tnp_ref, nused_ref, x_ref, dst_ref,
                   xg_ref, xs_ref, pi_ref, zb_ref, sem, zsem):
    i, nb = pl.program_id(0), pl.num_programs(0)
    slot = i % 2

    def piece_copy(s, lo, go):
        return pltpu.make_async_copy(xs_ref.at[s, pl.ds(lo, MOE_PIECE)],
                                     xg_ref.at[pl.ds(go, MOE_PIECE)], sem.at[s])

    def zero_copy(go):
        return pltpu.make_async_copy(zb_ref.at[pl.ds(0, MOE_PIECE)], xg_ref.at[pl.ds(go, MOE_PIECE)],
                                     zsem.at[0])

    def zero_block(b):
        return pltpu.make_async_copy(zb_ref, xg_ref.at[pl.ds(pl.multiple_of(b * MOE_RB, MOE_RB), MOE_RB)],
                                     zsem.at[1])

    @pl.when(i >= 2)
    def _():
        _repeat(ntot_ref[i - 2], lambda: piece_copy(slot, 0, 0).wait())

    x = x_ref[...]
    for c in range(MOE_CAP // MOE_CHUNK):
        @pl.when(c * MOE_PPC < ntot_ref[i])
        def _():
            _onehot_rows(pexp_ref, dst_ref, i, c, lambda e: 1.0, pi_ref)
            xs_ref[slot, c * MOE_CHUNK:(c + 1) * MOE_CHUNK, :] = _dot(pi_ref[...], x).astype(xs_ref.dtype)

            def issue(p, carry):
                q = c * MOE_PPC + p
                piece_copy(slot, pl.multiple_of(q * MOE_PIECE, MOE_PIECE),
                           pl.multiple_of(gdst_ref[i, q], MOE_PIECE)).start()
                return carry

            lax.fori_loop(0, jnp.minimum(ntot_ref[i] - c * MOE_PPC, MOE_PPC), issue, 0)

    @pl.when(i == nb - 1)
    def _():
        zb_ref[...] = jnp.zeros_like(zb_ref)

        def tail(e, n):
            def piece(p, c):
                zero_copy(pl.multiple_of(toff_ref[e] + p * MOE_PIECE, MOE_PIECE)).start()
                return c

            lax.fori_loop(0, tnp_ref[e], piece, 0)
            return n + tnp_ref[e]

        nz = lax.fori_loop(0, N_EXPERTS, tail, 0)
        nblk = xg_ref.shape[0] // MOE_RB

        def unused(b, c):
            zero_block(b).start()
            return c

        lax.fori_loop(nused_ref[0], nblk, unused, 0)
        _repeat(nz, lambda: zero_copy(0).wait())
        _repeat(nblk - nused_ref[0], lambda: zero_block(0).wait())
        _repeat(ntot_ref[i], lambda: piece_copy(slot, 0, 0).wait())

    @pl.when((i == nb - 1) & (i >= 1))
    def _():
        _repeat(ntot_ref[i - 1], lambda: piece_copy(1 - slot, 0, 0).wait())


def _moe_expert_body(be_ref, act_ref, x_ref, wg_ref, wu_ref, wd_ref, o_ref):
    del be_ref
    i = pl.program_id(0)

    @pl.when(act_ref[i] != 0)
    def _():
        x = x_ref[...]
        hid = _silu(_dot(x, wg_ref[...])) * _dot(x, wu_ref[...])
        o_ref[...] = _dot(hid.astype(MX), wd_ref[...]).astype(o_ref.dtype)

    @pl.when(act_ref[i] == 0)
    def _():
        o_ref[...] = jnp.zeros_like(o_ref)


def _moe_combine_body(pexp_ref, gdst_ref, ntot_ref, yg_ref, dst_ref, wt_ref, x_ref,
                      sg_ref, su_ref, sd_ref, h_ref, gpost_ref, gate_ref, o_ref,
                      ys_ref, pw_ref, acc_ref, sem):
    i, nb = pl.program_id(0), pl.num_programs(0)
    slot = i % 2

    def piece_copy(s, lo, go):
        return pltpu.make_async_copy(yg_ref.at[pl.ds(go, MOE_PIECE)],
                                     ys_ref.at[s, pl.ds(lo, MOE_PIECE)], sem.at[s])

    def fetch(blk, s):
        n = ntot_ref[blk]

        def issue(q, carry):
            piece_copy(s, pl.multiple_of(q * MOE_PIECE, MOE_PIECE),
                       pl.multiple_of(gdst_ref[blk, q], MOE_PIECE)).start()
            return carry

        lax.fori_loop(0, n, issue, 0)

        def clear(q, carry):
            ys_ref[s, pl.ds(pl.multiple_of(q * MOE_PIECE, MOE_PIECE), MOE_PIECE), :] = jnp.zeros(
                (MOE_PIECE, ys_ref.shape[2]), ys_ref.dtype)
            return carry

        lax.fori_loop(n, (n + MOE_PPC - 1) // MOE_PPC * MOE_PPC, clear, 0)

    @pl.when(i == 0)
    def _():
        fetch(0, 0)

    @pl.when(i + 1 < nb)
    def _():
        fetch(i + 1, 1 - slot)

    x = x_ref[...]
    acc_ref[...] = _dot((_silu(_dot(x, sg_ref[...])) * _dot(x, su_ref[...])).astype(MX), sd_ref[...])
    _repeat(ntot_ref[i], lambda: piece_copy(slot, 0, 0).wait())
    for c in range(MOE_CAP // MOE_CHUNK):
        @pl.when(c * MOE_PPC < ntot_ref[i])
        def _():
            _onehot_rows(pexp_ref, dst_ref, i, c, lambda e: wt_ref[pl.ds(e, 1), :], pw_ref)
            acc_ref[...] += lax.dot_general(
                pw_ref[...], ys_ref[slot, c * MOE_CHUNK:(c + 1) * MOE_CHUNK, :],
                (((0,), (0,)), ((), ())), preferred_element_type=F32)
    o_ref[...] = _residual(h_ref[...], acc_ref[...], gpost_ref[...], gate_ref[...])


def _moe(x, dst_t, w_t, cpad, wg, wu, wd, sg, su, sd, h, gpost, mod4, seg_fn):
    t, d = x.shape
    nb = t // MOE_TB
    plan = _dispatch_plan(cpad, t)
    ng = plan['ng']
    tables = (plan['pexp'], plan['gdst'], plan['ntot'])
    etab = pl.BlockSpec((None, N_EXPERTS, MOE_TB), lambda i, *_: (i, 0, 0))
    xg = pl.pallas_call(
        _moe_sort_body,
        name="moe_sort",
        out_shape=jax.ShapeDtypeStruct((ng, d), MX),
        grid_spec=pltpu.PrefetchScalarGridSpec(
            num_scalar_prefetch=6, grid=(nb,),
            in_specs=[pl.BlockSpec((MOE_TB, d), lambda i, *_: (i, 0)), etab],
            out_specs=pl.BlockSpec(memory_space=pl.ANY),
            scratch_shapes=[pltpu.VMEM((2, MOE_CAP, d), MX), pltpu.VMEM((MOE_CHUNK, MOE_TB), MX),
                            pltpu.VMEM((MOE_RB, d), MX),
                            pltpu.SemaphoreType.DMA((2,)), pltpu.SemaphoreType.DMA((2,))]),
        compiler_params=_cparams(("arbitrary",)),
    )(*tables, plan['toff'], plan['tnp'], plan['nused'], x, dst_t)
    yg = pl.pallas_call(
        _moe_expert_body,
        name="moe_experts",
        out_shape=jax.ShapeDtypeStruct((ng, d), MX),
        grid_spec=pltpu.PrefetchScalarGridSpec(
            num_scalar_prefetch=2, grid=(ng // MOE_RB,),
            in_specs=[pl.BlockSpec((MOE_RB, d), lambda i, be, act: (jnp.where(act[i] != 0, i, 0), 0)),
                      pl.BlockSpec((None, d, D_EXPERT), lambda i, be, act: (be[i], 0, 0)),
                      pl.BlockSpec((None, d, D_EXPERT), lambda i, be, act: (be[i], 0, 0)),
                      pl.BlockSpec((None, D_EXPERT, d), lambda i, be, act: (be[i], 0, 0))],
            out_specs=pl.BlockSpec((MOE_RB, d), lambda i, be, act: (i, 0))),
        compiler_params=_cparams(("arbitrary",)),
    )(plan['blk_e'], plan['active'], xg, wg, wu, wd)
    row = lambda n: pl.BlockSpec((MOE_TB, n), lambda i, *_: (i, 0))
    const = lambda arr: pl.BlockSpec(arr.shape, lambda i, *_: (0,) * arr.ndim)
    return pl.pallas_call(
        _moe_combine_body,
        name="moe_combine",
        out_shape=jax.ShapeDtypeStruct((t, d), F32),
        grid_spec=pltpu.PrefetchScalarGridSpec(
            num_scalar_prefetch=3, grid=(nb,),
            in_specs=[pl.BlockSpec(memory_space=pl.ANY), etab, etab, row(d),
                      const(sg), const(su), const(sd), row(d), const(gpost),
                      pl.BlockSpec((None, None, 1, d), lambda i, *_: (seg_fn(i), 5, 0, 0))],
            out_specs=row(d),
            scratch_shapes=[pltpu.VMEM((2, MOE_CAP, d), MX), pltpu.VMEM((MOE_CHUNK, MOE_TB), MX),
                            pltpu.VMEM((MOE_TB, d), F32), pltpu.SemaphoreType.DMA((2,))]),
        compiler_params=_cparams(("arbitrary",)),
    )(*tables, yg, dst_t, w_t, x, sg, su, sd, h, gpost, mod4)


def _rope_tables(seq_len, n_ident):
    rows = seq_len // GRID_W
    row = jnp.repeat(jnp.arange(rows, dtype=F32), GRID_W)
    col = jnp.tile(jnp.arange(GRID_W, dtype=F32), rows)
    axis_dim = 32
    inv_freq = ROPE_THETA ** (-jnp.arange(0, axis_dim, 2, dtype=F32) / axis_dim)
    ang = jnp.concatenate([row[:, None] * inv_freq, col[:, None] * inv_freq], axis=-1)
    cos, sin = jnp.cos(ang), jnp.sin(ang)
    cos64 = jnp.concatenate([cos, cos], axis=-1)
    sin64 = jnp.concatenate([-sin, sin], axis=-1)
    cos64 = jnp.concatenate([jnp.ones((n_ident, 64), F32), cos64], axis=0)
    sin64 = jnp.concatenate([jnp.zeros((n_ident, 64), F32), sin64], axis=0)
    return cos64, sin64


def _even_weights(w_in):
    d = w_in.shape[0]
    cuts = np.cumsum([GLA_QK, GLA_QK, GLA_VW, GLA_VW, GLA_RANK, GLA_RANK, SWA_W,
                      SWA_KV_HEADS * SWA_HD])
    q, k, v, g, af, ab, sq, sk, sv = jnp.split(w_in, [int(c) for c in cuts], axis=1)
    a = jnp.concatenate([af, ab, jnp.zeros((d, 128 - 2 * GLA_RANK), w_in.dtype)], axis=1)

    def rep(wkv):
        return jnp.tile(wkv.reshape(d, SWA_KV_HEADS, 1, SWA_HD), (1, 1, SWA_G, 1)).reshape(d, SWA_W)

    return jnp.concatenate([q, k, v, g, a, sq, rep(sk), rep(sv)], axis=1).astype(MX)


def _layer_tail(h, l, mod4, seg_fn, p):
    x, dst_t, w_t, cpad = _ffn_pre(h, p['g_ffn_pre'][l][None], mod4, p['w_router'][l], p['b_router'][l], seg_fn)
    return _moe(x, dst_t, w_t, cpad[:, :, 0], p['w_exp_gate'][l].astype(MX), p['w_exp_up'][l].astype(MX),
                p['w_exp_down'][l].astype(MX), p['w_sh_gate'][l].astype(MX),
                p['w_sh_up'][l].astype(MX), p['w_sh_down'][l].astype(MX),
                h, p['g_ffn_post'][l][None], mod4, seg_fn)


def kernel(x, c, ctx, c_ctx, w_mod, b_mod, g_mix_pre, g_mix_post, g_ffn_pre, g_ffn_post, w_in_e, gla_wa2, gla_ba, gla_norm, swa_sink, w_out_e, w_in_o, mla_q_norm, mla_kv_norm, w_uq, w_ukv, w_out_o, w_router, b_router, w_exp_gate, w_exp_up, w_exp_down, w_sh_gate, w_sh_up, w_sh_down):
    p = dict(g_ffn_pre=g_ffn_pre, g_ffn_post=g_ffn_post, w_router=w_router, b_router=b_router,
             w_exp_gate=w_exp_gate, w_exp_up=w_exp_up, w_exp_down=w_exp_down,
             w_sh_gate=w_sh_gate, w_sh_up=w_sh_up, w_sh_down=w_sh_down)
    batch, seq_len, d = x.shape
    ctx_len = ctx.shape[1]
    depth = w_mod.shape[0]
    assert depth == 2 and batch * ctx_len == ROW_TILE and seq_len % ROW_TILE == 0
    lat_blocks = seq_len // ROW_TILE

    cvec = jnp.concatenate([c, c_ctx[None], jnp.zeros((8 - batch - 1, d), F32)], axis=0)
    mod = _mod_vectors(cvec, w_mod, b_mod)
    h = jnp.concatenate([ctx.reshape(batch * ctx_len, d), x.reshape(batch * seq_len, d)], axis=0)

    seg_all = lambda i: jnp.where(i == 0, batch, (i - 1) // lat_blocks)
    seg_lat = lambda i: i // lat_blocks
    pos_all = lambda i: jnp.where(i == 0, 0, 1 + (i - 1) % lat_blocks)
    cos64, sin64 = _rope_tables(seq_len, ROW_TILE)

    mod4 = mod[0].reshape(8, 6, 1, d)
    q, k, v, gg, a, sq, skr, svr = _even_in(
        h, g_mix_pre[0][None], mod4, _even_weights(w_in_e[0]),
        jnp.tile(cos64, (1, SWA_HEADS)), jnp.tile(sin64, (1, SWA_HEADS)), seg_all, pos_all)
    wa = jnp.zeros((2, 128, GLA_QK), F32)
    wa = wa.at[0, :GLA_RANK].set(gla_wa2[0, 0]).at[1, GLA_RANK:2 * GLA_RANK].set(gla_wa2[0, 1])
    o_f, o_b = _gla(q, k, v, a, wa, gla_ba[0][:, None, :], batch, ctx_len, seq_len)
    a_swa = _swa(sq, skr, svr, swa_sink[0], batch, ctx_len, seq_len)
    w_out = w_out_e[0].astype(MX)
    h = _even_out(h, o_f, o_b, gg, a_swa, gla_norm[0][None], w_out[:GLA_VW], w_out[GLA_VW:],
                  g_mix_post[0][None], mod4, seg_all)
    h = _layer_tail(h, 0, mod4, seg_all, p)

    mod4 = mod[1].reshape(8, 6, 1, d)
    ones = jnp.ones_like(cos64)
    cos_h = jnp.concatenate([ones, ones, cos64, ones], axis=1)
    sin_h = jnp.concatenate([0 * ones, 0 * ones, sin64, 0 * ones], axis=1)
    w_in = jnp.concatenate([w_in_o[0], jnp.zeros((d, 128 - MLA_ROPE), F32)], axis=1).astype(MX)
    cq, ckv, kr = _odd_in(h, g_mix_pre[1][None], mod4, w_in, mla_q_norm[0][None],
                          mla_kv_norm[0][None], cos_h, sin_h, seg_all, pos_all)
    w_q = w_uq[0].reshape(MLA_Q_RANK, MLA_HEADS, MLA_NOPE + MLA_ROPE)
    w_q = jnp.pad(w_q, ((0, 0), (0, 0), (0, MLA_QK_PAD - MLA_NOPE - MLA_ROPE)))
    w_q = w_q.reshape(MLA_Q_RANK, MLA_HEADS * MLA_QK_PAD).astype(MX)
    n_lat = batch * seq_len
    qh = _q_up(cq, 1, n_lat, w_q, cos_h, sin_h, pos_all)
    w_kv = w_ukv[0].astype(MX)
    k_lat, v_lat = _kv_up(ckv, kr, 1, n_lat, ROW_TILE, w_kv)
    k_ctx, v_ctx = _kv_up(ckv, kr, 0, batch * ctx_len, ctx_len, w_kv)
    o = _mla_attn(qh, k_ctx, v_ctx, k_lat, v_lat, batch, ctx_len, seq_len)
    hl = _odd_out(h, 1, o, w_out_o[0].astype(MX), g_mix_post[1][None], mod4, seg_lat)
    hl = _layer_tail(hl, 1, mod4, seg_lat, p)
    return hl.reshape(batch, seq_len, d)
```

```python
import functools
import math

import jax
import jax.numpy as jnp
import numpy as np
from jax import lax
from jax.experimental import pallas as pl
from jax.experimental.pallas import tpu as pltpu

F32 = jnp.float32
MX = jnp.bfloat16
HI = lax.Precision.HIGHEST

GRID_W = 64
ROPE_THETA = 10000.0
RMS_EPS = 1e-6

GLA_HEADS, GLA_DK, GLA_DV, GLA_RANK, GLA_TAU = 4, 64, 128, 16, 16.0
GLA_QK = GLA_HEADS * GLA_DK
GLA_VW = GLA_HEADS * GLA_DV
GLA_CHUNK = 128

SWA_HEADS, SWA_KV_HEADS, SWA_HD, WINDOW = 8, 2, 64, 128
SWA_G = SWA_HEADS // SWA_KV_HEADS
SWA_W = SWA_HEADS * SWA_HD

MLA_HEADS, MLA_Q_RANK, MLA_KV_RANK = 8, 512, 256
MLA_NOPE, MLA_ROPE, MLA_V = 128, 64, 128
MLA_QK_PAD = 256
MLA_SCALE = (MLA_NOPE + MLA_ROPE) ** -0.5

N_EXPERTS, TOP_K, N_GROUPS, TOPK_GROUPS = 64, 8, 8, 4
GROUP_SIZE = N_EXPERTS // N_GROUPS
D_EXPERT = 256
ROUTED_SCALE = 2.5

ROW_TILE = 512
VMEM_LIMIT = 56 * 1024 * 1024


def _cparams(sem):
    return pltpu.CompilerParams(dimension_semantics=sem, vmem_limit_bytes=VMEM_LIMIT)


def _sigmoid(x):
    return 1.0 / (1.0 + jnp.exp(-x))


def _silu(x):
    return x * _sigmoid(x)


def _rms(x, g):
    ms = jnp.mean(x * x, axis=-1, keepdims=True)
    return x * lax.rsqrt(ms + RMS_EPS) * g


def _modulate(h, g, shift, scale):
    return _rms(h, g) * (1.0 + scale) + shift


def _dot(a, b):
    return jnp.dot(a, b, preferred_element_type=F32)


def _dot_t(a, b):
    return lax.dot_general(a, b, (((1,), (1,)), ((), ())), preferred_element_type=F32)


def _rope(x, cos, sin_signed):
    n = x.shape[-1]
    lane = lax.broadcasted_iota(jnp.int32, x.shape, 1)
    first = (lane % 64) < 32
    partner = jnp.where(first, pltpu.roll(x, n - 32, 1), pltpu.roll(x, 32, 1))
    return x * cos + partner * sin_signed


def _mod_body(c_ref, w_ref, b_ref, o_ref):
    s = _silu(c_ref[...])
    o_ref[0] = jnp.dot(s, w_ref[0], precision=HI, preferred_element_type=F32) + b_ref[0]


def _mod_vectors(cvec, w_mod, b_mod):
    depth, d, n = w_mod.shape
    tn = 1536
    return pl.pallas_call(
        _mod_body,
        name="mod_vectors",
        out_shape=jax.ShapeDtypeStruct((depth, 8, n), F32),
        grid=(depth, n // tn),
        in_specs=[pl.BlockSpec((8, d), lambda l, j: (0, 0)),
                  pl.BlockSpec((1, d, tn), lambda l, j: (l, 0, j)),
                  pl.BlockSpec((1, 1, tn), lambda l, j: (l, 0, j))],
        out_specs=pl.BlockSpec((1, 8, tn), lambda l, j: (l, 0, j)),
        compiler_params=_cparams(("arbitrary", "arbitrary")),
    )(cvec, w_mod, b_mod.reshape(depth, 1, n))


def _mod_spec(seg_fn, which, d):
    return pl.BlockSpec((None, None, 1, d), lambda i, *_: (seg_fn(i), which, 0, 0))


def _full_spec(arr):
    nd = arr.ndim
    return pl.BlockSpec(arr.shape, lambda *_: (0,) * nd)


_EV_Q, _EV_K, _EV_V, _EV_G, _EV_A, _EV_SQ, _EV_SK, _EV_SV, _EV_END = (
    0, 256, 512, 1024, 1536, 1664, 2176, 2688, 3200)


def _even_in_body(h_ref, g_ref, sh_ref, sc_ref, w_ref, cos_ref, sin_ref,
                  q_ref, k_ref, v_ref, gg_ref, a_ref, sq_ref, sk_ref, sv_ref):
    u = _modulate(h_ref[...], g_ref[...], sh_ref[...], sc_ref[...]).astype(MX)
    q_ref[...] = _dot(u, w_ref[:, _EV_Q:_EV_K]) * (GLA_DK ** -0.5)
    k_ref[...] = _dot(u, w_ref[:, _EV_K:_EV_V])
    v_ref[...] = _dot(u, w_ref[:, _EV_V:_EV_G])
    gg_ref[...] = _dot(u, w_ref[:, _EV_G:_EV_A])
    a_ref[...] = _dot(u, w_ref[:, _EV_A:_EV_SQ])
    cos = cos_ref[...]
    sin = sin_ref[...]
    sq = _rope(_dot(u, w_ref[:, _EV_SQ:_EV_SK]), cos, sin)
    sq_ref[...] = (sq * (SWA_HD ** -0.5)).astype(sq_ref.dtype)
    sk_ref[...] = _rope(_dot(u, w_ref[:, _EV_SK:_EV_SV]), cos, sin).astype(sk_ref.dtype)
    sv_ref[...] = _dot(u, w_ref[:, _EV_SV:_EV_END]).astype(sv_ref.dtype)


def _even_in(h, g_pre, mod4, w_cat, cos_t, sin_t, seg_fn, pos_fn):
    t, d = h.shape
    tm = ROW_TILE
    row = lambda n: pl.BlockSpec((tm, n), lambda i: (i, 0))
    outs = [(GLA_QK, F32), (GLA_QK, F32), (GLA_VW, F32), (GLA_VW, F32), (128, F32),
            (SWA_W, MX), (SWA_W, MX), (SWA_W, MX)]
    return pl.pallas_call(
        _even_in_body,
        name="even_in",
        out_shape=[jax.ShapeDtypeStruct((t, n), dt) for n, dt in outs],
        grid=(t // tm,),
        in_specs=[row(d), _full_spec(g_pre), _mod_spec(seg_fn, 0, d), _mod_spec(seg_fn, 1, d),
                  _full_spec(w_cat),
                  pl.BlockSpec((tm, SWA_W), lambda i: (pos_fn(i), 0)),
                  pl.BlockSpec((tm, SWA_W), lambda i: (pos_fn(i), 0))],
        out_specs=[row(n) for n, _ in outs],
        compiler_params=_cparams(("parallel",)),
    )(h, g_pre, mod4, mod4, w_cat, cos_t, sin_t)


def _gla_consts(c, reverse):
    nlev = int(round(math.log2(c)))
    idx = np.arange(c)
    i, m = idx[:, None], idx[None, :]
    sizes = [c >> (l + 1) for l in range(nlev)]
    mats = [m <= i, m > i]
    for s in sizes:
        mats.append((m >= (i // s) * s) & (m <= i))
    for s in sizes:
        mats.append((m > i) & (m <= (i // s) * s + s - 1))
    a = np.stack(mats).astype(np.float32)
    masks = [np.eye(c, dtype=bool)]
    for s in sizes:
        bi, bj = i // s, m // s
        masks.append((bi % 2 == 1) & (bj == bi - 1))
    msk = np.stack(masks).astype(np.float32)
    if reverse:
        a = a[:, ::-1, ::-1]
        msk = msk[:, ::-1, ::-1]
    return (np.ascontiguousarray(a.reshape(-1, c)),
            np.ascontiguousarray(np.tile(msk, (1, 1, GLA_HEADS))))


def _split3(x):
    hi = x.astype(MX)
    r1 = x - hi.astype(F32)
    mid = r1.astype(MX)
    lo = (r1 - mid.astype(F32)).astype(MX)
    return hi, mid, lo


def _gla_direction(q, k, v, a, wa, ba, amat, lmask, hmask, vmask, bdmask, st_ref, d, last_row):
    c = q.shape[0]
    nlev = int(round(math.log2(c)))
    x = jnp.dot(a, wa, precision=HI, preferred_element_type=F32) + ba
    la = (jnp.minimum(x, 0.0) - jnp.log1p(jnp.exp(-jnp.abs(x)))) * (1.0 / GLA_TAU)
    hi, mid, lo = _split3(la)
    am = amat.astype(MX)
    ex = jnp.exp(_dot(am, hi) + _dot(am, mid) + _dot(am, lo))
    qd = (q * ex[0:c]).astype(MX)
    kd = (k * ex[c:2 * c]).astype(MX)
    st = st_ref[d]
    o = _dot_t(qd, st.astype(MX))
    scat = jnp.zeros((c, GLA_HEADS * c), F32)
    for lev in range(nlev + 1):
        if lev == 0:
            ql, kl = q, k
        else:
            ql = q * ex[(1 + lev) * c:(2 + lev) * c]
            kl = k * ex[(1 + nlev + lev) * c:(2 + nlev + lev) * c]
        kst = (jnp.concatenate([kl] * GLA_HEADS, axis=0) * hmask).astype(MX)
        scat = scat + _dot_t(ql.astype(MX), kst) * lmask[lev]
    vbd = (jnp.concatenate([v] * GLA_HEADS, axis=0) * vmask).astype(MX)
    o = o + _dot(scat.astype(MX), vbd)
    upd = lax.dot_general(v.astype(MX), kd, (((0,), (0,)), ((), ())), preferred_element_type=F32)
    st_ref[d] = st * ex[last_row:last_row + 1] + upd * bdmask
    return o


def _gla_body(qf_ref, kf_ref, vf_ref, af_ref, qb_ref, kb_ref, vb_ref, ab_ref,
              wa_ref, ba_ref, amf_ref, lmf_ref, amb_ref, lmb_ref, hm_ref, vm_ref, bd_ref,
              of_ref, ob_ref, st_ref):
    @pl.when(pl.program_id(1) == 0)
    def _():
        st_ref[...] = jnp.zeros_like(st_ref)

    c = qf_ref.shape[0]
    hm, vm, bd = hm_ref[...], vm_ref[...], bd_ref[...]
    of_ref[...] = _gla_direction(qf_ref[...], kf_ref[...], vf_ref[...], af_ref[...],
                                 wa_ref[0], ba_ref[0], amf_ref[...], lmf_ref, hm, vm, bd,
                                 st_ref, 0, c - 1)
    ob_ref[...] = _gla_direction(qb_ref[...], kb_ref[...], vb_ref[...], ab_ref[...],
                                 wa_ref[1], ba_ref[1], amb_ref[...], lmb_ref, hm, vm, bd,
                                 st_ref, 1, 0)


def _gla(q, k, v, a, wa, ba, batch, ctx_len, seq_len):
    t = q.shape[0]
    c = GLA_CHUNK
    nc, nl = ctx_len // c, seq_len // c
    amf, lmf = _gla_consts(c, False)
    amb, lmb = _gla_consts(c, True)
    r = np.arange(GLA_HEADS * c)[:, None] // c
    hm = (r == np.arange(GLA_QK)[None, :] // GLA_DK).astype(np.float32)
    vm = (r == np.arange(GLA_VW)[None, :] // GLA_DV).astype(np.float32)
    bd = (np.arange(GLA_VW)[:, None] // GLA_DV
          == np.arange(GLA_QK)[None, :] // GLA_DK).astype(np.float32)

    def fwd(b, s):
        return jnp.where(s < nc, nc * b + s, batch * nc + nl * b + (s - nc))

    def bwd(b, s):
        return jnp.where(s < nc, nc * b + (nc - 1 - s), batch * nc + nl * b + (nl - 1 - (s - nc)))

    def chunk(n, fn):
        return pl.BlockSpec((c, n), lambda b, s: (fn(b, s), 0))

    consts = [jnp.asarray(z) for z in (amf, lmf, amb, lmb, hm, vm, bd)]
    ins = [q, k, v, a, q, k, v, a, wa, ba] + consts
    specs = ([chunk(GLA_QK, fwd), chunk(GLA_QK, fwd), chunk(GLA_VW, fwd), chunk(128, fwd),
              chunk(GLA_QK, bwd), chunk(GLA_QK, bwd), chunk(GLA_VW, bwd), chunk(128, bwd)]
             + [_full_spec(z) for z in ins[8:]])
    return pl.pallas_call(
        _gla_body,
        name="gla_scan",
        out_shape=[jax.ShapeDtypeStruct((t, GLA_VW), F32)] * 2,
        grid=(batch, nc + nl),
        in_specs=specs,
        out_specs=[chunk(GLA_VW, fwd), chunk(GLA_VW, bwd)],
        scratch_shapes=[pltpu.VMEM((2, GLA_VW, GLA_QK), F32)],
        compiler_params=_cparams(("parallel", "arbitrary")),
    )(*ins)


def _swa_heads(q, kcat, vcat, valid, sink_ref, h):
    lane_head = lax.broadcasted_iota(jnp.int32, (1, SWA_G * SWA_HD), 1) // SWA_HD
    acc = jnp.zeros((q.shape[0], SWA_G * SWA_HD), F32)
    for g in range(SWA_G):
        hm = lane_head == g
        s = _dot_t(jnp.where(hm, q, jnp.zeros_like(q)), kcat)
        if valid is not None:
            s = jnp.where(valid, s, -jnp.inf)
        sk = sink_ref[h * SWA_G + g]
        m = jnp.maximum(jnp.max(s, axis=-1, keepdims=True), sk)
        p = jnp.exp(s - m)
        den = jnp.sum(p, axis=-1, keepdims=True) + jnp.exp(sk - m)
        og = _dot(p.astype(MX), jnp.where(hm, vcat, jnp.zeros_like(vcat)))
        acc = acc + og * (1.0 / den)
    return acc


def _swa_latent_body(sink_ref, q_ref, kc_ref, vc_ref, kp_ref, k0_ref, kn_ref,
                     vp_ref, v0_ref, vn_ref, o_ref, *, nblk):
    n = pl.program_id(1)
    blk = q_ref.shape[0]
    nctx = kc_ref.shape[0]
    shape = (blk, nctx + 3 * blk)
    qi = lax.broadcasted_iota(jnp.int32, shape, 0)
    col = lax.broadcasted_iota(jnp.int32, shape, 1)
    si = col - nctx
    kpos = (n - 1) * blk + si
    valid = (col < nctx) | ((jnp.abs(si - blk - qi) <= WINDOW) & (kpos >= 0) & (kpos < nblk * blk))
    w = SWA_G * SWA_HD
    for h in range(SWA_KV_HEADS):
        cols = slice(h * w, (h + 1) * w)
        kcat = jnp.concatenate([kc_ref[:, cols], kp_ref[:, cols], k0_ref[:, cols], kn_ref[:, cols]], axis=0)
        vcat = jnp.concatenate([vc_ref[:, cols], vp_ref[:, cols], v0_ref[:, cols], vn_ref[:, cols]], axis=0)
        o_ref[:, cols] = _swa_heads(q_ref[:, cols], kcat, vcat, valid, sink_ref, h).astype(o_ref.dtype)


def _swa_ctx_body(sink_ref, q_ref, kc_ref, vc_ref, o_ref):
    h = pl.program_id(1)
    o_ref[...] = _swa_heads(q_ref[...], kc_ref[...], vc_ref[...], None, sink_ref, h).astype(o_ref.dtype)


def _swa(sq, skr, svr, sink, batch, ctx_len, seq_len):
    t = sq.shape[0]
    blk = WINDOW
    w = SWA_G * SWA_HD
    nb = seq_len // blk
    cb = ctx_len // blk
    lat0 = batch * cb
    smem = pl.BlockSpec(memory_space=pltpu.SMEM)
    ctx_kv = pl.BlockSpec((ctx_len, w), lambda b, h, n: (b, h))

    def win(off):
        return pl.BlockSpec((blk, SWA_W), lambda b, n: (lat0 + b * nb + jnp.clip(n + off, 0, nb - 1), 0))

    ctx_all = pl.BlockSpec((ctx_len, SWA_W), lambda b, n: (b, 0))
    lat = pl.pallas_call(
        functools.partial(_swa_latent_body, nblk=nb),
        name="swa_latent",
        out_shape=jax.ShapeDtypeStruct((batch * seq_len, SWA_W), MX),
        grid=(batch, nb),
        in_specs=[smem, pl.BlockSpec((blk, SWA_W), lambda b, n: (lat0 + b * nb + n, 0)),
                  ctx_all, ctx_all, win(-1), win(0), win(1), win(-1), win(0), win(1)],
        out_specs=pl.BlockSpec((blk, SWA_W), lambda b, n: (b * nb + n, 0)),
        compiler_params=_cparams(("parallel", "arbitrary")),
    )(sink, sq, skr, svr, skr, skr, skr, svr, svr, svr)
    ctx = pl.pallas_call(
        _swa_ctx_body,
        name="swa_ctx",
        out_shape=jax.ShapeDtypeStruct((batch * ctx_len, SWA_W), MX),
        grid=(batch, SWA_KV_HEADS, cb),
        in_specs=[smem, pl.BlockSpec((blk, w), lambda b, h, n: (b * cb + n, h)), ctx_kv, ctx_kv],
        out_specs=pl.BlockSpec((blk, w), lambda b, h, n: (b * cb + n, h)),
        compiler_params=_cparams(("parallel", "parallel", "arbitrary")),
    )(sink, sq, skr, svr)
    return jnp.concatenate([ctx, lat], axis=0)


def _residual(h, y, gpost, gate):
    return h + gate * _rms(y, gpost)


def _even_out_body(h_ref, of_ref, ob_ref, gg_ref, a_ref, gn_ref, w1_ref, w2_ref, gpost_ref,
                   gate_ref, o_ref):
    o = of_ref[...] + ob_ref[...]
    gn = gn_ref[...]
    parts = [_rms(o[:, j * GLA_DV:(j + 1) * GLA_DV], gn) for j in range(GLA_HEADS)]
    gl = jnp.concatenate(parts, axis=-1) * _silu(gg_ref[...])
    y = _dot(gl.astype(MX), w1_ref[...]) + _dot(a_ref[...], w2_ref[...])
    o_ref[...] = _residual(h_ref[...], y, gpost_ref[...], gate_ref[...])


def _even_out(h, o_f, o_b, gg, a_swa, gn, w1, w2, gpost, mod4, seg_fn):
    t, d = h.shape
    tm = ROW_TILE
    row = lambda n: pl.BlockSpec((tm, n), lambda i: (i, 0))
    return pl.pallas_call(
        _even_out_body,
        name="even_out",
        out_shape=jax.ShapeDtypeStruct((t, d), F32),
        grid=(t // tm,),
        in_specs=[row(d), row(GLA_VW), row(GLA_VW), row(GLA_VW), row(SWA_W), _full_spec(gn),
                  _full_spec(w1), _full_spec(w2), _full_spec(gpost), _mod_spec(seg_fn, 2, d)],
        out_specs=row(d),
        compiler_params=_cparams(("parallel",)),
    )(h, o_f, o_b, gg, a_swa, gn, w1, w2, gpost, mod4)


def _odd_out_body(h_ref, o_ref_in, w_ref, gpost_ref, gate_ref, o_ref):
    y = _dot(o_ref_in[...], w_ref[...])
    o_ref[...] = _residual(h_ref[...], y, gpost_ref[...], gate_ref[...])


def _odd_out(h, h_off, o, w, gpost, mod4, seg_fn):
    t, d = o.shape[0], h.shape[1]
    tm = ROW_TILE
    row = lambda n: pl.BlockSpec((tm, n), lambda i: (i, 0))
    return pl.pallas_call(
        _odd_out_body,
        name="odd_out",
        out_shape=jax.ShapeDtypeStruct((t, d), F32),
        grid=(t // tm,),
        in_specs=[pl.BlockSpec((tm, d), lambda i: (i + h_off, 0)), row(o.shape[1]),
                  _full_spec(w), _full_spec(gpost), _mod_spec(seg_fn, 2, d)],
        out_specs=row(d),
        compiler_params=_cparams(("parallel",)),
    )(h, o, w, gpost, mod4)


def _odd_in_body(h_ref, g_ref, sh_ref, sc_ref, w_ref, qn_ref, kvn_ref, cos_ref, sin_ref,
                 cq_ref, ckv_ref, kr_ref):
    u = _modulate(h_ref[...], g_ref[...], sh_ref[...], sc_ref[...]).astype(MX)
    cq_ref[...] = _rms(_dot(u, w_ref[:, 0:MLA_Q_RANK]), qn_ref[...]).astype(cq_ref.dtype)
    c1 = MLA_Q_RANK + MLA_KV_RANK
    ckv_ref[...] = _rms(_dot(u, w_ref[:, MLA_Q_RANK:c1]), kvn_ref[...]).astype(ckv_ref.dtype)
    kr = _dot(u, w_ref[:, c1:c1 + 128])
    kr_ref[...] = _rope(kr, cos_ref[...], sin_ref[...]).astype(kr_ref.dtype)


def _odd_in(h, g_pre, mod4, w_cat, qn, kvn, cos_t, sin_t, seg_fn, pos_fn):
    t, d = h.shape
    tm = ROW_TILE
    row = lambda n: pl.BlockSpec((tm, n), lambda i: (i, 0))
    tab = pl.BlockSpec((tm, 128), lambda i: (pos_fn(i), 1))
    return pl.pallas_call(
        _odd_in_body,
        name="odd_in",
        out_shape=[jax.ShapeDtypeStruct((t, MLA_Q_RANK), MX),
                   jax.ShapeDtypeStruct((t, MLA_KV_RANK), MX),
                   jax.ShapeDtypeStruct((t, 128), MX)],
        grid=(t // tm,),
        in_specs=[row(d), _full_spec(g_pre), _mod_spec(seg_fn, 0, d), _mod_spec(seg_fn, 1, d),
                  _full_spec(w_cat), _full_spec(qn), _full_spec(kvn), tab, tab],
        out_specs=[row(MLA_Q_RANK), row(MLA_KV_RANK), row(128)],
        compiler_params=_cparams(("parallel",)),
    )(h, g_pre, mod4, mod4, w_cat, qn, kvn, cos_t, sin_t)


def _q_up_body(cq_ref, w_ref, cos_ref, sin_ref, o_ref):
    cq, cos, sin = cq_ref[...], cos_ref[...], sin_ref[...]
    scale = MLA_SCALE * math.log2(math.e)
    for h in range(MLA_HEADS):
        c0 = h * MLA_QK_PAD
        z = _dot(cq, w_ref[:, c0:c0 + MLA_QK_PAD])
        o_ref[:, c0:c0 + MLA_NOPE] = (z[:, :MLA_NOPE] * scale).astype(o_ref.dtype)
        zr = _rope(z[:, MLA_NOPE:], cos, sin)
        o_ref[:, c0 + MLA_NOPE:c0 + MLA_QK_PAD] = (zr * scale).astype(o_ref.dtype)


def _q_up(cq, row_off, nrows, w_pad, cos_t, sin_t, pos_fn):
    tm = ROW_TILE
    width = MLA_HEADS * MLA_QK_PAD
    return pl.pallas_call(
        _q_up_body,
        name="mla_q_up",
        out_shape=jax.ShapeDtypeStruct((nrows, width), MX),
        grid=(nrows // tm,),
        in_specs=[pl.BlockSpec((tm, MLA_Q_RANK), lambda i: (i + row_off, 0)),
                  _full_spec(w_pad),
                  pl.BlockSpec((tm, 128), lambda i: (pos_fn(i + row_off), 1)),
                  pl.BlockSpec((tm, 128), lambda i: (pos_fn(i + row_off), 1))],
        out_specs=pl.BlockSpec((tm, width), lambda i: (i, 0)),
        compiler_params=_cparams(("parallel",)),
    )(cq, w_pad, cos_t, sin_t)


def _kv_up_body(ckv_ref, kr_ref, w_ref, k_ref, v_ref):
    ckv, kr = ckv_ref[...], kr_ref[...]
    for h in range(MLA_HEADS):
        c0 = h * (MLA_NOPE + MLA_V)
        z = _dot(ckv, w_ref[:, c0:c0 + MLA_NOPE + MLA_V])
        k0 = h * MLA_QK_PAD
        k_ref[:, k0:k0 + MLA_NOPE] = z[:, :MLA_NOPE].astype(k_ref.dtype)
        k_ref[:, k0 + MLA_NOPE:k0 + MLA_QK_PAD] = kr
        v_ref[:, h * MLA_V:(h + 1) * MLA_V] = z[:, MLA_NOPE:].astype(v_ref.dtype)


def _kv_up(ckv, kr, row_off, nrows, tm, w_ukv):
    return pl.pallas_call(
        _kv_up_body,
        name="mla_kv_up",
        out_shape=[jax.ShapeDtypeStruct((nrows, MLA_HEADS * MLA_QK_PAD), MX),
                   jax.ShapeDtypeStruct((nrows, MLA_HEADS * MLA_V), MX)],
        grid=(nrows // tm,),
        in_specs=[pl.BlockSpec((tm, MLA_KV_RANK), lambda i: (i + row_off, 0)),
                  pl.BlockSpec((tm, 128), lambda i: (i + row_off, 0)),
                  _full_spec(w_ukv)],
        out_specs=[pl.BlockSpec((tm, MLA_HEADS * MLA_QK_PAD), lambda i: (i, 0)),
                   pl.BlockSpec((tm, MLA_HEADS * MLA_V), lambda i: (i, 0))],
        compiler_params=_cparams(("parallel",)),
    )(ckv, kr, w_ukv)


def _mla_attn_body(q_ref, kc_ref, vc_ref, k_ref, v_ref, o_ref, *, tk, unroll):
    q = q_ref[...]
    s = _dot_t(q, kc_ref[...])
    m = jnp.max(s, axis=-1, keepdims=True)
    p = jnp.exp2(s - m)
    l = jnp.sum(p, axis=-1, keepdims=True)
    acc = _dot(p.astype(MX), vc_ref[...])

    def step(j, carry):
        m, l, acc = carry
        start = pl.multiple_of(j * tk, tk)
        s = _dot_t(q, k_ref[pl.ds(start, tk), :])
        mn = jnp.maximum(m, jnp.max(s, axis=-1, keepdims=True))
        alpha = jnp.exp2(m - mn)
        p = jnp.exp2(s - mn)
        l = alpha * l + jnp.sum(p, axis=-1, keepdims=True)
        acc = alpha * acc + _dot(p.astype(MX), v_ref[pl.ds(start, tk), :])
        return mn, l, acc

    m, l, acc = lax.fori_loop(0, k_ref.shape[0] // tk, step, (m, l, acc), unroll=unroll)
    o_ref[...] = (acc * (1.0 / l)).astype(o_ref.dtype)


def _mla_attn(q, k_ctx, v_ctx, k_lat, v_lat, batch, ctx_len, seq_len, tq=1024, tk=512, unroll=8):
    nq = seq_len // tq
    return pl.pallas_call(
        functools.partial(_mla_attn_body, tk=tk, unroll=unroll),
        name="mla_attn",
        out_shape=jax.ShapeDtypeStruct((batch * seq_len, MLA_HEADS * MLA_V), MX),
        grid=(batch, MLA_HEADS, nq),
        in_specs=[pl.BlockSpec((tq, MLA_QK_PAD), lambda b, h, i: (b * nq + i, h)),
                  pl.BlockSpec((ctx_len, MLA_QK_PAD), lambda b, h, i: (b, h)),
                  pl.BlockSpec((ctx_len, MLA_V), lambda b, h, i: (b, h)),
                  pl.BlockSpec((seq_len, MLA_QK_PAD), lambda b, h, i: (b, h)),
                  pl.BlockSpec((seq_len, MLA_V), lambda b, h, i: (b, h))],
        out_specs=pl.BlockSpec((tq, MLA_V), lambda b, h, i: (b * nq + i, h)),
        compiler_params=_cparams(("parallel", "parallel", "arbitrary")),
    )(q, k_ctx, v_ctx, k_lat, v_lat)


def _route(scores, sel):
    e, t = sel.shape
    neg = -jnp.inf
    x3 = sel.reshape(N_GROUPS, GROUP_SIZE, t)
    pos = lax.broadcasted_iota(jnp.int32, x3.shape, 1)
    m1 = jnp.max(x3, axis=1, keepdims=True)
    i1 = jnp.min(jnp.where(x3 == m1, pos, GROUP_SIZE), axis=1, keepdims=True)
    m2 = jnp.max(jnp.where(pos == i1, neg, x3), axis=1, keepdims=True)
    gs = m1 + m2
    gid = lax.broadcasted_iota(jnp.int32, gs.shape, 0)
    beaten = jnp.zeros(gs.shape, jnp.int32)
    for g in range(N_GROUPS):
        other = gs[g:g + 1]
        beaten = beaten + jnp.where((other > gs) | ((other == gs) & (g < gid)), 1, 0)
    cur = jnp.where(beaten < TOPK_GROUPS, x3, neg).reshape(e, t)
    row = lax.broadcasted_iota(jnp.int32, (e, t), 0)
    chosen = jnp.zeros((e, t), F32)
    for _ in range(TOP_K):
        m = jnp.max(cur, axis=0, keepdims=True)
        i = jnp.min(jnp.where(cur == m, row, e), axis=0, keepdims=True)
        hit = row == i
        chosen = jnp.where(hit, 1.0, chosen)
        cur = jnp.where(hit, neg, cur)
    w = chosen * scores
    return chosen, w / jnp.sum(w, axis=0, keepdims=True) * ROUTED_SCALE


def _ffn_pre_body(h_ref, g_ref, sh_ref, sc_ref, wr_ref, br_ref, us_ref, ls_ref,
                  v_ref, dst_ref, wt_ref, cpad_ref):
    vl = _modulate(h_ref[...], g_ref[...], sh_ref[...], sc_ref[...])
    v_ref[...] = vl.astype(v_ref.dtype)
    logits = lax.dot_general(wr_ref[...], vl, (((1,), (1,)), ((), ())), precision=HI,
                             preferred_element_type=F32)
    scores = _sigmoid(logits)
    chosen, w = _route(scores, scores + br_ref[...])
    rank = _dot(chosen.astype(MX), us_ref[...])
    cnt = jnp.sum(chosen, axis=1, keepdims=True)
    cpad = jnp.ceil(cnt * (1.0 / MOE_PIECE)) * MOE_PIECE
    cpad_b = jnp.broadcast_to(cpad, (cpad.shape[0], 128))
    loff = _dot(ls_ref[...], cpad_b.astype(MX))
    dst_ref[...] = jnp.where(chosen > 0.0, loff[:, 0:1] + rank, -1.0).astype(jnp.int32)
    wt_ref[...] = w
    cpad_ref[...] = cpad_b.astype(jnp.int32)


def _ffn_pre(h, g_pre, mod4, w_router, b_router, seg_fn):
    t, d = h.shape
    tm = MOE_TB
    nb = t // tm
    row = lambda n: pl.BlockSpec((tm, n), lambda i: (i, 0))
    blk = lambda n: pl.BlockSpec((None, N_EXPERTS, n), lambda i: (i, 0, 0))
    us = jnp.asarray(np.triu(np.ones((tm, tm), np.float32), 1), MX)
    ls = jnp.asarray(np.tril(np.ones((N_EXPERTS, N_EXPERTS), np.float32), -1), MX)
    wr_t = w_router.T
    br_t = b_router.reshape(N_EXPERTS, 1)
    return pl.pallas_call(
        _ffn_pre_body,
        name="ffn_pre_router",
        out_shape=[jax.ShapeDtypeStruct((t, d), MX),
                   jax.ShapeDtypeStruct((nb, N_EXPERTS, tm), jnp.int32),
                   jax.ShapeDtypeStruct((nb, N_EXPERTS, tm), F32),
                   jax.ShapeDtypeStruct((nb, N_EXPERTS, 128), jnp.int32)],
        grid=(nb,),
        in_specs=[row(d), _full_spec(g_pre), _mod_spec(seg_fn, 3, d), _mod_spec(seg_fn, 4, d),
                  _full_spec(wr_t), _full_spec(br_t), _full_spec(us), _full_spec(ls)],
        out_specs=[row(d), blk(tm), blk(tm), blk(128)],
        compiler_params=_cparams(("parallel",)),
    )(h, g_pre, mod4, mod4, wr_t, br_t, us, ls)


MOE_TB = ROW_TILE
MOE_PIECE = 16
MOE_CAP = TOP_K * MOE_TB + N_EXPERTS * MOE_PIECE
MOE_NPIECE = MOE_CAP // MOE_PIECE
MOE_CHUNK = 1024
MOE_PPC = MOE_CHUNK // MOE_PIECE
MOE_RB = 512


def _dispatch_plan(cpad, t):
    nb = cpad.shape[0]
    loff = jnp.cumsum(cpad, axis=1) - cpad
    tot = jnp.sum(cpad, axis=0)
    reg = (tot + MOE_RB - 1) // MOE_RB * MOE_RB
    gend = jnp.cumsum(reg)
    gbase = gend - reg
    goff = gbase[None] + jnp.cumsum(cpad, axis=0) - cpad
    prow = jnp.arange(MOE_NPIECE, dtype=jnp.int32) * MOE_PIECE
    pexp = jnp.sum((prow[None, :, None] >= (loff + cpad)[:, None, :]).astype(jnp.int32), axis=2)
    pexp = jnp.minimum(pexp, N_EXPERTS - 1)
    gdst = jnp.take_along_axis(goff - loff, pexp, axis=1) + prow[None]
    ng = t * TOP_K + nb * N_EXPERTS * (MOE_PIECE - 1) + N_EXPERTS * (MOE_RB - 1)
    ng = -(-ng // MOE_RB) * MOE_RB
    start = jnp.arange(ng // MOE_RB, dtype=jnp.int32) * MOE_RB
    blk_e = jnp.sum((start[:, None] >= gend[None, :]).astype(jnp.int32), axis=1)
    blk_e = jnp.minimum(blk_e, N_EXPERTS - 1)
    active = (start < (gbase + tot)[blk_e]).astype(jnp.int32)
    i32 = lambda z: z.astype(jnp.int32)
    return dict(pexp=i32(pexp), gdst=i32(gdst), ntot=i32(jnp.sum(cpad, axis=1) // MOE_PIECE),
                toff=i32(gbase + tot), tnp=i32((reg - tot) // MOE_PIECE),
                nused=i32(gend[-1:] // MOE_RB), blk_e=blk_e, active=active, ng=ng)


def _onehot_rows(pexp_ref, dst_ref, blk, c, val_of, out_ref):
    rows = lax.broadcasted_iota(jnp.int32, (MOE_PIECE, MOE_TB), 0)
    for p in range(MOE_PPC):
        e = pexp_ref[blk, c * MOE_PPC + p]
        hit = dst_ref[pl.ds(e, 1), :] == rows + (c * MOE_CHUNK + p * MOE_PIECE)
        out_ref[p * MOE_PIECE:(p + 1) * MOE_PIECE, :] = jnp.where(hit, val_of(e), 0.0).astype(out_ref.dtype)


def _repeat(n, fn):
    def body(_, c):
        fn()
        return c

    lax.fori_loop(0, n, body, 0)


def _start_chunk_pieces(n, c, start_fn):
    @pl.when(n >= (c + 1) * MOE_PPC)
    def _():
        for p in range(MOE_PPC):
            start_fn(c * MOE_PPC + p)

    @pl.when((n > c * MOE_PPC) & (n < (c + 1) * MOE_PPC))
    def _():
        def body(q, carry):
            start_fn(q)
            return carry

        lax.fori_loop(c * MOE_PPC, n, body, 0)


def _wait_pieces(n, wait_chunk, wait_piece):
    _repeat(n // MOE_PPC, wait_chunk)
    _repeat(n % MOE_PPC, wait_piece)


def _moe_sort_body(pexp_ref, gdst_ref, ntot_ref, toff_ref, tnp_ref, nused_ref, x_ref, dst_ref,
                   xg_ref, xs_ref, pi_ref, zb_ref, sem, zsem):
    i, nb = pl.program_id(0), pl.num_programs(0)
    slot = i % 2

    def piece_copy(s, lo, go):
        return pltpu.make_async_copy(xs_ref.at[s, pl.ds(lo, MOE_PIECE)],
                                     xg_ref.at[pl.ds(go, MOE_PIECE)], sem.at[s])

    def zero_copy(go):
        return pltpu.make_async_copy(zb_ref.at[pl.ds(0, MOE_PIECE)], xg_ref.at[pl.ds(go, MOE_PIECE)],
                                     zsem.at[0])

    def zero_block(b):
        return pltpu.make_async_copy(zb_ref, xg_ref.at[pl.ds(pl.multiple_of(b * MOE_RB, MOE_RB), MOE_RB)],
                                     zsem.at[1])

    def wait_slot(s, n):
        def chunk():
            pltpu.make_async_copy(xs_ref.at[s, pl.ds(0, MOE_CHUNK)], xg_ref.at[pl.ds(0, MOE_CHUNK)],
                                  sem.at[s]).wait()

        _wait_pieces(n, chunk, lambda: piece_copy(s, 0, 0).wait())

    def start(q):
        piece_copy(slot, pl.multiple_of(q * MOE_PIECE, MOE_PIECE),
                   pl.multiple_of(gdst_ref[i, q], MOE_PIECE)).start()

    @pl.when(i >= 2)
    def _():
        wait_slot(slot, ntot_ref[i - 2])

    x = x_ref[...]
    for c in range(MOE_CAP // MOE_CHUNK):
        @pl.when(c * MOE_PPC < ntot_ref[i])
        def _():
            _onehot_rows(pexp_ref, dst_ref, i, c, lambda e: 1.0, pi_ref)
            xs_ref[slot, c * MOE_CHUNK:(c + 1) * MOE_CHUNK, :] = _dot(pi_ref[...], x).astype(xs_ref.dtype)

        _start_chunk_pieces(ntot_ref[i], c, start)

    @pl.when(i == nb - 1)
    def _():
        zb_ref[...] = jnp.zeros_like(zb_ref)

        def tail(e, n):
            def piece(p, c):
                zero_copy(pl.multiple_of(toff_ref[e] + p * MOE_PIECE, MOE_PIECE)).start()
                return c

            lax.fori_loop(0, tnp_ref[e], piece, 0)
            return n + tnp_ref[e]

        nz = lax.fori_loop(0, N_EXPERTS, tail, 0)
        nblk = xg_ref.shape[0] // MOE_RB

        def unused(b, c):
            zero_block(b).start()
            return c

        lax.fori_loop(nused_ref[0], nblk, unused, 0)
        _repeat(nz, lambda: zero_copy(0).wait())
        _repeat(nblk - nused_ref[0], lambda: zero_block(0).wait())
        wait_slot(slot, ntot_ref[i])

    @pl.when((i == nb - 1) & (i >= 1))
    def _():
        wait_slot(1 - slot, ntot_ref[i - 1])


def _moe_expert_body(be_ref, act_ref, x_ref, wg_ref, wu_ref, wd_ref, o_ref):
    del be_ref
    i = pl.program_id(0)

    @pl.when(act_ref[i] != 0)
    def _():
        x = x_ref[...]
        hid = _silu(_dot(x, wg_ref[...].astype(MX))) * _dot(x, wu_ref[...].astype(MX))
        o_ref[...] = _dot(hid.astype(MX), wd_ref[...].astype(MX)).astype(o_ref.dtype)

    @pl.when(act_ref[i] == 0)
    def _():
        o_ref[...] = jnp.zeros_like(o_ref)


def _moe_combine_body(pexp_ref, gdst_ref, ntot_ref, yg_ref, dst_ref, wt_ref, x_ref,
                      sg_ref, su_ref, sd_ref, h_ref, gpost_ref, gate_ref, o_ref,
                      ys_ref, pw_ref, acc_ref, sem):
    i, nb = pl.program_id(0), pl.num_programs(0)
    slot = i % 2

    def piece_copy(s, lo, go):
        return pltpu.make_async_copy(yg_ref.at[pl.ds(go, MOE_PIECE)],
                                     ys_ref.at[s, pl.ds(lo, MOE_PIECE)], sem.at[s])

    def fetch(blk, s):
        n = ntot_ref[blk]

        def start(q):
            piece_copy(s, pl.multiple_of(q * MOE_PIECE, MOE_PIECE),
                       pl.multiple_of(gdst_ref[blk, q], MOE_PIECE)).start()

        for c in range(MOE_CAP // MOE_CHUNK):
            _start_chunk_pieces(n, c, start)

        def clear(q, carry):
            ys_ref[s, pl.ds(pl.multiple_of(q * MOE_PIECE, MOE_PIECE), MOE_PIECE), :] = jnp.zeros(
                (MOE_PIECE, ys_ref.shape[2]), ys_ref.dtype)
            return carry

        lax.fori_loop(n, (n + MOE_PPC - 1) // MOE_PPC * MOE_PPC, clear, 0)

    @pl.when(i == 0)
    def _():
        fetch(0, 0)

    @pl.when(i + 1 < nb)
    def _():
        fetch(i + 1, 1 - slot)

    x = x_ref[...]
    hid = _silu(_dot(x, sg_ref[...].astype(MX))) * _dot(x, su_ref[...].astype(MX))
    acc_ref[...] = _dot(hid.astype(MX), sd_ref[...].astype(MX))
    def wait_chunk():
        pltpu.make_async_copy(yg_ref.at[pl.ds(0, MOE_CHUNK)], ys_ref.at[slot, pl.ds(0, MOE_CHUNK)],
                              sem.at[slot]).wait()

    _wait_pieces(ntot_ref[i], wait_chunk, lambda: piece_copy(slot, 0, 0).wait())
    for c in range(MOE_CAP // MOE_CHUNK):
        @pl.when(c * MOE_PPC < ntot_ref[i])
        def _():
            _onehot_rows(pexp_ref, dst_ref, i, c, lambda e: wt_ref[pl.ds(e, 1), :], pw_ref)
            acc_ref[...] += lax.dot_general(
                pw_ref[...], ys_ref[slot, c * MOE_CHUNK:(c + 1) * MOE_CHUNK, :],
                (((0,), (0,)), ((), ())), preferred_element_type=F32)
    o_ref[...] = _residual(h_ref[...], acc_ref[...], gpost_ref[...], gate_ref[...])


def _moe(x, dst_t, w_t, cpad, layer, wg, wu, wd, sg, su, sd, h, gpost, mod4, seg_fn):
    t, d = x.shape
    nb = t // MOE_TB
    plan = _dispatch_plan(cpad, t)
    ng = plan['ng']
    tables = (plan['pexp'], plan['gdst'], plan['ntot'])
    etab = pl.BlockSpec((None, N_EXPERTS, MOE_TB), lambda i, *_: (i, 0, 0))
    xg = pl.pallas_call(
        _moe_sort_body,
        name="moe_sort",
        out_shape=jax.ShapeDtypeStruct((ng, d), MX),
        grid_spec=pltpu.PrefetchScalarGridSpec(
            num_scalar_prefetch=6, grid=(nb,),
            in_specs=[pl.BlockSpec((MOE_TB, d), lambda i, *_: (i, 0)), etab],
            out_specs=pl.BlockSpec(memory_space=pl.ANY),
            scratch_shapes=[pltpu.VMEM((2, MOE_CAP, d), MX), pltpu.VMEM((MOE_CHUNK, MOE_TB), MX),
                            pltpu.VMEM((MOE_RB, d), MX),
                            pltpu.SemaphoreType.DMA((2,)), pltpu.SemaphoreType.DMA((2,))]),
        compiler_params=_cparams(("arbitrary",)),
    )(*tables, plan['toff'], plan['tnp'], plan['nused'], x, dst_t)
    yg = pl.pallas_call(
        _moe_expert_body,
        name="moe_experts",
        out_shape=jax.ShapeDtypeStruct((ng, d), MX),
        grid_spec=pltpu.PrefetchScalarGridSpec(
            num_scalar_prefetch=2, grid=(ng // MOE_RB,),
            in_specs=[pl.BlockSpec((MOE_RB, d), lambda i, be, act: (jnp.where(act[i] != 0, i, 0), 0)),
                      pl.BlockSpec((None, None, d, D_EXPERT), lambda i, be, act: (layer, be[i], 0, 0)),
                      pl.BlockSpec((None, None, d, D_EXPERT), lambda i, be, act: (layer, be[i], 0, 0)),
                      pl.BlockSpec((None, None, D_EXPERT, d), lambda i, be, act: (layer, be[i], 0, 0))],
            out_specs=pl.BlockSpec((MOE_RB, d), lambda i, be, act: (i, 0))),
        compiler_params=_cparams(("arbitrary",)),
    )(plan['blk_e'], plan['active'], xg, wg, wu, wd)
    row = lambda n: pl.BlockSpec((MOE_TB, n), lambda i, *_: (i, 0))
    const = lambda arr: pl.BlockSpec(arr.shape, lambda i, *_: (0,) * arr.ndim)
    return pl.pallas_call(
        _moe_combine_body,
        name="moe_combine",
        out_shape=jax.ShapeDtypeStruct((t, d), F32),
        grid_spec=pltpu.PrefetchScalarGridSpec(
            num_scalar_prefetch=3, grid=(nb,),
            in_specs=[pl.BlockSpec(memory_space=pl.ANY), etab, etab, row(d),
                      const(sg), const(su), const(sd), row(d), const(gpost),
                      pl.BlockSpec((None, None, 1, d), lambda i, *_: (seg_fn(i), 5, 0, 0))],
            out_specs=row(d),
            scratch_shapes=[pltpu.VMEM((2, MOE_CAP, d), MX), pltpu.VMEM((MOE_CHUNK, MOE_TB), MX),
                            pltpu.VMEM((MOE_TB, d), F32), pltpu.SemaphoreType.DMA((2,))]),
        compiler_params=_cparams(("arbitrary",)),
    )(*tables, yg, dst_t, w_t, x, sg, su, sd, h, gpost, mod4)


def _rope_tables(seq_len, n_ident):
    rows = seq_len // GRID_W
    row = jnp.repeat(jnp.arange(rows, dtype=F32), GRID_W)
    col = jnp.tile(jnp.arange(GRID_W, dtype=F32), rows)
    axis_dim = 32
    inv_freq = ROPE_THETA ** (-jnp.arange(0, axis_dim, 2, dtype=F32) / axis_dim)
    ang = jnp.concatenate([row[:, None] * inv_freq, col[:, None] * inv_freq], axis=-1)
    cos, sin = jnp.cos(ang), jnp.sin(ang)
    cos64 = jnp.concatenate([cos, cos], axis=-1)
    sin64 = jnp.concatenate([-sin, sin], axis=-1)
    cos64 = jnp.concatenate([jnp.ones((n_ident, 64), F32), cos64], axis=0)
    sin64 = jnp.concatenate([jnp.zeros((n_ident, 64), F32), sin64], axis=0)
    return cos64, sin64


def _even_weights(w_in):
    d = w_in.shape[0]
    cuts = np.cumsum([GLA_QK, GLA_QK, GLA_VW, GLA_VW, GLA_RANK, GLA_RANK, SWA_W,
                      SWA_KV_HEADS * SWA_HD])
    q, k, v, g, af, ab, sq, sk, sv = jnp.split(w_in, [int(c) for c in cuts], axis=1)
    a = jnp.concatenate([af, ab, jnp.zeros((d, 128 - 2 * GLA_RANK), w_in.dtype)], axis=1)

    def rep(wkv):
        return jnp.tile(wkv.reshape(d, SWA_KV_HEADS, 1, SWA_HD), (1, 1, SWA_G, 1)).reshape(d, SWA_W)

    return jnp.concatenate([q, k, v, g, a, sq, rep(sk), rep(sv)], axis=1).astype(MX)


def _layer_tail(h, l, mod4, seg_fn, p):
    x, dst_t, w_t, cpad = _ffn_pre(h, p['g_ffn_pre'][l][None], mod4, p['w_router'][l], p['b_router'][l], seg_fn)
    return _moe(x, dst_t, w_t, cpad[:, :, 0], l, p['w_exp_gate'], p['w_exp_up'], p['w_exp_down'],
                p['w_sh_gate'][l], p['w_sh_up'][l], p['w_sh_down'][l],
                h, p['g_ffn_post'][l][None], mod4, seg_fn)


def kernel(x, c, ctx, c_ctx, w_mod, b_mod, g_mix_pre, g_mix_post, g_ffn_pre, g_ffn_post, w_in_e, gla_wa2, gla_ba, gla_norm, swa_sink, w_out_e, w_in_o, mla_q_norm, mla_kv_norm, w_uq, w_ukv, w_out_o, w_router, b_router, w_exp_gate, w_exp_up, w_exp_down, w_sh_gate, w_sh_up, w_sh_down):
    p = dict(g_ffn_pre=g_ffn_pre, g_ffn_post=g_ffn_post, w_router=w_router, b_router=b_router,
             w_exp_gate=w_exp_gate, w_exp_up=w_exp_up, w_exp_down=w_exp_down,
             w_sh_gate=w_sh_gate, w_sh_up=w_sh_up, w_sh_down=w_sh_down)
    batch, seq_len, d = x.shape
    ctx_len = ctx.shape[1]
    depth = w_mod.shape[0]
    assert depth == 2 and batch * ctx_len == ROW_TILE and seq_len % ROW_TILE == 0
    lat_blocks = seq_len // ROW_TILE

    cvec = jnp.concatenate([c, c_ctx[None], jnp.zeros((8 - batch - 1, d), F32)], axis=0)
    mod = _mod_vectors(cvec, w_mod, b_mod)
    h = jnp.concatenate([ctx.reshape(batch * ctx_len, d), x.reshape(batch * seq_len, d)], axis=0)

    seg_all = lambda i: jnp.where(i == 0, batch, (i - 1) // lat_blocks)
    seg_lat = lambda i: i // lat_blocks
    pos_all = lambda i: jnp.where(i == 0, 0, 1 + (i - 1) % lat_blocks)
    cos64, sin64 = _rope_tables(seq_len, ROW_TILE)

    mod4 = mod[0].reshape(8, 6, 1, d)
    q, k, v, gg, a, sq, skr, svr = _even_in(
        h, g_mix_pre[0][None], mod4, _even_weights(w_in_e[0]),
        jnp.tile(cos64, (1, SWA_HEADS)), jnp.tile(sin64, (1, SWA_HEADS)), seg_all, pos_all)
    wa = jnp.zeros((2, 128, GLA_QK), F32)
    wa = wa.at[0, :GLA_RANK].set(gla_wa2[0, 0]).at[1, GLA_RANK:2 * GLA_RANK].set(gla_wa2[0, 1])
    o_f, o_b = _gla(q, k, v, a, wa, gla_ba[0][:, None, :], batch, ctx_len, seq_len)
    a_swa = _swa(sq, skr, svr, swa_sink[0], batch, ctx_len, seq_len)
    w_out = w_out_e[0].astype(MX)
    h = _even_out(h, o_f, o_b, gg, a_swa, gla_norm[0][None], w_out[:GLA_VW], w_out[GLA_VW:],
                  g_mix_post[0][None], mod4, seg_all)
    h = _layer_tail(h, 0, mod4, seg_all, p)

    mod4 = mod[1].reshape(8, 6, 1, d)
    ones = jnp.ones_like(cos64)
    cos_h = jnp.concatenate([ones, ones, cos64, ones], axis=1)
    sin_h = jnp.concatenate([0 * ones, 0 * ones, sin64, 0 * ones], axis=1)
    w_in = jnp.concatenate([w_in_o[0], jnp.zeros((d, 128 - MLA_ROPE), F32)], axis=1).astype(MX)
    cq, ckv, kr = _odd_in(h, g_mix_pre[1][None], mod4, w_in, mla_q_norm[0][None],
                          mla_kv_norm[0][None], cos_h, sin_h, seg_all, pos_all)
    w_q = w_uq[0].reshape(MLA_Q_RANK, MLA_HEADS, MLA_NOPE + MLA_ROPE)
    w_q = jnp.pad(w_q, ((0, 0), (0, 0), (0, MLA_QK_PAD - MLA_NOPE - MLA_ROPE)))
    w_q = w_q.reshape(MLA_Q_RANK, MLA_HEADS * MLA_QK_PAD).astype(MX)
    n_lat = batch * seq_len
    qh = _q_up(cq, 1, n_lat, w_q, cos_h, sin_h, pos_all)
    w_kv = w_ukv[0].astype(MX)
    k_lat, v_lat = _kv_up(ckv, kr, 1, n_lat, ROW_TILE, w_kv)
    k_ctx, v_ctx = _kv_up(ckv, kr, 0, batch * ctx_len, ctx_len, w_kv)
    o = _mla_attn(qh, k_ctx, v_ctx, k_lat, v_lat, batch, ctx_len, seq_len)
    hl = _odd_out(h, 1, o, w_out_o[0].astype(MX), g_mix_post[1][None], mod4, seg_lat)
    hl = _layer_tail(hl, 1, mod4, seg_lat, p)
    return hl.reshape(batch, seq_len, d)
```

```python
import functools
import math

import jax
import jax.numpy as jnp
import numpy as np
from jax import lax
from jax.experimental import pallas as pl
from jax.experimental.pallas import tpu as pltpu

F32 = jnp.float32
MX = jnp.bfloat16
HI = lax.Precision.HIGHEST

GRID_W = 64
ROPE_THETA = 10000.0
RMS_EPS = 1e-6

GLA_HEADS, GLA_DK, GLA_DV, GLA_RANK, GLA_TAU = 4, 64, 128, 16, 16.0
GLA_QK = GLA_HEADS * GLA_DK
GLA_VW = GLA_HEADS * GLA_DV
GLA_CHUNK = 128

SWA_HEADS, SWA_KV_HEADS, SWA_HD, WINDOW = 8, 2, 64, 128
SWA_G = SWA_HEADS // SWA_KV_HEADS
SWA_W = SWA_HEADS * SWA_HD

MLA_HEADS, MLA_Q_RANK, MLA_KV_RANK = 8, 512, 256
MLA_NOPE, MLA_ROPE, MLA_V = 128, 64, 128
MLA_QK_PAD = 256
MLA_SCALE = (MLA_NOPE + MLA_ROPE) ** -0.5

N_EXPERTS, TOP_K, N_GROUPS, TOPK_GROUPS = 64, 8, 8, 4
GROUP_SIZE = N_EXPERTS // N_GROUPS
D_EXPERT = 256
ROUTED_SCALE = 2.5

ROW_TILE = 512
VMEM_LIMIT = 56 * 1024 * 1024


def _cparams(sem):
    return pltpu.CompilerParams(dimension_semantics=sem, vmem_limit_bytes=VMEM_LIMIT)


def _sigmoid(x):
    return 1.0 / (1.0 + jnp.exp(-x))


def _silu(x):
    return x * _sigmoid(x)


def _rms(x, g):
    ms = jnp.mean(x * x, axis=-1, keepdims=True)
    return x * lax.rsqrt(ms + RMS_EPS) * g


def _modulate(h, g, shift, scale):
    return _rms(h, g) * (1.0 + scale) + shift


def _dot(a, b):
    return jnp.dot(a, b, preferred_element_type=F32)


def _dot_t(a, b):
    return lax.dot_general(a, b, (((1,), (1,)), ((), ())), preferred_element_type=F32)


def _rope(x, cos, sin_signed):
    n = x.shape[-1]
    lane = lax.broadcasted_iota(jnp.int32, x.shape, 1)
    first = (lane % 64) < 32
    partner = jnp.where(first, pltpu.roll(x, n - 32, 1), pltpu.roll(x, 32, 1))
    return x * cos + partner * sin_signed


def _mod_body(c_ref, w_ref, b_ref, o_ref):
    s = _silu(c_ref[...])
    o_ref[0] = jnp.dot(s, w_ref[0], precision=HI, preferred_element_type=F32) + b_ref[0]


def _mod_vectors(cvec, w_mod, b_mod):
    depth, d, n = w_mod.shape
    tn = 1536
    return pl.pallas_call(
        _mod_body,
        name="mod_vectors",
        out_shape=jax.ShapeDtypeStruct((depth, 8, n), F32),
        grid=(depth, n // tn),
        in_specs=[pl.BlockSpec((8, d), lambda l, j: (0, 0)),
                  pl.BlockSpec((1, d, tn), lambda l, j: (l, 0, j)),
                  pl.BlockSpec((1, 1, tn), lambda l, j: (l, 0, j))],
        out_specs=pl.BlockSpec((1, 8, tn), lambda l, j: (l, 0, j)),
        compiler_params=_cparams(("arbitrary", "arbitrary")),
    )(cvec, w_mod, b_mod.reshape(depth, 1, n))


def _mod_spec(seg_fn, which, d):
    return pl.BlockSpec((None, None, 1, d), lambda i, *_: (seg_fn(i), which, 0, 0))


def _full_spec(arr):
    nd = arr.ndim
    return pl.BlockSpec(arr.shape, lambda *_: (0,) * nd)


_EV_Q, _EV_K, _EV_V, _EV_G, _EV_A, _EV_SQ, _EV_SK, _EV_SV, _EV_END = (
    0, 256, 512, 1024, 1536, 1664, 2176, 2688, 3200)


def _even_in_body(h_ref, g_ref, sh_ref, sc_ref, w_ref, cos_ref, sin_ref,
                  q_ref, k_ref, v_ref, gg_ref, a_ref, sq_ref, sk_ref, sv_ref):
    u = _modulate(h_ref[...], g_ref[...], sh_ref[...], sc_ref[...]).astype(MX)
    q_ref[...] = _dot(u, w_ref[:, _EV_Q:_EV_K]) * (GLA_DK ** -0.5)
    k_ref[...] = _dot(u, w_ref[:, _EV_K:_EV_V])
    v_ref[...] = _dot(u, w_ref[:, _EV_V:_EV_G])
    gg_ref[...] = _dot(u, w_ref[:, _EV_G:_EV_A])
    a_ref[...] = _dot(u, w_ref[:, _EV_A:_EV_SQ])
    cos = cos_ref[...]
    sin = sin_ref[...]
    sq = _rope(_dot(u, w_ref[:, _EV_SQ:_EV_SK]), cos, sin)
    sq_ref[...] = (sq * (SWA_HD ** -0.5)).astype(sq_ref.dtype)
    sk_ref[...] = _rope(_dot(u, w_ref[:, _EV_SK:_EV_SV]), cos, sin).astype(sk_ref.dtype)
    sv_ref[...] = _dot(u, w_ref[:, _EV_SV:_EV_END]).astype(sv_ref.dtype)


def _even_in(h, g_pre, mod4, w_cat, cos_t, sin_t, seg_fn, pos_fn):
    t, d = h.shape
    tm = ROW_TILE
    row = lambda n: pl.BlockSpec((tm, n), lambda i: (i, 0))
    outs = [(GLA_QK, F32), (GLA_QK, F32), (GLA_VW, F32), (GLA_VW, F32), (128, F32),
            (SWA_W, MX), (SWA_W, MX), (SWA_W, MX)]
    return pl.pallas_call(
        _even_in_body,
        name="even_in",
        out_shape=[jax.ShapeDtypeStruct((t, n), dt) for n, dt in outs],
        grid=(t // tm,),
        in_specs=[row(d), _full_spec(g_pre), _mod_spec(seg_fn, 0, d), _mod_spec(seg_fn, 1, d),
                  _full_spec(w_cat),
                  pl.BlockSpec((tm, SWA_W), lambda i: (pos_fn(i), 0)),
                  pl.BlockSpec((tm, SWA_W), lambda i: (pos_fn(i), 0))],
        out_specs=[row(n) for n, _ in outs],
        compiler_params=_cparams(("parallel",)),
    )(h, g_pre, mod4, mod4, w_cat, cos_t, sin_t)


def _gla_consts(c, reverse):
    nlev = int(round(math.log2(c)))
    idx = np.arange(c)
    i, m = idx[:, None], idx[None, :]
    sizes = [c >> (l + 1) for l in range(nlev)]
    a = (m <= i).astype(np.float32)
    masks = [np.eye(c, dtype=bool)]
    for s in sizes:
        bi, bj = i // s, m // s
        masks.append((bi % 2 == 1) & (bj == bi - 1))
    msk = np.stack(masks).astype(np.float32)
    if reverse:
        a = a[::-1, ::-1]
        msk = msk[:, ::-1, ::-1]
    return (np.ascontiguousarray(a), np.ascontiguousarray(np.tile(msk, (1, 1, GLA_HEADS))))


def _split3(x):
    hi = x.astype(MX)
    r1 = x - hi.astype(F32)
    mid = r1.astype(MX)
    lo = (r1 - mid.astype(F32)).astype(MX)
    return hi, mid, lo


def _gla_level_exponents(la, cum, s, reverse):
    c, w = la.shape
    nblk = c // s
    if s >= 8:
        zero = jnp.zeros((1, w), F32)
        if not reverse:
            qrows = [zero if b == 0 else cum[b * s - 1:b * s] for b in range(nblk)]
            krows = [cum[(b + 1) * s - 1:(b + 1) * s] for b in range(nblk)]
        else:
            qrows = [zero if b == nblk - 1 else cum[(b + 1) * s:(b + 1) * s + 1] for b in range(nblk)]
            krows = [cum[b * s:b * s + 1] for b in range(nblk)]
        spread = lambda rows: jnp.concatenate([jnp.broadcast_to(r, (s, w)) for r in rows], axis=0)
        return cum - spread(qrows), spread(krows) - cum
    pos = lax.broadcasted_iota(jnp.int32, (c, 1), 0) & (s - 1)
    qe, ke = la, jnp.zeros_like(la)
    for d in range(1, s):
        before = jnp.where(pos >= d, pltpu.roll(la, d, 0), 0.0)
        after = jnp.where(pos + d <= s - 1, pltpu.roll(la, c - d, 0), 0.0)
        qe, ke = (qe + after, ke + before) if reverse else (qe + before, ke + after)
    return qe, ke


def _gla_direction(q, k, v, a, wa, ba, amat, lmask, hmask, vmask, bdmask, st_ref, d, reverse):
    c = q.shape[0]
    nlev = int(round(math.log2(c)))
    x = jnp.dot(a, wa, precision=HI, preferred_element_type=F32) + ba
    la = (jnp.minimum(x, 0.0) - jnp.log1p(jnp.exp(-jnp.abs(x)))) * (1.0 / GLA_TAU)
    hi, mid, lo = _split3(la)
    cum = _dot(amat, hi) + _dot(amat, mid) + _dot(amat, lo)
    last = cum[0:1] if reverse else cum[c - 1:c]
    qd = (q * jnp.exp(cum)).astype(MX)
    kd = (k * jnp.exp(last - cum)).astype(MX)
    st = st_ref[d]
    o = _dot_t(qd, st.astype(MX))
    scat = jnp.zeros((c, GLA_HEADS * c), F32)
    for lev in range(nlev + 1):
        if lev == 0:
            ql, kl = q, k
        else:
            qe, ke = _gla_level_exponents(la, cum, c >> lev, reverse)
            ql, kl = q * jnp.exp(qe), k * jnp.exp(ke)
        kst = (jnp.concatenate([kl] * GLA_HEADS, axis=0) * hmask).astype(MX)
        scat = scat + _dot_t(ql.astype(MX), kst) * lmask[lev]
    vbd = (jnp.concatenate([v] * GLA_HEADS, axis=0) * vmask).astype(MX)
    o = o + _dot(scat.astype(MX), vbd)
    upd = lax.dot_general(v.astype(MX), kd, (((0,), (0,)), ((), ())), preferred_element_type=F32)
    st_ref[d] = st * jnp.exp(last) + upd * bdmask
    return o


def _gla_body(qf_ref, kf_ref, vf_ref, af_ref, qb_ref, kb_ref, vb_ref, ab_ref,
              wa_ref, ba_ref, amf_ref, lmf_ref, amb_ref, lmb_ref, hm_ref, vm_ref, bd_ref,
              of_ref, ob_ref, st_ref):
    @pl.when(pl.program_id(1) == 0)
    def _():
        st_ref[...] = jnp.zeros_like(st_ref)

    hm, vm, bd = hm_ref[...], vm_ref[...], bd_ref[...]
    of_ref[...] = _gla_direction(qf_ref[...], kf_ref[...], vf_ref[...], af_ref[...],
                                 wa_ref[0], ba_ref[0], amf_ref[...], lmf_ref, hm, vm, bd,
                                 st_ref, 0, False)
    ob_ref[...] = _gla_direction(qb_ref[...], kb_ref[...], vb_ref[...], ab_ref[...],
                                 wa_ref[1], ba_ref[1], amb_ref[...], lmb_ref, hm, vm, bd,
                                 st_ref, 1, True)


def _gla(q, k, v, a, wa, ba, batch, ctx_len, seq_len):
    t = q.shape[0]
    c = GLA_CHUNK
    nc, nl = ctx_len // c, seq_len // c
    amf, lmf = _gla_consts(c, False)
    amb, lmb = _gla_consts(c, True)
    r = np.arange(GLA_HEADS * c)[:, None] // c
    hm = (r == np.arange(GLA_QK)[None, :] // GLA_DK).astype(np.float32)
    vm = (r == np.arange(GLA_VW)[None, :] // GLA_DV).astype(np.float32)
    bd = (np.arange(GLA_VW)[:, None] // GLA_DV
          == np.arange(GLA_QK)[None, :] // GLA_DK).astype(np.float32)

    def fwd(b, s):
        return jnp.where(s < nc, nc * b + s, batch * nc + nl * b + (s - nc))

    def bwd(b, s):
        return jnp.where(s < nc, nc * b + (nc - 1 - s), batch * nc + nl * b + (nl - 1 - (s - nc)))

    def chunk(n, fn):
        return pl.BlockSpec((c, n), lambda b, s: (fn(b, s), 0))

    consts = [jnp.asarray(amf, MX), jnp.asarray(lmf), jnp.asarray(amb, MX), jnp.asarray(lmb),
              jnp.asarray(hm), jnp.asarray(vm), jnp.asarray(bd)]
    ins = [q, k, v, a, q, k, v, a, wa, ba] + consts
    specs = ([chunk(GLA_QK, fwd), chunk(GLA_QK, fwd), chunk(GLA_VW, fwd), chunk(128, fwd),
              chunk(GLA_QK, bwd), chunk(GLA_QK, bwd), chunk(GLA_VW, bwd), chunk(128, bwd)]
             + [_full_spec(z) for z in ins[8:]])
    return pl.pallas_call(
        _gla_body,
        name="gla_scan",
        out_shape=[jax.ShapeDtypeStruct((t, GLA_VW), F32)] * 2,
        grid=(batch, nc + nl),
        in_specs=specs,
        out_specs=[chunk(GLA_VW, fwd), chunk(GLA_VW, bwd)],
        scratch_shapes=[pltpu.VMEM((2, GLA_VW, GLA_QK), F32)],
        compiler_params=_cparams(("parallel", "arbitrary")),
    )(*ins)


def _swa_heads(q, kcat, vcat, valid, sink_ref, h):
    lane_head = lax.broadcasted_iota(jnp.int32, (1, SWA_G * SWA_HD), 1) // SWA_HD
    acc = jnp.zeros((q.shape[0], SWA_G * SWA_HD), F32)
    for g in range(SWA_G):
        hm = lane_head == g
        s = _dot_t(jnp.where(hm, q, jnp.zeros_like(q)), kcat)
        if valid is not None:
            s = jnp.where(valid, s, -jnp.inf)
        sk = sink_ref[h * SWA_G + g]
        m = jnp.maximum(jnp.max(s, axis=-1, keepdims=True), sk)
        p = jnp.exp(s - m)
        den = jnp.sum(p, axis=-1, keepdims=True) + jnp.exp(sk - m)
        og = _dot(p.astype(MX), jnp.where(hm, vcat, jnp.zeros_like(vcat)))
        acc = acc + og * (1.0 / den)
    return acc


def _swa_latent_body(sink_ref, q_ref, kc_ref, vc_ref, kp_ref, k0_ref, kn_ref,
                     vp_ref, v0_ref, vn_ref, o_ref, *, nblk):
    n = pl.program_id(1)
    blk = q_ref.shape[0]
    nctx = kc_ref.shape[0]
    shape = (blk, nctx + 3 * blk)
    qi = lax.broadcasted_iota(jnp.int32, shape, 0)
    col = lax.broadcasted_iota(jnp.int32, shape, 1)
    si = col - nctx
    kpos = (n - 1) * blk + si
    valid = (col < nctx) | ((jnp.abs(si - blk - qi) <= WINDOW) & (kpos >= 0) & (kpos < nblk * blk))
    w = SWA_G * SWA_HD
    for h in range(SWA_KV_HEADS):
        cols = slice(h * w, (h + 1) * w)
        kcat = jnp.concatenate([kc_ref[:, cols], kp_ref[:, cols], k0_ref[:, cols], kn_ref[:, cols]], axis=0)
        vcat = jnp.concatenate([vc_ref[:, cols], vp_ref[:, cols], v0_ref[:, cols], vn_ref[:, cols]], axis=0)
        o_ref[:, cols] = _swa_heads(q_ref[:, cols], kcat, vcat, valid, sink_ref, h).astype(o_ref.dtype)


def _swa_ctx_body(sink_ref, q_ref, kc_ref, vc_ref, o_ref):
    h = pl.program_id(1)
    o_ref[...] = _swa_heads(q_ref[...], kc_ref[...], vc_ref[...], None, sink_ref, h).astype(o_ref.dtype)


def _swa(sq, skr, svr, sink, batch, ctx_len, seq_len):
    t = sq.shape[0]
    blk = WINDOW
    w = SWA_G * SWA_HD
    nb = seq_len // blk
    cb = ctx_len // blk
    lat0 = batch * cb
    smem = pl.BlockSpec(memory_space=pltpu.SMEM)
    ctx_kv = pl.BlockSpec((ctx_len, w), lambda b, h, n: (b, h))

    def win(off):
        return pl.BlockSpec((blk, SWA_W), lambda b, n: (lat0 + b * nb + jnp.clip(n + off, 0, nb - 1), 0))

    ctx_all = pl.BlockSpec((ctx_len, SWA_W), lambda b, n: (b, 0))
    lat = pl.pallas_call(
        functools.partial(_swa_latent_body, nblk=nb),
        name="swa_latent",
        out_shape=jax.ShapeDtypeStruct((batch * seq_len, SWA_W), MX),
        grid=(batch, nb),
        in_specs=[smem, pl.BlockSpec((blk, SWA_W), lambda b, n: (lat0 + b * nb + n, 0)),
                  ctx_all, ctx_all, win(-1), win(0), win(1), win(-1), win(0), win(1)],
        out_specs=pl.BlockSpec((blk, SWA_W), lambda b, n: (b * nb + n, 0)),
        compiler_params=_cparams(("parallel", "arbitrary")),
    )(sink, sq, skr, svr, skr, skr, skr, svr, svr, svr)
    ctx = pl.pallas_call(
        _swa_ctx_body,
        name="swa_ctx",
        out_shape=jax.ShapeDtypeStruct((batch * ctx_len, SWA_W), MX),
        grid=(batch, SWA_KV_HEADS, cb),
        in_specs=[smem, pl.BlockSpec((blk, w), lambda b, h, n: (b * cb + n, h)), ctx_kv, ctx_kv],
        out_specs=pl.BlockSpec((blk, w), lambda b, h, n: (b * cb + n, h)),
        compiler_params=_cparams(("parallel", "parallel", "arbitrary")),
    )(sink, sq, skr, svr)
    return jnp.concatenate([ctx, lat], axis=0)


def _residual(h, y, gpost, gate):
    return h + gate * _rms(y, gpost)


def _even_out_body(h_ref, of_ref, ob_ref, gg_ref, a_ref, gn_ref, w1_ref, w2_ref, gpost_ref,
                   gate_ref, o_ref):
    o = of_ref[...] + ob_ref[...]
    gn = gn_ref[...]
    parts = [_rms(o[:, j * GLA_DV:(j + 1) * GLA_DV], gn) for j in range(GLA_HEADS)]
    gl = jnp.concatenate(parts, axis=-1) * _silu(gg_ref[...])
    y = _dot(gl.astype(MX), w1_ref[...]) + _dot(a_ref[...], w2_ref[...])
    o_ref[...] = _residual(h_ref[...], y, gpost_ref[...], gate_ref[...])


def _even_out(h, o_f, o_b, gg, a_swa, gn, w1, w2, gpost, mod4, seg_fn):
    t, d = h.shape
    tm = ROW_TILE
    row = lambda n: pl.BlockSpec((tm, n), lambda i: (i, 0))
    return pl.pallas_call(
        _even_out_body,
        name="even_out",
        out_shape=jax.ShapeDtypeStruct((t, d), F32),
        grid=(t // tm,),
        in_specs=[row(d), row(GLA_VW), row(GLA_VW), row(GLA_VW), row(SWA_W), _full_spec(gn),
                  _full_spec(w1), _full_spec(w2), _full_spec(gpost), _mod_spec(seg_fn, 2, d)],
        out_specs=row(d),
        compiler_params=_cparams(("parallel",)),
    )(h, o_f, o_b, gg, a_swa, gn, w1, w2, gpost, mod4)


def _odd_out_body(h_ref, o_ref_in, w_ref, gpost_ref, gate_ref, o_ref):
    y = _dot(o_ref_in[...], w_ref[...])
    o_ref[...] = _residual(h_ref[...], y, gpost_ref[...], gate_ref[...])


def _odd_out(h, h_off, o, w, gpost, mod4, seg_fn):
    t, d = o.shape[0], h.shape[1]
    tm = ROW_TILE
    row = lambda n: pl.BlockSpec((tm, n), lambda i: (i, 0))
    return pl.pallas_call(
        _odd_out_body,
        name="odd_out",
        out_shape=jax.ShapeDtypeStruct((t, d), F32),
        grid=(t // tm,),
        in_specs=[pl.BlockSpec((tm, d), lambda i: (i + h_off, 0)), row(o.shape[1]),
                  _full_spec(w), _full_spec(gpost), _mod_spec(seg_fn, 2, d)],
        out_specs=row(d),
        compiler_params=_cparams(("parallel",)),
    )(h, o, w, gpost, mod4)


def _odd_in_body(h_ref, g_ref, sh_ref, sc_ref, w_ref, qn_ref, kvn_ref, cos_ref, sin_ref,
                 cq_ref, ckv_ref, kr_ref):
    u = _modulate(h_ref[...], g_ref[...], sh_ref[...], sc_ref[...]).astype(MX)
    cq_ref[...] = _rms(_dot(u, w_ref[:, 0:MLA_Q_RANK]), qn_ref[...]).astype(cq_ref.dtype)
    c1 = MLA_Q_RANK + MLA_KV_RANK
    ckv_ref[...] = _rms(_dot(u, w_ref[:, MLA_Q_RANK:c1]), kvn_ref[...]).astype(ckv_ref.dtype)
    kr = _dot(u, w_ref[:, c1:c1 + 128])
    kr_ref[...] = _rope(kr, cos_ref[...], sin_ref[...]).astype(kr_ref.dtype)


def _odd_in(h, g_pre, mod4, w_cat, qn, kvn, cos_t, sin_t, seg_fn, pos_fn):
    t, d = h.shape
    tm = ROW_TILE
    row = lambda n: pl.BlockSpec((tm, n), lambda i: (i, 0))
    tab = pl.BlockSpec((tm, 128), lambda i: (pos_fn(i), 1))
    return pl.pallas_call(
        _odd_in_body,
        name="odd_in",
        out_shape=[jax.ShapeDtypeStruct((t, MLA_Q_RANK), MX),
                   jax.ShapeDtypeStruct((t, MLA_KV_RANK), MX),
                   jax.ShapeDtypeStruct((t, 128), MX)],
        grid=(t // tm,),
        in_specs=[row(d), _full_spec(g_pre), _mod_spec(seg_fn, 0, d), _mod_spec(seg_fn, 1, d),
                  _full_spec(w_cat), _full_spec(qn), _full_spec(kvn), tab, tab],
        out_specs=[row(MLA_Q_RANK), row(MLA_KV_RANK), row(128)],
        compiler_params=_cparams(("parallel",)),
    )(h, g_pre, mod4, mod4, w_cat, qn, kvn, cos_t, sin_t)


def _q_up_body(cq_ref, w_ref, cos_ref, sin_ref, o_ref):
    cq, cos, sin = cq_ref[...], cos_ref[...], sin_ref[...]
    scale = MLA_SCALE * math.log2(math.e)
    for h in range(MLA_HEADS):
        c0 = h * MLA_QK_PAD
        z = _dot(cq, w_ref[:, c0:c0 + MLA_QK_PAD])
        o_ref[:, c0:c0 + MLA_NOPE] = (z[:, :MLA_NOPE] * scale).astype(o_ref.dtype)
        zr = _rope(z[:, MLA_NOPE:], cos, sin)
        o_ref[:, c0 + MLA_NOPE:c0 + MLA_QK_PAD] = (zr * scale).astype(o_ref.dtype)


def _q_up(cq, row_off, nrows, w_pad, cos_t, sin_t, pos_fn):
    tm = ROW_TILE
    width = MLA_HEADS * MLA_QK_PAD
    return pl.pallas_call(
        _q_up_body,
        name="mla_q_up",
        out_shape=jax.ShapeDtypeStruct((nrows, width), MX),
        grid=(nrows // tm,),
        in_specs=[pl.BlockSpec((tm, MLA_Q_RANK), lambda i: (i + row_off, 0)),
                  _full_spec(w_pad),
                  pl.BlockSpec((tm, 128), lambda i: (pos_fn(i + row_off), 1)),
                  pl.BlockSpec((tm, 128), lambda i: (pos_fn(i + row_off), 1))],
        out_specs=pl.BlockSpec((tm, width), lambda i: (i, 0)),
        compiler_params=_cparams(("parallel",)),
    )(cq, w_pad, cos_t, sin_t)


def _kv_up_body(ckv_ref, kr_ref, w_ref, k_ref, v_ref):
    ckv, kr = ckv_ref[...], kr_ref[...]
    for h in range(MLA_HEADS):
        c0 = h * (MLA_NOPE + MLA_V)
        z = _dot(ckv, w_ref[:, c0:c0 + MLA_NOPE + MLA_V])
        k0 = h * MLA_QK_PAD
        k_ref[:, k0:k0 + MLA_NOPE] = z[:, :MLA_NOPE].astype(k_ref.dtype)
        k_ref[:, k0 + MLA_NOPE:k0 + MLA_QK_PAD] = kr
        v_ref[:, h * MLA_V:(h + 1) * MLA_V] = z[:, MLA_NOPE:].astype(v_ref.dtype)


def _kv_up(ckv, kr, row_off, nrows, tm, w_ukv):
    return pl.pallas_call(
        _kv_up_body,
        name="mla_kv_up",
        out_shape=[jax.ShapeDtypeStruct((nrows, MLA_HEADS * MLA_QK_PAD), MX),
                   jax.ShapeDtypeStruct((nrows, MLA_HEADS * MLA_V), MX)],
        grid=(nrows // tm,),
        in_specs=[pl.BlockSpec((tm, MLA_KV_RANK), lambda i: (i + row_off, 0)),
                  pl.BlockSpec((tm, 128), lambda i: (i + row_off, 0)),
                  _full_spec(w_ukv)],
        out_specs=[pl.BlockSpec((tm, MLA_HEADS * MLA_QK_PAD), lambda i: (i, 0)),
                   pl.BlockSpec((tm, MLA_HEADS * MLA_V), lambda i: (i, 0))],
        compiler_params=_cparams(("parallel",)),
    )(ckv, kr, w_ukv)


def _mla_attn_body(q_ref, kc_ref, vc_ref, k_ref, v_ref, o_ref, *, tk, unroll):
    q = q_ref[...]
    s = _dot_t(q, kc_ref[...])
    m = jnp.max(s, axis=-1, keepdims=True)
    p = jnp.exp2(s - m)
    l = jnp.sum(p, axis=-1, keepdims=True)
    acc = _dot(p.astype(MX), vc_ref[...])

    def step(j, carry):
        m, l, acc = carry
        start = pl.multiple_of(j * tk, tk)
        s = _dot_t(q, k_ref[pl.ds(start, tk), :])
        mn = jnp.maximum(m, jnp.max(s, axis=-1, keepdims=True))
        alpha = jnp.exp2(m - mn)
        p = jnp.exp2(s - mn)
        l = alpha * l + jnp.sum(p, axis=-1, keepdims=True)
        acc = alpha * acc + _dot(p.astype(MX), v_ref[pl.ds(start, tk), :])
        return mn, l, acc

    m, l, acc = lax.fori_loop(0, k_ref.shape[0] // tk, step, (m, l, acc), unroll=unroll)
    o_ref[...] = (acc * (1.0 / l)).astype(o_ref.dtype)


def _mla_attn(q, k_ctx, v_ctx, k_lat, v_lat, batch, ctx_len, seq_len, tq=1024, tk=512, unroll=8):
    nq = seq_len // tq
    return pl.pallas_call(
        functools.partial(_mla_attn_body, tk=tk, unroll=unroll),
        name="mla_attn",
        out_shape=jax.ShapeDtypeStruct((batch * seq_len, MLA_HEADS * MLA_V), MX),
        grid=(batch, MLA_HEADS, nq),
        in_specs=[pl.BlockSpec((tq, MLA_QK_PAD), lambda b, h, i: (b * nq + i, h)),
                  pl.BlockSpec((ctx_len, MLA_QK_PAD), lambda b, h, i: (b, h)),
                  pl.BlockSpec((ctx_len, MLA_V), lambda b, h, i: (b, h)),
                  pl.BlockSpec((seq_len, MLA_QK_PAD), lambda b, h, i: (b, h)),
                  pl.BlockSpec((seq_len, MLA_V), lambda b, h, i: (b, h))],
        out_specs=pl.BlockSpec((tq, MLA_V), lambda b, h, i: (b * nq + i, h)),
        compiler_params=_cparams(("parallel", "parallel", "arbitrary")),
    )(q, k_ctx, v_ctx, k_lat, v_lat)


def _route(scores, sel):
    e, t = sel.shape
    neg = -jnp.inf
    x3 = sel.reshape(N_GROUPS, GROUP_SIZE, t)
    pos = lax.broadcasted_iota(jnp.int32, x3.shape, 1)
    m1 = jnp.max(x3, axis=1, keepdims=True)
    i1 = jnp.min(jnp.where(x3 == m1, pos, GROUP_SIZE), axis=1, keepdims=True)
    m2 = jnp.max(jnp.where(pos == i1, neg, x3), axis=1, keepdims=True)
    gs = m1 + m2
    gid = lax.broadcasted_iota(jnp.int32, gs.shape, 0)
    beaten = jnp.zeros(gs.shape, jnp.int32)
    for g in range(N_GROUPS):
        other = gs[g:g + 1]
        beaten = beaten + jnp.where((other > gs) | ((other == gs) & (g < gid)), 1, 0)
    cur = jnp.where(beaten < TOPK_GROUPS, x3, neg).reshape(e, t)
    row = lax.broadcasted_iota(jnp.int32, (e, t), 0)
    chosen = jnp.zeros((e, t), F32)
    for _ in range(TOP_K):
        m = jnp.max(cur, axis=0, keepdims=True)
        i = jnp.min(jnp.where(cur == m, row, e), axis=0, keepdims=True)
        hit = row == i
        chosen = jnp.where(hit, 1.0, chosen)
        cur = jnp.where(hit, neg, cur)
    w = chosen * scores
    return chosen, w / jnp.sum(w, axis=0, keepdims=True) * ROUTED_SCALE


def _ffn_pre_body(h_ref, g_ref, sh_ref, sc_ref, wr_ref, br_ref, us_ref, ls_ref,
                  v_ref, dst_ref, wt_ref, cpad_ref):
    vl = _modulate(h_ref[...], g_ref[...], sh_ref[...], sc_ref[...])
    v_ref[...] = vl.astype(v_ref.dtype)
    logits = lax.dot_general(wr_ref[...], vl, (((1,), (1,)), ((), ())), precision=HI,
                             preferred_element_type=F32)
    scores = _sigmoid(logits)
    chosen, w = _route(scores, scores + br_ref[...])
    rank = _dot(chosen.astype(MX), us_ref[...])
    cnt = jnp.sum(chosen, axis=1, keepdims=True)
    cpad = jnp.ceil(cnt * (1.0 / MOE_PIECE)) * MOE_PIECE
    cpad_b = jnp.broadcast_to(cpad, (cpad.shape[0], 128))
    loff = _dot(ls_ref[...], cpad_b.astype(MX))
    dst_ref[...] = jnp.where(chosen > 0.0, loff[:, 0:1] + rank, -1.0).astype(jnp.int32)
    wt_ref[...] = w
    cpad_ref[...] = cpad_b.astype(jnp.int32)


def _ffn_pre(h, g_pre, mod4, w_router, b_router, seg_fn):
    t, d = h.shape
    tm = MOE_TB
    nb = t // tm
    row = lambda n: pl.BlockSpec((tm, n), lambda i: (i, 0))
    blk = lambda n: pl.BlockSpec((None, N_EXPERTS, n), lambda i: (i, 0, 0))
    us = jnp.asarray(np.triu(np.ones((tm, tm), np.float32), 1), MX)
    ls = jnp.asarray(np.tril(np.ones((N_EXPERTS, N_EXPERTS), np.float32), -1), MX)
    wr_t = w_router.T
    br_t = b_router.reshape(N_EXPERTS, 1)
    return pl.pallas_call(
        _ffn_pre_body,
        name="ffn_pre_router",
        out_shape=[jax.ShapeDtypeStruct((t, d), MX),
                   jax.ShapeDtypeStruct((nb, N_EXPERTS, tm), jnp.int32),
                   jax.ShapeDtypeStruct((nb, N_EXPERTS, tm), F32),
                   jax.ShapeDtypeStruct((nb, N_EXPERTS, 128), jnp.int32)],
        grid=(nb,),
        in_specs=[row(d), _full_spec(g_pre), _mod_spec(seg_fn, 3, d), _mod_spec(seg_fn, 4, d),
                  _full_spec(wr_t), _full_spec(br_t), _full_spec(us), _full_spec(ls)],
        out_specs=[row(d), blk(tm), blk(tm), blk(128)],
        compiler_params=_cparams(("parallel",)),
    )(h, g_pre, mod4, mod4, wr_t, br_t, us, ls)


MOE_TB = ROW_TILE
MOE_PIECE = 16
MOE_CAP = TOP_K * MOE_TB + N_EXPERTS * MOE_PIECE
MOE_NPIECE = MOE_CAP // MOE_PIECE
MOE_CHUNK = 1024
MOE_PPC = MOE_CHUNK // MOE_PIECE
MOE_RB = 512


def _dispatch_plan(cpad, t):
    nb = cpad.shape[0]
    loff = jnp.cumsum(cpad, axis=1) - cpad
    tot = jnp.sum(cpad, axis=0)
    reg = (tot + MOE_RB - 1) // MOE_RB * MOE_RB
    gend = jnp.cumsum(reg)
    gbase = gend - reg
    goff = gbase[None] + jnp.cumsum(cpad, axis=0) - cpad
    prow = jnp.arange(MOE_NPIECE, dtype=jnp.int32) * MOE_PIECE
    pexp = jnp.sum((prow[None, :, None] >= (loff + cpad)[:, None, :]).astype(jnp.int32), axis=2)
    pexp = jnp.minimum(pexp, N_EXPERTS - 1)
    own = pexp[:, :, None] == jnp.arange(N_EXPERTS, dtype=jnp.int32)[None, None, :]
    gdst = jnp.sum(jnp.where(own, (goff - loff)[:, None, :], 0), axis=2) + prow[None]
    ng = t * TOP_K + nb * N_EXPERTS * (MOE_PIECE - 1) + N_EXPERTS * (MOE_RB - 1)
    ng = -(-ng // MOE_RB) * MOE_RB
    start = jnp.arange(ng // MOE_RB, dtype=jnp.int32) * MOE_RB
    blk_e = jnp.sum((start[:, None] >= gend[None, :]).astype(jnp.int32), axis=1)
    blk_e = jnp.minimum(blk_e, N_EXPERTS - 1)
    active = (start < (gbase + tot)[blk_e]).astype(jnp.int32)
    i32 = lambda z: z.astype(jnp.int32)
    return dict(pexp=i32(pexp), gdst=i32(gdst), ntot=i32(jnp.sum(cpad, axis=1) // MOE_PIECE),
                toff=i32(gbase + tot), tnp=i32((reg - tot) // MOE_PIECE),
                nused=i32(gend[-1:] // MOE_RB), blk_e=blk_e, active=active, ng=ng)


def _onehot_rows(pexp_ref, dst_ref, blk, c, val_of, out_ref):
    rows = lax.broadcasted_iota(jnp.int32, (MOE_PIECE, MOE_TB), 0)
    for p in range(MOE_PPC):
        e = pexp_ref[blk, c * MOE_PPC + p]
        hit = dst_ref[pl.ds(e, 1), :] == rows + (c * MOE_CHUNK + p * MOE_PIECE)
        out_ref[p * MOE_PIECE:(p + 1) * MOE_PIECE, :] = jnp.where(hit, val_of(e), 0.0).astype(out_ref.dtype)


def _repeat(n, fn):
    def body(_, c):
        fn()
        return c

    lax.fori_loop(0, n, body, 0)


def _start_chunk_pieces(n, c, start_fn):
    @pl.when(n >= (c + 1) * MOE_PPC)
    def _():
        for p in range(MOE_PPC):
            start_fn(c * MOE_PPC + p)

    @pl.when((n > c * MOE_PPC) & (n < (c + 1) * MOE_PPC))
    def _():
        def body(q, carry):
            start_fn(q)
            return carry

        lax.fori_loop(c * MOE_PPC, n, body, 0)


def _wait_pieces(n, wait_chunk, wait_piece):
    _repeat(n // MOE_PPC, wait_chunk)
    _repeat(n % MOE_PPC, wait_piece)


def _moe_sort_body(pexp_ref, gdst_ref, ntot_ref, toff_ref, tnp_ref, nused_ref, x_ref, dst_ref,
                   xg_ref, xs_ref, pi_ref, zb_ref, sem, zsem):
    i, nb = pl.program_id(0), pl.num_programs(0)
    slot = i % 2

    def piece_copy(s, lo, go):
        return pltpu.make_async_copy(xs_ref.at[s, pl.ds(lo, MOE_PIECE)],
                                     xg_ref.at[pl.ds(go, MOE_PIECE)], sem.at[s])

    def zero_copy(go):
        return pltpu.make_async_copy(zb_ref.at[pl.ds(0, MOE_PIECE)], xg_ref.at[pl.ds(go, MOE_PIECE)],
                                     zsem.at[0])

    def zero_block(b):
        return pltpu.make_async_copy(zb_ref, xg_ref.at[pl.ds(pl.multiple_of(b * MOE_RB, MOE_RB), MOE_RB)],
                                     zsem.at[1])

    def wait_slot(s, n):
        def chunk():
            pltpu.make_async_copy(xs_ref.at[s, pl.ds(0, MOE_CHUNK)], xg_ref.at[pl.ds(0, MOE_CHUNK)],
                                  sem.at[s]).wait()

        _wait_pieces(n, chunk, lambda: piece_copy(s, 0, 0).wait())

    def start(q):
        piece_copy(slot, pl.multiple_of(q * MOE_PIECE, MOE_PIECE),
                   pl.multiple_of(gdst_ref[i, q], MOE_PIECE)).start()

    @pl.when(i >= 2)
    def _():
        wait_slot(slot, ntot_ref[i - 2])

    x = x_ref[...]
    for c in range(MOE_CAP // MOE_CHUNK):
        @pl.when(c * MOE_PPC < ntot_ref[i])
        def _():
            _onehot_rows(pexp_ref, dst_ref, i, c, lambda e: 1.0, pi_ref)
            xs_ref[slot, c * MOE_CHUNK:(c + 1) * MOE_CHUNK, :] = _dot(pi_ref[...], x).astype(xs_ref.dtype)

        _start_chunk_pieces(ntot_ref[i], c, start)

    @pl.when(i == nb - 1)
    def _():
        zb_ref[...] = jnp.zeros_like(zb_ref)

        def tail(e, n):
            def piece(p, c):
                zero_copy(pl.multiple_of(toff_ref[e] + p * MOE_PIECE, MOE_PIECE)).start()
                return c

            lax.fori_loop(0, tnp_ref[e], piece, 0)
            return n + tnp_ref[e]

        nz = lax.fori_loop(0, N_EXPERTS, tail, 0)
        nblk = xg_ref.shape[0] // MOE_RB

        def unused(b, c):
            zero_block(b).start()
            return c

        lax.fori_loop(nused_ref[0], nblk, unused, 0)
        _repeat(nz, lambda: zero_copy(0).wait())
        _repeat(nblk - nused_ref[0], lambda: zero_block(0).wait())
        wait_slot(slot, ntot_ref[i])

    @pl.when((i == nb - 1) & (i >= 1))
    def _():
        wait_slot(1 - slot, ntot_ref[i - 1])


def _moe_expert_body(be_ref, act_ref, x_ref, wg_ref, wu_ref, wd_ref, o_ref):
    del be_ref
    i = pl.program_id(0)

    @pl.when(act_ref[i] != 0)
    def _():
        x = x_ref[...]
        hid = _silu(_dot(x, wg_ref[...].astype(MX))) * _dot(x, wu_ref[...].astype(MX))
        o_ref[...] = _dot(hid.astype(MX), wd_ref[...].astype(MX)).astype(o_ref.dtype)

    @pl.when(act_ref[i] == 0)
    def _():
        o_ref[...] = jnp.zeros_like(o_ref)


def _moe_combine_body(pexp_ref, gdst_ref, ntot_ref, yg_ref, dst_ref, wt_ref, x_ref,
                      sg_ref, su_ref, sd_ref, h_ref, gpost_ref, gate_ref, o_ref,
                      ys_ref, pw_ref, acc_ref, sem):
    i, nb = pl.program_id(0), pl.num_programs(0)
    slot = i % 2

    def piece_copy(s, lo, go):
        return pltpu.make_async_copy(yg_ref.at[pl.ds(go, MOE_PIECE)],
                                     ys_ref.at[s, pl.ds(lo, MOE_PIECE)], sem.at[s])

    def fetch(blk, s):
        n = ntot_ref[blk]

        def start(q):
            piece_copy(s, pl.multiple_of(q * MOE_PIECE, MOE_PIECE),
                       pl.multiple_of(gdst_ref[blk, q], MOE_PIECE)).start()

        for c in range(MOE_CAP // MOE_CHUNK):
            _start_chunk_pieces(n, c, start)

        def clear(q, carry):
            ys_ref[s, pl.ds(pl.multiple_of(q * MOE_PIECE, MOE_PIECE), MOE_PIECE), :] = jnp.zeros(
                (MOE_PIECE, ys_ref.shape[2]), ys_ref.dtype)
            return carry

        lax.fori_loop(n, (n + MOE_PPC - 1) // MOE_PPC * MOE_PPC, clear, 0)

    @pl.when(i == 0)
    def _():
        fetch(0, 0)

    @pl.when(i + 1 < nb)
    def _():
        fetch(i + 1, 1 - slot)

    x = x_ref[...]
    hid = _silu(_dot(x, sg_ref[...].astype(MX))) * _dot(x, su_ref[...].astype(MX))
    acc_ref[...] = _dot(hid.astype(MX), sd_ref[...].astype(MX))
    def wait_chunk():
        pltpu.make_async_copy(yg_ref.at[pl.ds(0, MOE_CHUNK)], ys_ref.at[slot, pl.ds(0, MOE_CHUNK)],
                              sem.at[slot]).wait()

    _wait_pieces(ntot_ref[i], wait_chunk, lambda: piece_copy(slot, 0, 0).wait())
    for c in range(MOE_CAP // MOE_CHUNK):
        @pl.when(c * MOE_PPC < ntot_ref[i])
        def _():
            _onehot_rows(pexp_ref, dst_ref, i, c, lambda e: wt_ref[pl.ds(e, 1), :], pw_ref)
            acc_ref[...] += lax.dot_general(
                pw_ref[...], ys_ref[slot, c * MOE_CHUNK:(c + 1) * MOE_CHUNK, :],
                (((0,), (0,)), ((), ())), preferred_element_type=F32)
    o_ref[...] = _residual(h_ref[...], acc_ref[...], gpost_ref[...], gate_ref[...])


def _moe(x, dst_t, w_t, cpad, layer, wg, wu, wd, sg, su, sd, h, gpost, mod4, seg_fn):
    t, d = x.shape
    nb = t // MOE_TB
    plan = _dispatch_plan(cpad, t)
    ng = plan['ng']
    tables = (plan['pexp'], plan['gdst'], plan['ntot'])
    etab = pl.BlockSpec((None, N_EXPERTS, MOE_TB), lambda i, *_: (i, 0, 0))
    xg = pl.pallas_call(
        _moe_sort_body,
        name="moe_sort",
        out_shape=jax.ShapeDtypeStruct((ng, d), MX),
        grid_spec=pltpu.PrefetchScalarGridSpec(
            num_scalar_prefetch=6, grid=(nb,),
            in_specs=[pl.BlockSpec((MOE_TB, d), lambda i, *_: (i, 0)), etab],
            out_specs=pl.BlockSpec(memory_space=pl.ANY),
            scratch_shapes=[pltpu.VMEM((2, MOE_CAP, d), MX), pltpu.VMEM((MOE_CHUNK, MOE_TB), MX),
                            pltpu.VMEM((MOE_RB, d), MX),
                            pltpu.SemaphoreType.DMA((2,)), pltpu.SemaphoreType.DMA((2,))]),
        compiler_params=_cparams(("arbitrary",)),
    )(*tables, plan['toff'], plan['tnp'], plan['nused'], x, dst_t)
    yg = pl.pallas_call(
        _moe_expert_body,
        name="moe_experts",
        out_shape=jax.ShapeDtypeStruct((ng, d), MX),
        grid_spec=pltpu.PrefetchScalarGridSpec(
            num_scalar_prefetch=2, grid=(ng // MOE_RB,),
            in_specs=[pl.BlockSpec((MOE_RB, d), lambda i, be, act: (jnp.where(act[i] != 0, i, 0), 0)),
                      pl.BlockSpec((None, None, d, D_EXPERT), lambda i, be, act: (layer, be[i], 0, 0)),
                      pl.BlockSpec((None, None, d, D_EXPERT), lambda i, be, act: (layer, be[i], 0, 0)),
                      pl.BlockSpec((None, None, D_EXPERT, d), lambda i, be, act: (layer, be[i], 0, 0))],
            out_specs=pl.BlockSpec((MOE_RB, d), lambda i, be, act: (i, 0))),
        compiler_params=_cparams(("arbitrary",)),
    )(plan['blk_e'], plan['active'], xg, wg, wu, wd)
    row = lambda n: pl.BlockSpec((MOE_TB, n), lambda i, *_: (i, 0))
    const = lambda arr: pl.BlockSpec(arr.shape, lambda i, *_: (0,) * arr.ndim)
    return pl.pallas_call(
        _moe_combine_body,
        name="moe_combine",
        out_shape=jax.ShapeDtypeStruct((t, d), F32),
        grid_spec=pltpu.PrefetchScalarGridSpec(
            num_scalar_prefetch=3, grid=(nb,),
            in_specs=[pl.BlockSpec(memory_space=pl.ANY), etab, etab, row(d),
                      const(sg), const(su), const(sd), row(d), const(gpost),
                      pl.BlockSpec((None, None, 1, d), lambda i, *_: (seg_fn(i), 5, 0, 0))],
            out_specs=row(d),
            scratch_shapes=[pltpu.VMEM((2, MOE_CAP, d), MX), pltpu.VMEM((MOE_CHUNK, MOE_TB), MX),
                            pltpu.VMEM((MOE_TB, d), F32), pltpu.SemaphoreType.DMA((2,))]),
        compiler_params=_cparams(("arbitrary",)),
    )(*tables, yg, dst_t, w_t, x, sg, su, sd, h, gpost, mod4)


def _rope_tables(seq_len, n_ident):
    rows = seq_len // GRID_W
    row = jnp.repeat(jnp.arange(rows, dtype=F32), GRID_W)
    col = jnp.tile(jnp.arange(GRID_W, dtype=F32), rows)
    axis_dim = 32
    inv_freq = ROPE_THETA ** (-jnp.arange(0, axis_dim, 2, dtype=F32) / axis_dim)
    ang = jnp.concatenate([row[:, None] * inv_freq, col[:, None] * inv_freq], axis=-1)
    cos, sin = jnp.cos(ang), jnp.sin(ang)
    cos64 = jnp.concatenate([cos, cos], axis=-1)
    sin64 = jnp.concatenate([-sin, sin], axis=-1)
    cos64 = jnp.concatenate([jnp.ones((n_ident, 64), F32), cos64], axis=0)
    sin64 = jnp.concatenate([jnp.zeros((n_ident, 64), F32), sin64], axis=0)
    return cos64, sin64


def _even_weights(w_in):
    d = w_in.shape[0]
    cuts = np.cumsum([GLA_QK, GLA_QK, GLA_VW, GLA_VW, GLA_RANK, GLA_RANK, SWA_W,
                      SWA_KV_HEADS * SWA_HD])
    q, k, v, g, af, ab, sq, sk, sv = jnp.split(w_in, [int(c) for c in cuts], axis=1)
    a = jnp.concatenate([af, ab, jnp.zeros((d, 128 - 2 * GLA_RANK), w_in.dtype)], axis=1)

    def rep(wkv):
        return jnp.tile(wkv.reshape(d, SWA_KV_HEADS, 1, SWA_HD), (1, 1, SWA_G, 1)).reshape(d, SWA_W)

    return jnp.concatenate([q, k, v, g, a, sq, rep(sk), rep(sv)], axis=1).astype(MX)


def _layer_tail(h, l, mod4, seg_fn, p):
    x, dst_t, w_t, cpad = _ffn_pre(h, p['g_ffn_pre'][l][None], mod4, p['w_router'][l], p['b_router'][l], seg_fn)
    return _moe(x, dst_t, w_t, cpad[:, :, 0], l, p['w_exp_gate'], p['w_exp_up'], p['w_exp_down'],
                p['w_sh_gate'][l], p['w_sh_up'][l], p['w_sh_down'][l],
                h, p['g_ffn_post'][l][None], mod4, seg_fn)


def kernel(x, c, ctx, c_ctx, w_mod, b_mod, g_mix_pre, g_mix_post, g_ffn_pre, g_ffn_post, w_in_e, gla_wa2, gla_ba, gla_norm, swa_sink, w_out_e, w_in_o, mla_q_norm, mla_kv_norm, w_uq, w_ukv, w_out_o, w_router, b_router, w_exp_gate, w_exp_up, w_exp_down, w_sh_gate, w_sh_up, w_sh_down):
    p = dict(g_ffn_pre=g_ffn_pre, g_ffn_post=g_ffn_post, w_router=w_router, b_router=b_router,
             w_exp_gate=w_exp_gate, w_exp_up=w_exp_up, w_exp_down=w_exp_down,
             w_sh_gate=w_sh_gate, w_sh_up=w_sh_up, w_sh_down=w_sh_down)
    batch, seq_len, d = x.shape
    ctx_len = ctx.shape[1]
    depth = w_mod.shape[0]
    assert depth == 2 and batch * ctx_len == ROW_TILE and seq_len % ROW_TILE == 0
    lat_blocks = seq_len // ROW_TILE

    cvec = jnp.concatenate([c, c_ctx[None], jnp.zeros((8 - batch - 1, d), F32)], axis=0)
    mod = _mod_vectors(cvec, w_mod, b_mod)
    h = jnp.concatenate([ctx.reshape(batch * ctx_len, d), x.reshape(batch * seq_len, d)], axis=0)

    seg_all = lambda i: jnp.where(i == 0, batch, (i - 1) // lat_blocks)
    seg_lat = lambda i: i // lat_blocks
    pos_all = lambda i: jnp.where(i == 0, 0, 1 + (i - 1) % lat_blocks)
    cos64, sin64 = _rope_tables(seq_len, ROW_TILE)

    mod4 = mod[0].reshape(8, 6, 1, d)
    q, k, v, gg, a, sq, skr, svr = _even_in(
        h, g_mix_pre[0][None], mod4, _even_weights(w_in_e[0]),
        jnp.tile(cos64, (1, SWA_HEADS)), jnp.tile(sin64, (1, SWA_HEADS)), seg_all, pos_all)
    wa = jnp.zeros((2, 128, GLA_QK), F32)
    wa = wa.at[0, :GLA_RANK].set(gla_wa2[0, 0]).at[1, GLA_RANK:2 * GLA_RANK].set(gla_wa2[0, 1])
    o_f, o_b = _gla(q, k, v, a, wa, gla_ba[0][:, None, :], batch, ctx_len, seq_len)
    a_swa = _swa(sq, skr, svr, swa_sink[0], batch, ctx_len, seq_len)
    w_out = w_out_e[0].astype(MX)
    h = _even_out(h, o_f, o_b, gg, a_swa, gla_norm[0][None], w_out[:GLA_VW], w_out[GLA_VW:],
                  g_mix_post[0][None], mod4, seg_all)
    h = _layer_tail(h, 0, mod4, seg_all, p)

    mod4 = mod[1].reshape(8, 6, 1, d)
    ones = jnp.ones_like(cos64)
    cos_h = jnp.concatenate([ones, ones, cos64, ones], axis=1)
    sin_h = jnp.concatenate([0 * ones, 0 * ones, sin64, 0 * ones], axis=1)
    w_in = jnp.concatenate([w_in_o[0], jnp.zeros((d, 128 - MLA_ROPE), F32)], axis=1).astype(MX)
    cq, ckv, kr = _odd_in(h, g_mix_pre[1][None], mod4, w_in, mla_q_norm[0][None],
                          mla_kv_norm[0][None], cos_h, sin_h, seg_all, pos_all)
    w_q = w_uq[0].reshape(MLA_Q_RANK, MLA_HEADS, MLA_NOPE + MLA_ROPE)
    w_q = jnp.pad(w_q, ((0, 0), (0, 0), (0, MLA_QK_PAD - MLA_NOPE - MLA_ROPE)))
    w_q = w_q.reshape(MLA_Q_RANK, MLA_HEADS * MLA_QK_PAD).astype(MX)
    n_lat = batch * seq_len
    qh = _q_up(cq, 1, n_lat, w_q, cos_h, sin_h, pos_all)
    w_kv = w_ukv[0].astype(MX)
    k_lat, v_lat = _kv_up(ckv, kr, 1, n_lat, ROW_TILE, w_kv)
    k_ctx, v_ctx = _kv_up(ckv, kr, 0, batch * ctx_len, ctx_len, w_kv)
    o = _mla_attn(qh, k_ctx, v_ctx, k_lat, v_lat, batch, ctx_len, seq_len)
    hl = _odd_out(h, 1, o, w_out_o[0].astype(MX), g_mix_post[1][None], mod4, seg_lat)
    hl = _layer_tail(hl, 1, mod4, seg_lat, p)
    return hl.reshape(batch, seq_len, d)
```

```python
import functools
import math

import jax
import jax.numpy as jnp
import numpy as np
from jax import lax
from jax.experimental import pallas as pl
from jax.experimental.pallas import tpu as pltpu

F32 = jnp.float32
MX = jnp.bfloat16
HI = lax.Precision.HIGHEST

GRID_W = 64
ROPE_THETA = 10000.0
RMS_EPS = 1e-6

GLA_HEADS, GLA_DK, GLA_DV, GLA_RANK, GLA_TAU = 4, 64, 128, 16, 16.0
GLA_QK = GLA_HEADS * GLA_DK
GLA_VW = GLA_HEADS * GLA_DV
GLA_CHUNK = 128

SWA_HEADS, SWA_KV_HEADS, SWA_HD, WINDOW = 8, 2, 64, 128
SWA_G = SWA_HEADS // SWA_KV_HEADS
SWA_W = SWA_HEADS * SWA_HD

MLA_HEADS, MLA_Q_RANK, MLA_KV_RANK = 8, 512, 256
MLA_NOPE, MLA_ROPE, MLA_V = 128, 64, 128
MLA_QK_PAD = 256
MLA_VW = 256
MLA_SCALE = (MLA_NOPE + MLA_ROPE) ** -0.5

N_EXPERTS, TOP_K, N_GROUPS, TOPK_GROUPS = 64, 8, 8, 4
GROUP_SIZE = N_EXPERTS // N_GROUPS
D_EXPERT = 256
ROUTED_SCALE = 2.5

ROW_TILE = 512
VMEM_LIMIT = 56 * 1024 * 1024


def _cparams(sem):
    return pltpu.CompilerParams(dimension_semantics=sem, vmem_limit_bytes=VMEM_LIMIT)


def _sigmoid(x):
    return 1.0 / (1.0 + jnp.exp(-x))


def _silu(x):
    return x * _sigmoid(x)


def _rms(x, g):
    ms = jnp.mean(x * x, axis=-1, keepdims=True)
    return x * lax.rsqrt(ms + RMS_EPS) * g


def _modulate(h, g, shift, scale):
    return _rms(h, g) * (1.0 + scale) + shift


def _dot(a, b):
    return jnp.dot(a, b, preferred_element_type=F32)


def _dot_t(a, b):
    return lax.dot_general(a, b, (((1,), (1,)), ((), ())), preferred_element_type=F32)


def _rope(x, cos, sin_signed):
    n = x.shape[-1]
    lane = lax.broadcasted_iota(jnp.int32, x.shape, 1)
    first = (lane % 64) < 32
    partner = jnp.where(first, pltpu.roll(x, n - 32, 1), pltpu.roll(x, 32, 1))
    return x * cos + partner * sin_signed


def _mod_body(c_ref, w_ref, b_ref, o_ref):
    s = _silu(c_ref[...])
    o_ref[0] = jnp.dot(s, w_ref[0], precision=HI, preferred_element_type=F32) + b_ref[0]


def _mod_vectors(cvec, w_mod, b_mod):
    depth, d, n = w_mod.shape
    tn = 1536
    return pl.pallas_call(
        _mod_body,
        name="mod_vectors",
        out_shape=jax.ShapeDtypeStruct((depth, 8, n), F32),
        grid=(depth, n // tn),
        in_specs=[pl.BlockSpec((8, d), lambda l, j: (0, 0)),
                  pl.BlockSpec((1, d, tn), lambda l, j: (l, 0, j)),
                  pl.BlockSpec((1, 1, tn), lambda l, j: (l, 0, j))],
        out_specs=pl.BlockSpec((1, 8, tn), lambda l, j: (l, 0, j)),
        compiler_params=_cparams(("arbitrary", "arbitrary")),
    )(cvec, w_mod, b_mod.reshape(depth, 1, n))


def _mod_spec(seg_fn, which, d):
    return pl.BlockSpec((None, None, 1, d), lambda i, *_: (seg_fn(i), which, 0, 0))


def _full_spec(arr):
    nd = arr.ndim
    return pl.BlockSpec(arr.shape, lambda *_: (0,) * nd)


_EV_Q, _EV_K, _EV_V, _EV_G, _EV_A, _EV_SQ, _EV_SK, _EV_SV, _EV_END = (
    0, 256, 512, 1024, 1536, 1664, 2176, 2688, 3200)


def _even_in_body(h_ref, g_ref, sh_ref, sc_ref, w_ref, cos_ref, sin_ref,
                  q_ref, k_ref, v_ref, gg_ref, a_ref, sq_ref, sk_ref, sv_ref):
    u = _modulate(h_ref[...], g_ref[...], sh_ref[...], sc_ref[...]).astype(MX)
    q_ref[...] = _dot(u, w_ref[:, _EV_Q:_EV_K]) * (GLA_DK ** -0.5)
    k_ref[...] = _dot(u, w_ref[:, _EV_K:_EV_V])
    v_ref[...] = _dot(u, w_ref[:, _EV_V:_EV_G])
    gg_ref[...] = _dot(u, w_ref[:, _EV_G:_EV_A])
    a_ref[...] = _dot(u, w_ref[:, _EV_A:_EV_SQ])
    cos = jnp.concatenate([cos_ref[...]] * (SWA_W // 128), axis=-1)
    sin = jnp.concatenate([sin_ref[...]] * (SWA_W // 128), axis=-1)
    sq = _rope(_dot(u, w_ref[:, _EV_SQ:_EV_SK]), cos, sin)
    sq_ref[...] = (sq * (SWA_HD ** -0.5)).astype(sq_ref.dtype)
    sk_ref[...] = _rope(_dot(u, w_ref[:, _EV_SK:_EV_SV]), cos, sin).astype(sk_ref.dtype)
    sv_ref[...] = _dot(u, w_ref[:, _EV_SV:_EV_END]).astype(sv_ref.dtype)


def _even_in(h, g_pre, mod4, w_cat, cos_t, sin_t, seg_fn, pos_fn):
    t, d = h.shape
    tm = ROW_TILE
    row = lambda n: pl.BlockSpec((tm, n), lambda i: (i, 0))
    outs = [(GLA_QK, F32), (GLA_QK, F32), (GLA_VW, F32), (GLA_VW, F32), (128, F32),
            (SWA_W, MX), (SWA_W, MX), (SWA_W, MX)]
    return pl.pallas_call(
        _even_in_body,
        name="even_in",
        out_shape=[jax.ShapeDtypeStruct((t, n), dt) for n, dt in outs],
        grid=(t // tm,),
        in_specs=[row(d), _full_spec(g_pre), _mod_spec(seg_fn, 0, d), _mod_spec(seg_fn, 1, d),
                  _full_spec(w_cat),
                  pl.BlockSpec((tm, 128), lambda i: (pos_fn(i), 0)),
                  pl.BlockSpec((tm, 128), lambda i: (pos_fn(i), 0))],
        out_specs=[row(n) for n, _ in outs],
        compiler_params=_cparams(("parallel",)),
    )(h, g_pre, mod4, mod4, w_cat, cos_t, sin_t)


def _gla_consts(c, reverse):
    nlev = int(round(math.log2(c)))
    idx = np.arange(c)
    i, m = idx[:, None], idx[None, :]
    sizes = [c >> (l + 1) for l in range(nlev)]
    a = (m <= i).astype(np.float32)
    masks = [np.eye(c, dtype=bool)]
    for s in sizes:
        bi, bj = i // s, m // s
        masks.append((bi % 2 == 1) & (bj == bi - 1))
    msk = np.stack(masks).astype(np.float32)
    if reverse:
        a = a[::-1, ::-1]
        msk = msk[:, ::-1, ::-1]
    return (np.ascontiguousarray(a), np.ascontiguousarray(np.tile(msk, (1, 1, GLA_HEADS))))


def _split3(x):
    hi = x.astype(MX)
    r1 = x - hi.astype(F32)
    mid = r1.astype(MX)
    lo = (r1 - mid.astype(F32)).astype(MX)
    return hi, mid, lo


def _gla_level_exponents(la, cum, s, reverse):
    c, w = la.shape
    nblk = c // s
    if s >= 8:
        zero = jnp.zeros((1, w), F32)
        if not reverse:
            qrows = [zero if b == 0 else cum[b * s - 1:b * s] for b in range(nblk)]
            krows = [cum[(b + 1) * s - 1:(b + 1) * s] for b in range(nblk)]
        else:
            qrows = [zero if b == nblk - 1 else cum[(b + 1) * s:(b + 1) * s + 1] for b in range(nblk)]
            krows = [cum[b * s:b * s + 1] for b in range(nblk)]
        spread = lambda rows: jnp.concatenate([jnp.broadcast_to(r, (s, w)) for r in rows], axis=0)
        return cum - spread(qrows), spread(krows) - cum
    pos = lax.broadcasted_iota(jnp.int32, (c, 1), 0) & (s - 1)
    qe, ke = la, jnp.zeros_like(la)
    for d in range(1, s):
        before = jnp.where(pos >= d, pltpu.roll(la, d, 0), 0.0)
        after = jnp.where(pos + d <= s - 1, pltpu.roll(la, c - d, 0), 0.0)
        qe, ke = (qe + after, ke + before) if reverse else (qe + before, ke + after)
    return qe, ke


def _gla_direction(q, k, v, a, wa, ba, amat, lmask, hmask, vmask, bdmask, st_ref, d, reverse):
    c = q.shape[0]
    nlev = int(round(math.log2(c)))
    x = jnp.dot(a, wa, precision=HI, preferred_element_type=F32) + ba
    la = (jnp.minimum(x, 0.0) - jnp.log1p(jnp.exp(-jnp.abs(x)))) * (1.0 / GLA_TAU)
    hi, mid, lo = _split3(la)
    cum = _dot(amat, hi) + _dot(amat, mid) + _dot(amat, lo)
    last = cum[0:1] if reverse else cum[c - 1:c]
    qd = (q * jnp.exp(cum)).astype(MX)
    kd = (k * jnp.exp(last - cum)).astype(MX)
    st = st_ref[d]
    o = _dot_t(qd, st.astype(MX))
    scat = jnp.zeros((c, GLA_HEADS * c), F32)
    for lev in range(nlev + 1):
        if lev == 0:
            ql, kl = q, k
        else:
            qe, ke = _gla_level_exponents(la, cum, c >> lev, reverse)
            ql, kl = q * jnp.exp(qe), k * jnp.exp(ke)
        kst = (jnp.concatenate([kl] * GLA_HEADS, axis=0) * hmask).astype(MX)
        scat = scat + _dot_t(ql.astype(MX), kst) * lmask[lev]
    vbd = (jnp.concatenate([v] * GLA_HEADS, axis=0) * vmask).astype(MX)
    o = o + _dot(scat.astype(MX), vbd)
    upd = lax.dot_general(v.astype(MX), kd, (((0,), (0,)), ((), ())), preferred_element_type=F32)
    st_ref[d] = st * jnp.exp(last) + upd * bdmask
    return o


def _gla_body(qf_ref, kf_ref, vf_ref, af_ref, qb_ref, kb_ref, vb_ref, ab_ref,
              wa_ref, ba_ref, amf_ref, lmf_ref, amb_ref, lmb_ref, hm_ref, vm_ref, bd_ref,
              of_ref, ob_ref, st_ref):
    @pl.when(pl.program_id(1) == 0)
    def _():
        st_ref[...] = jnp.zeros_like(st_ref)

    hm, vm, bd = hm_ref[...], vm_ref[...], bd_ref[...]
    of_ref[...] = _gla_direction(qf_ref[...], kf_ref[...], vf_ref[...], af_ref[...],
                                 wa_ref[0], ba_ref[0], amf_ref[...], lmf_ref, hm, vm, bd,
                                 st_ref, 0, False)
    ob_ref[...] = _gla_direction(qb_ref[...], kb_ref[...], vb_ref[...], ab_ref[...],
                                 wa_ref[1], ba_ref[1], amb_ref[...], lmb_ref, hm, vm, bd,
                                 st_ref, 1, True)


def _gla(q, k, v, a, wa, ba, batch, ctx_len, seq_len):
    t = q.shape[0]
    c = GLA_CHUNK
    nc, nl = ctx_len // c, seq_len // c
    amf, lmf = _gla_consts(c, False)
    amb, lmb = _gla_consts(c, True)
    r = np.arange(GLA_HEADS * c)[:, None] // c
    hm = (r == np.arange(GLA_QK)[None, :] // GLA_DK).astype(np.float32)
    vm = (r == np.arange(GLA_VW)[None, :] // GLA_DV).astype(np.float32)
    bd = (np.arange(GLA_VW)[:, None] // GLA_DV
          == np.arange(GLA_QK)[None, :] // GLA_DK).astype(np.float32)

    def fwd(b, s):
        return jnp.where(s < nc, nc * b + s, batch * nc + nl * b + (s - nc))

    def bwd(b, s):
        return jnp.where(s < nc, nc * b + (nc - 1 - s), batch * nc + nl * b + (nl - 1 - (s - nc)))

    def chunk(n, fn):
        return pl.BlockSpec((c, n), lambda b, s: (fn(b, s), 0))

    consts = [jnp.asarray(amf, MX), jnp.asarray(lmf), jnp.asarray(amb, MX), jnp.asarray(lmb),
              jnp.asarray(hm), jnp.asarray(vm), jnp.asarray(bd)]
    ins = [q, k, v, a, q, k, v, a, wa, ba] + consts
    specs = ([chunk(GLA_QK, fwd), chunk(GLA_QK, fwd), chunk(GLA_VW, fwd), chunk(128, fwd),
              chunk(GLA_QK, bwd), chunk(GLA_QK, bwd), chunk(GLA_VW, bwd), chunk(128, bwd)]
             + [_full_spec(z) for z in ins[8:]])
    return pl.pallas_call(
        _gla_body,
        name="gla_scan",
        out_shape=[jax.ShapeDtypeStruct((t, GLA_VW), F32)] * 2,
        grid=(batch, nc + nl),
        in_specs=specs,
        out_specs=[chunk(GLA_VW, fwd), chunk(GLA_VW, bwd)],
        scratch_shapes=[pltpu.VMEM((2, GLA_VW, GLA_QK), F32)],
        compiler_params=_cparams(("parallel", "arbitrary")),
    )(*ins)


def _swa_heads(q, kcat, vcat, valid, sink_ref, h):
    lane_head = lax.broadcasted_iota(jnp.int32, (1, SWA_G * SWA_HD), 1) // SWA_HD
    acc = jnp.zeros((q.shape[0], SWA_G * SWA_HD), F32)
    for g in range(SWA_G):
        hm = lane_head == g
        s = _dot_t(jnp.where(hm, q, jnp.zeros_like(q)), kcat)
        if valid is not None:
            s = jnp.where(valid, s, -jnp.inf)
        sk = sink_ref[h * SWA_G + g]
        m = jnp.maximum(jnp.max(s, axis=-1, keepdims=True), sk)
        p = jnp.exp(s - m)
        den = jnp.sum(p, axis=-1, keepdims=True) + jnp.exp(sk - m)
        og = _dot(p.astype(MX), jnp.where(hm, vcat, jnp.zeros_like(vcat)))
        acc = acc + og * (1.0 / den)
    return acc


def _swa_latent_body(sink_ref, q_ref, kc_ref, vc_ref, kp_ref, k0_ref, kn_ref,
                     vp_ref, v0_ref, vn_ref, o_ref, *, nblk):
    n = pl.program_id(1)
    blk = q_ref.shape[0]
    nctx = kc_ref.shape[0]
    shape = (blk, nctx + 3 * blk)
    qi = lax.broadcasted_iota(jnp.int32, shape, 0)
    col = lax.broadcasted_iota(jnp.int32, shape, 1)
    si = col - nctx
    kpos = (n - 1) * blk + si
    valid = (col < nctx) | ((jnp.abs(si - blk - qi) <= WINDOW) & (kpos >= 0) & (kpos < nblk * blk))
    w = SWA_G * SWA_HD
    for h in range(SWA_KV_HEADS):
        cols = slice(h * w, (h + 1) * w)
        kcat = jnp.concatenate([kc_ref[:, cols], kp_ref[:, cols], k0_ref[:, cols], kn_ref[:, cols]], axis=0)
        vcat = jnp.concatenate([vc_ref[:, cols], vp_ref[:, cols], v0_ref[:, cols], vn_ref[:, cols]], axis=0)
        o_ref[:, cols] = _swa_heads(q_ref[:, cols], kcat, vcat, valid, sink_ref, h).astype(o_ref.dtype)


def _swa_ctx_body(sink_ref, q_ref, kc_ref, vc_ref, o_ref):
    h = pl.program_id(1)
    o_ref[...] = _swa_heads(q_ref[...], kc_ref[...], vc_ref[...], None, sink_ref, h).astype(o_ref.dtype)


def _swa(sq, skr, svr, sink, batch, ctx_len, seq_len):
    t = sq.shape[0]
    blk = WINDOW
    w = SWA_G * SWA_HD
    nb = seq_len // blk
    cb = ctx_len // blk
    lat0 = batch * cb
    smem = pl.BlockSpec(memory_space=pltpu.SMEM)
    ctx_kv = pl.BlockSpec((ctx_len, w), lambda b, h, n: (b, h))

    def win(off):
        return pl.BlockSpec((blk, SWA_W), lambda b, n: (lat0 + b * nb + jnp.clip(n + off, 0, nb - 1), 0))

    ctx_all = pl.BlockSpec((ctx_len, SWA_W), lambda b, n: (b, 0))
    lat = pl.pallas_call(
        functools.partial(_swa_latent_body, nblk=nb),
        name="swa_latent",
        out_shape=jax.ShapeDtypeStruct((batch * seq_len, SWA_W), MX),
        grid=(batch, nb),
        in_specs=[smem, pl.BlockSpec((blk, SWA_W), lambda b, n: (lat0 + b * nb + n, 0)),
                  ctx_all, ctx_all, win(-1), win(0), win(1), win(-1), win(0), win(1)],
        out_specs=pl.BlockSpec((blk, SWA_W), lambda b, n: (b * nb + n, 0)),
        compiler_params=_cparams(("parallel", "arbitrary")),
    )(sink, sq, skr, svr, skr, skr, skr, svr, svr, svr)
    ctx = pl.pallas_call(
        _swa_ctx_body,
        name="swa_ctx",
        out_shape=jax.ShapeDtypeStruct((batch * ctx_len, SWA_W), MX),
        grid=(batch, SWA_KV_HEADS, cb),
        in_specs=[smem, pl.BlockSpec((blk, w), lambda b, h, n: (b * cb + n, h)), ctx_kv, ctx_kv],
        out_specs=pl.BlockSpec((blk, w), lambda b, h, n: (b * cb + n, h)),
        compiler_params=_cparams(("parallel", "parallel", "arbitrary")),
    )(sink, sq, skr, svr)
    return jnp.concatenate([ctx, lat], axis=0)


def _residual(h, y, gpost, gate):
    return h + gate * _rms(y, gpost)


def _even_out_body(h_ref, of_ref, ob_ref, gg_ref, a_ref, gn_ref, w1_ref, w2_ref, gpost_ref,
                   gate_ref, o_ref):
    o = of_ref[...] + ob_ref[...]
    gn = gn_ref[...]
    parts = [_rms(o[:, j * GLA_DV:(j + 1) * GLA_DV], gn) for j in range(GLA_HEADS)]
    gl = jnp.concatenate(parts, axis=-1) * _silu(gg_ref[...])
    y = _dot(gl.astype(MX), w1_ref[...]) + _dot(a_ref[...], w2_ref[...])
    o_ref[...] = _residual(h_ref[...], y, gpost_ref[...], gate_ref[...])


def _even_out(h, o_f, o_b, gg, a_swa, gn, w1, w2, gpost, mod4, seg_fn):
    t, d = h.shape
    tm = ROW_TILE
    row = lambda n: pl.BlockSpec((tm, n), lambda i: (i, 0))
    return pl.pallas_call(
        _even_out_body,
        name="even_out",
        out_shape=jax.ShapeDtypeStruct((t, d), F32),
        grid=(t // tm,),
        in_specs=[row(d), row(GLA_VW), row(GLA_VW), row(GLA_VW), row(SWA_W), _full_spec(gn),
                  _full_spec(w1), _full_spec(w2), _full_spec(gpost), _mod_spec(seg_fn, 2, d)],
        out_specs=row(d),
        compiler_params=_cparams(("parallel",)),
    )(h, o_f, o_b, gg, a_swa, gn, w1, w2, gpost, mod4)


def _odd_out_body(h_ref, o_ref_in, w_ref, gpost_ref, gate_ref, o_ref):
    y = _dot(o_ref_in[...], w_ref[...])
    o_ref[...] = _residual(h_ref[...], y, gpost_ref[...], gate_ref[...])


def _odd_out(h, h_off, o, w, gpost, mod4, seg_fn):
    t, d = o.shape[0], h.shape[1]
    tm = ROW_TILE
    row = lambda n: pl.BlockSpec((tm, n), lambda i: (i, 0))
    return pl.pallas_call(
        _odd_out_body,
        name="odd_out",
        out_shape=jax.ShapeDtypeStruct((t, d), F32),
        grid=(t // tm,),
        in_specs=[pl.BlockSpec((tm, d), lambda i: (i + h_off, 0)), row(o.shape[1]),
                  _full_spec(w), _full_spec(gpost), _mod_spec(seg_fn, 2, d)],
        out_specs=row(d),
        compiler_params=_cparams(("parallel",)),
    )(h, o, w, gpost, mod4)


def _odd_in_body(h_ref, g_ref, sh_ref, sc_ref, w_ref, qn_ref, kvn_ref, cos_ref, sin_ref,
                 cq_ref, ckv_ref, kr_ref):
    u = _modulate(h_ref[...], g_ref[...], sh_ref[...], sc_ref[...]).astype(MX)
    cq_ref[...] = _rms(_dot(u, w_ref[:, 0:MLA_Q_RANK]), qn_ref[...]).astype(cq_ref.dtype)
    c1 = MLA_Q_RANK + MLA_KV_RANK
    ckv_ref[...] = _rms(_dot(u, w_ref[:, MLA_Q_RANK:c1]), kvn_ref[...]).astype(ckv_ref.dtype)
    kr = _dot(u, w_ref[:, c1:c1 + 128])
    kr_ref[...] = _rope(kr, cos_ref[...], sin_ref[...]).astype(kr_ref.dtype)


def _odd_in(h, g_pre, mod4, w_cat, qn, kvn, cos_t, sin_t, seg_fn, pos_fn):
    t, d = h.shape
    tm = ROW_TILE
    row = lambda n: pl.BlockSpec((tm, n), lambda i: (i, 0))
    tab = pl.BlockSpec((tm, 128), lambda i: (pos_fn(i), 0))
    return pl.pallas_call(
        _odd_in_body,
        name="odd_in",
        out_shape=[jax.ShapeDtypeStruct((t, MLA_Q_RANK), MX),
                   jax.ShapeDtypeStruct((t, MLA_KV_RANK), MX),
                   jax.ShapeDtypeStruct((t, 128), MX)],
        grid=(t // tm,),
        in_specs=[row(d), _full_spec(g_pre), _mod_spec(seg_fn, 0, d), _mod_spec(seg_fn, 1, d),
                  _full_spec(w_cat), _full_spec(qn), _full_spec(kvn), tab, tab],
        out_specs=[row(MLA_Q_RANK), row(MLA_KV_RANK), row(128)],
        compiler_params=_cparams(("parallel",)),
    )(h, g_pre, mod4, mod4, w_cat, qn, kvn, cos_t, sin_t)


def _q_up_body(cq_ref, w_ref, cos_ref, sin_ref, o_ref):
    cq, cos, sin = cq_ref[...], cos_ref[...], sin_ref[...]
    scale = MLA_SCALE * math.log2(math.e)
    for h in range(MLA_HEADS):
        c0 = h * MLA_QK_PAD
        z = _dot(cq, w_ref[:, c0:c0 + MLA_QK_PAD])
        o_ref[:, c0:c0 + MLA_NOPE] = (z[:, :MLA_NOPE] * scale).astype(o_ref.dtype)
        zr = _rope(z[:, MLA_NOPE:], cos, sin)
        o_ref[:, c0 + MLA_NOPE:c0 + MLA_QK_PAD] = (zr * scale).astype(o_ref.dtype)


def _q_up(cq, row_off, nrows, w_pad, cos_t, sin_t, pos_fn):
    tm = ROW_TILE
    width = MLA_HEADS * MLA_QK_PAD
    return pl.pallas_call(
        _q_up_body,
        name="mla_q_up",
        out_shape=jax.ShapeDtypeStruct((nrows, width), MX),
        grid=(nrows // tm,),
        in_specs=[pl.BlockSpec((tm, MLA_Q_RANK), lambda i: (i + row_off, 0)),
                  _full_spec(w_pad),
                  pl.BlockSpec((tm, 128), lambda i: (pos_fn(i + row_off), 0)),
                  pl.BlockSpec((tm, 128), lambda i: (pos_fn(i + row_off), 0))],
        out_specs=pl.BlockSpec((tm, width), lambda i: (i, 0)),
        compiler_params=_cparams(("parallel",)),
    )(cq, w_pad, cos_t, sin_t)


def _kv_up_body(ckv_ref, kr_ref, w_ref, k_ref, v_ref):
    ckv, kr = ckv_ref[...], kr_ref[...]
    for h in range(MLA_HEADS):
        c0 = h * (MLA_NOPE + MLA_V)
        z = _dot(ckv, w_ref[:, c0:c0 + MLA_NOPE + MLA_V])
        k0 = h * MLA_QK_PAD
        k_ref[:, k0:k0 + MLA_NOPE] = z[:, :MLA_NOPE].astype(k_ref.dtype)
        k_ref[:, k0 + MLA_NOPE:k0 + MLA_QK_PAD] = kr
        v0 = h * MLA_VW
        v_ref[:, v0:v0 + MLA_V] = z[:, MLA_NOPE:].astype(v_ref.dtype)
        v_ref[:, v0 + MLA_V:v0 + MLA_VW] = jnp.ones((z.shape[0], MLA_VW - MLA_V), v_ref.dtype)


def _kv_up(ckv, kr, row_off, nrows, tm, w_ukv):
    return pl.pallas_call(
        _kv_up_body,
        name="mla_kv_up",
        out_shape=[jax.ShapeDtypeStruct((nrows, MLA_HEADS * MLA_QK_PAD), MX),
                   jax.ShapeDtypeStruct((nrows, MLA_HEADS * MLA_VW), MX)],
        grid=(nrows // tm,),
        in_specs=[pl.BlockSpec((tm, MLA_KV_RANK), lambda i: (i + row_off, 0)),
                  pl.BlockSpec((tm, 128), lambda i: (i + row_off, 0)),
                  _full_spec(w_ukv)],
        out_specs=[pl.BlockSpec((tm, MLA_HEADS * MLA_QK_PAD), lambda i: (i, 0)),
                   pl.BlockSpec((tm, MLA_HEADS * MLA_VW), lambda i: (i, 0))],
        compiler_params=_cparams(("parallel",)),
    )(ckv, kr, w_ukv)


def _mla_attn_body(q_ref, kc_ref, vc_ref, k_ref, v_ref, o_ref, *, tk, unroll):
    q = q_ref[...]
    s = _dot_t(q, kc_ref[...])
    m = jnp.max(s, axis=-1, keepdims=True)
    acc = _dot(jnp.exp2((s - m).astype(MX)), vc_ref[...])

    def step(j, carry):
        m, acc = carry
        start = pl.multiple_of(j * tk, tk)
        s = _dot_t(q, k_ref[pl.ds(start, tk), :])
        mn = jnp.maximum(m, jnp.max(s, axis=-1, keepdims=True))
        p = jnp.exp2((s - mn).astype(MX))
        acc = jnp.exp2(m - mn) * acc + _dot(p, v_ref[pl.ds(start, tk), :])
        return mn, acc

    m, acc = lax.fori_loop(0, k_ref.shape[0] // tk, step, (m, acc), unroll=unroll)
    o_ref[...] = (acc[:, :MLA_V] * (1.0 / acc[:, MLA_V:MLA_V + 1])).astype(o_ref.dtype)


def _mla_attn(q, k_ctx, v_ctx, k_lat, v_lat, batch, ctx_len, seq_len, tq=1024, tk=512, unroll=16):
    nq = seq_len // tq
    return pl.pallas_call(
        functools.partial(_mla_attn_body, tk=tk, unroll=unroll),
        name="mla_attn",
        out_shape=jax.ShapeDtypeStruct((batch * seq_len, MLA_HEADS * MLA_V), MX),
        grid=(batch, MLA_HEADS, nq),
        in_specs=[pl.BlockSpec((tq, MLA_QK_PAD), lambda b, h, i: (b * nq + i, h)),
                  pl.BlockSpec((ctx_len, MLA_QK_PAD), lambda b, h, i: (b, h)),
                  pl.BlockSpec((ctx_len, MLA_VW), lambda b, h, i: (b, h)),
                  pl.BlockSpec((seq_len, MLA_QK_PAD), lambda b, h, i: (b, h)),
                  pl.BlockSpec((seq_len, MLA_VW), lambda b, h, i: (b, h))],
        out_specs=pl.BlockSpec((tq, MLA_V), lambda b, h, i: (b * nq + i, h)),
        compiler_params=_cparams(("parallel", "parallel", "arbitrary")),
    )(q, k_ctx, v_ctx, k_lat, v_lat)


def _route(scores, sel):
    e, t = sel.shape
    neg = -jnp.inf
    x3 = sel.reshape(N_GROUPS, GROUP_SIZE, t)
    pos = lax.broadcasted_iota(jnp.int32, x3.shape, 1)
    m1 = jnp.max(x3, axis=1, keepdims=True)
    i1 = jnp.min(jnp.where(x3 == m1, pos, GROUP_SIZE), axis=1, keepdims=True)
    m2 = jnp.max(jnp.where(pos == i1, neg, x3), axis=1, keepdims=True)
    gs = m1 + m2
    gid = lax.broadcasted_iota(jnp.int32, gs.shape, 0)
    beaten = jnp.zeros(gs.shape, jnp.int32)
    for g in range(N_GROUPS):
        other = gs[g:g + 1]
        beaten = beaten + jnp.where((other > gs) | ((other == gs) & (g < gid)), 1, 0)
    cur = jnp.where(beaten < TOPK_GROUPS, x3, neg).reshape(e, t)
    row = lax.broadcasted_iota(jnp.int32, (e, t), 0)
    chosen = jnp.zeros((e, t), F32)
    for _ in range(TOP_K):
        m = jnp.max(cur, axis=0, keepdims=True)
        i = jnp.min(jnp.where(cur == m, row, e), axis=0, keepdims=True)
        hit = row == i
        chosen = jnp.where(hit, 1.0, chosen)
        cur = jnp.where(hit, neg, cur)
    w = chosen * scores
    return chosen, w / jnp.sum(w, axis=0, keepdims=True) * ROUTED_SCALE


def _ffn_pre_body(h_ref, g_ref, sh_ref, sc_ref, wr_ref, br_ref, us_ref, ls_ref,
                  v_ref, dst_ref, wt_ref, cpad_ref):
    vl = _modulate(h_ref[...], g_ref[...], sh_ref[...], sc_ref[...])
    v_ref[...] = vl.astype(v_ref.dtype)
    logits = lax.dot_general(wr_ref[...], vl, (((1,), (1,)), ((), ())), precision=HI,
                             preferred_element_type=F32)
    scores = _sigmoid(logits)
    chosen, w = _route(scores, scores + br_ref[...])
    rank = _dot(chosen.astype(MX), us_ref[...])
    cnt = jnp.sum(chosen, axis=1, keepdims=True)
    cpad = jnp.ceil(cnt * (1.0 / MOE_PIECE)) * MOE_PIECE
    cpad_b = jnp.broadcast_to(cpad, (cpad.shape[0], 128))
    loff = _dot(ls_ref[...], cpad_b.astype(MX))
    dst_ref[...] = jnp.where(chosen > 0.0, loff[:, 0:1] + rank, -1.0).astype(jnp.int32)
    wt_ref[...] = w
    cpad_ref[...] = cpad_b.astype(jnp.int32)


def _ffn_pre(h, g_pre, mod4, w_router, b_router, seg_fn):
    t, d = h.shape
    tm = MOE_TB
    nb = t // tm
    row = lambda n: pl.BlockSpec((tm, n), lambda i: (i, 0))
    blk = lambda n: pl.BlockSpec((None, N_EXPERTS, n), lambda i: (i, 0, 0))
    us = jnp.asarray(np.triu(np.ones((tm, tm), np.float32), 1), MX)
    ls = jnp.asarray(np.tril(np.ones((N_EXPERTS, N_EXPERTS), np.float32), -1), MX)
    wr_t = w_router.T
    br_t = b_router.reshape(N_EXPERTS, 1)
    return pl.pallas_call(
        _ffn_pre_body,
        name="ffn_pre_router",
        out_shape=[jax.ShapeDtypeStruct((t, d), MX),
                   jax.ShapeDtypeStruct((nb, N_EXPERTS, tm), jnp.int32),
                   jax.ShapeDtypeStruct((nb, N_EXPERTS, tm), F32),
                   jax.ShapeDtypeStruct((nb, N_EXPERTS, 128), jnp.int32)],
        grid=(nb,),
        in_specs=[row(d), _full_spec(g_pre), _mod_spec(seg_fn, 3, d), _mod_spec(seg_fn, 4, d),
                  _full_spec(wr_t), _full_spec(br_t), _full_spec(us), _full_spec(ls)],
        out_specs=[row(d), blk(tm), blk(tm), blk(128)],
        compiler_params=_cparams(("parallel",)),
    )(h, g_pre, mod4, mod4, wr_t, br_t, us, ls)


MOE_TB = 512
MOE_PIECE = 16
MOE_CAP = TOP_K * MOE_TB + N_EXPERTS * MOE_PIECE
MOE_NPIECE = MOE_CAP // MOE_PIECE
MOE_CHUNK = 1024
MOE_PPC = MOE_CHUNK // MOE_PIECE
MOE_RB = 512


def _dispatch_plan(cpad, t):
    nb = cpad.shape[0]
    loff = jnp.cumsum(cpad, axis=1) - cpad
    tot = jnp.sum(cpad, axis=0)
    reg = (tot + MOE_RB - 1) // MOE_RB * MOE_RB
    gend = jnp.cumsum(reg)
    gbase = gend - reg
    goff = gbase[None] + jnp.cumsum(cpad, axis=0) - cpad
    prow = jnp.arange(MOE_NPIECE, dtype=jnp.int32) * MOE_PIECE
    pexp = jnp.sum((prow[None, :, None] >= (loff + cpad)[:, None, :]).astype(jnp.int32), axis=2)
    pexp = jnp.minimum(pexp, N_EXPERTS - 1)
    own = pexp[:, :, None] == jnp.arange(N_EXPERTS, dtype=jnp.int32)[None, None, :]
    gdst = jnp.sum(jnp.where(own, (goff - loff)[:, None, :], 0), axis=2) + prow[None]
    ng = t * TOP_K + nb * N_EXPERTS * (MOE_PIECE - 1) + N_EXPERTS * (MOE_RB - 1)
    ng = -(-ng // MOE_RB) * MOE_RB
    start = jnp.arange(ng // MOE_RB, dtype=jnp.int32) * MOE_RB
    blk_e = jnp.sum((start[:, None] >= gend[None, :]).astype(jnp.int32), axis=1)
    blk_e = jnp.minimum(blk_e, N_EXPERTS - 1)
    active = (start < (gbase + tot)[blk_e]).astype(jnp.int32)
    i32 = lambda z: z.astype(jnp.int32)
    return dict(pexp=i32(pexp), gdst=i32(gdst), ntot=i32(jnp.sum(cpad, axis=1) // MOE_PIECE),
                toff=i32(gbase + tot), tnp=i32((reg - tot) // MOE_PIECE),
                nused=i32(gend[-1:] // MOE_RB), blk_e=blk_e, active=active, ng=ng)


def _onehot_rows(pexp_ref, dst_ref, blk, c, val_of, out_ref):
    rows = lax.broadcasted_iota(jnp.int32, (MOE_PIECE, MOE_TB), 0)
    for p in range(MOE_PPC):
        e = pexp_ref[blk, c * MOE_PPC + p]
        hit = dst_ref[pl.ds(e, 1), :] == rows + (c * MOE_CHUNK + p * MOE_PIECE)
        out_ref[p * MOE_PIECE:(p + 1) * MOE_PIECE, :] = jnp.where(hit, val_of(e), 0.0).astype(out_ref.dtype)


def _repeat(n, fn):
    def body(_, c):
        fn()
        return c

    lax.fori_loop(0, n, body, 0)


def _start_chunk_pieces(n, c, start_fn):
    @pl.when(n >= (c + 1) * MOE_PPC)
    def _():
        for p in range(MOE_PPC):
            start_fn(c * MOE_PPC + p)

    @pl.when((n > c * MOE_PPC) & (n < (c + 1) * MOE_PPC))
    def _():
        def body(q, carry):
            start_fn(q)
            return carry

        lax.fori_loop(c * MOE_PPC, n, body, 0)


def _wait_pieces(n, wait_chunk, wait_piece):
    _repeat(n // MOE_PPC, wait_chunk)
    _repeat(n % MOE_PPC, wait_piece)


def _moe_sort_body(pexp_ref, gdst_ref, ntot_ref, toff_ref, tnp_ref, nused_ref, x_ref, dst_ref,
                   xg_ref, xs_ref, pi_ref, zb_ref, sem, zsem):
    i, nb = pl.program_id(0), pl.num_programs(0)
    slot = i % 2

    def piece_copy(s, lo, go):
        return pltpu.make_async_copy(xs_ref.at[s, pl.ds(lo, MOE_PIECE)],
                                     xg_ref.at[pl.ds(go, MOE_PIECE)], sem.at[s])

    def zero_copy(go):
        return pltpu.make_async_copy(zb_ref.at[pl.ds(0, MOE_PIECE)], xg_ref.at[pl.ds(go, MOE_PIECE)],
                                     zsem.at[0])

    def zero_block(b):
        return pltpu.make_async_copy(zb_ref, xg_ref.at[pl.ds(pl.multiple_of(b * MOE_RB, MOE_RB), MOE_RB)],
                                     zsem.at[1])

    def wait_slot(s, n):
        def chunk():
            pltpu.make_async_copy(xs_ref.at[s, pl.ds(0, MOE_CHUNK)], xg_ref.at[pl.ds(0, MOE_CHUNK)],
                                  sem.at[s]).wait()

        _wait_pieces(n, chunk, lambda: piece_copy(s, 0, 0).wait())

    def start(q):
        piece_copy(slot, pl.multiple_of(q * MOE_PIECE, MOE_PIECE),
                   pl.multiple_of(gdst_ref[i, q], MOE_PIECE)).start()

    @pl.when(i >= 2)
    def _():
        wait_slot(slot, ntot_ref[i - 2])

    x = x_ref[...]
    for c in range(MOE_CAP // MOE_CHUNK):
        @pl.when(c * MOE_PPC < ntot_ref[i])
        def _():
            _onehot_rows(pexp_ref, dst_ref, i, c, lambda e: 1.0, pi_ref)
            xs_ref[slot, c * MOE_CHUNK:(c + 1) * MOE_CHUNK, :] = _dot(pi_ref[...], x).astype(xs_ref.dtype)

        _start_chunk_pieces(ntot_ref[i], c, start)

    @pl.when(i == nb - 1)
    def _():
        zb_ref[...] = jnp.zeros_like(zb_ref)

        def tail(e, n):
            def piece(p, c):
                zero_copy(pl.multiple_of(toff_ref[e] + p * MOE_PIECE, MOE_PIECE)).start()
                return c

            lax.fori_loop(0, tnp_ref[e], piece, 0)
            return n + tnp_ref[e]

        nz = lax.fori_loop(0, N_EXPERTS, tail, 0)
        nblk = xg_ref.shape[0] // MOE_RB

        def unused(b, c):
            zero_block(b).start()
            return c

        lax.fori_loop(nused_ref[0], nblk, unused, 0)
        _repeat(nz, lambda: zero_copy(0).wait())
        _repeat(nblk - nused_ref[0], lambda: zero_block(0).wait())
        wait_slot(slot, ntot_ref[i])

    @pl.when((i == nb - 1) & (i >= 1))
    def _():
        wait_slot(1 - slot, ntot_ref[i - 1])


def _moe_expert_body(be_ref, act_ref, x_ref, wg_ref, wu_ref, wd_ref, o_ref):
    del be_ref
    i = pl.program_id(0)

    @pl.when(act_ref[i] != 0)
    def _():
        x = x_ref[...]
        hid = _silu(_dot(x, wg_ref[...].astype(MX))) * _dot(x, wu_ref[...].astype(MX))
        o_ref[...] = _dot(hid.astype(MX), wd_ref[...].astype(MX)).astype(o_ref.dtype)

    @pl.when(act_ref[i] == 0)
    def _():
        o_ref[...] = jnp.zeros_like(o_ref)


def _moe_combine_body(pexp_ref, gdst_ref, ntot_ref, yg_ref, dst_ref, wt_ref, x_ref,
                      sg_ref, su_ref, sd_ref, h_ref, gpost_ref, gate_ref, o_ref,
                      ys_ref, pw_ref, acc_ref, sem):
    i, nb = pl.program_id(0), pl.num_programs(0)
    slot = i % 2

    def piece_copy(s, lo, go):
        return pltpu.make_async_copy(yg_ref.at[pl.ds(go, MOE_PIECE)],
                                     ys_ref.at[s, pl.ds(lo, MOE_PIECE)], sem.at[s])

    def fetch(blk, s):
        n = ntot_ref[blk]

        def start(q):
            piece_copy(s, pl.multiple_of(q * MOE_PIECE, MOE_PIECE),
                       pl.multiple_of(gdst_ref[blk, q], MOE_PIECE)).start()

        for c in range(MOE_CAP // MOE_CHUNK):
            _start_chunk_pieces(n, c, start)

        def clear(q, carry):
            ys_ref[s, pl.ds(pl.multiple_of(q * MOE_PIECE, MOE_PIECE), MOE_PIECE), :] = jnp.zeros(
                (MOE_PIECE, ys_ref.shape[2]), ys_ref.dtype)
            return carry

        lax.fori_loop(n, (n + MOE_PPC - 1) // MOE_PPC * MOE_PPC, clear, 0)

    @pl.when(i == 0)
    def _():
        fetch(0, 0)

    @pl.when(i + 1 < nb)
    def _():
        fetch(i + 1, 1 - slot)

    x = x_ref[...]
    hid = _silu(_dot(x, sg_ref[...].astype(MX))) * _dot(x, su_ref[...].astype(MX))
    acc_ref[...] = _dot(hid.astype(MX), sd_ref[...].astype(MX))
    def wait_chunk():
        pltpu.make_async_copy(yg_ref.at[pl.ds(0, MOE_CHUNK)], ys_ref.at[slot, pl.ds(0, MOE_CHUNK)],
                              sem.at[slot]).wait()

    _wait_pieces(ntot_ref[i], wait_chunk, lambda: piece_copy(slot, 0, 0).wait())
    for c in range(MOE_CAP // MOE_CHUNK):
        @pl.when(c * MOE_PPC < ntot_ref[i])
        def _():
            _onehot_rows(pexp_ref, dst_ref, i, c, lambda e: wt_ref[pl.ds(e, 1), :], pw_ref)
            acc_ref[...] += lax.dot_general(
                pw_ref[...], ys_ref[slot, c * MOE_CHUNK:(c + 1) * MOE_CHUNK, :],
                (((0,), (0,)), ((), ())), preferred_element_type=F32)
    o_ref[...] = _residual(h_ref[...], acc_ref[...], gpost_ref[...], gate_ref[...])


def _moe(x, dst_t, w_t, cpad, layer, wg, wu, wd, sg, su, sd, h, gpost, mod4, seg_fn):
    t, d = x.shape
    nb = t // MOE_TB
    plan = _dispatch_plan(cpad, t)
    ng = plan['ng']
    tables = (plan['pexp'], plan['gdst'], plan['ntot'])
    etab = pl.BlockSpec((None, N_EXPERTS, MOE_TB), lambda i, *_: (i, 0, 0))
    xg = pl.pallas_call(
        _moe_sort_body,
        name="moe_sort",
        out_shape=jax.ShapeDtypeStruct((ng, d), MX),
        grid_spec=pltpu.PrefetchScalarGridSpec(
            num_scalar_prefetch=6, grid=(nb,),
            in_specs=[pl.BlockSpec((MOE_TB, d), lambda i, *_: (i, 0)), etab],
            out_specs=pl.BlockSpec(memory_space=pl.ANY),
            scratch_shapes=[pltpu.VMEM((2, MOE_CAP, d), MX), pltpu.VMEM((MOE_CHUNK, MOE_TB), MX),
                            pltpu.VMEM((MOE_RB, d), MX),
                            pltpu.SemaphoreType.DMA((2,)), pltpu.SemaphoreType.DMA((2,))]),
        compiler_params=_cparams(("arbitrary",)),
    )(*tables, plan['toff'], plan['tnp'], plan['nused'], x, dst_t)
    yg = pl.pallas_call(
        _moe_expert_body,
        name="moe_experts",
        out_shape=jax.ShapeDtypeStruct((ng, d), MX),
        grid_spec=pltpu.PrefetchScalarGridSpec(
            num_scalar_prefetch=2, grid=(ng // MOE_RB,),
            in_specs=[pl.BlockSpec((MOE_RB, d), lambda i, be, act: (jnp.where(act[i] != 0, i, 0), 0)),
                      pl.BlockSpec((None, None, d, D_EXPERT), lambda i, be, act: (layer, be[i], 0, 0)),
                      pl.BlockSpec((None, None, d, D_EXPERT), lambda i, be, act: (layer, be[i], 0, 0)),
                      pl.BlockSpec((None, None, D_EXPERT, d), lambda i, be, act: (layer, be[i], 0, 0))],
            out_specs=pl.BlockSpec((MOE_RB, d), lambda i, be, act: (i, 0))),
        compiler_params=_cparams(("arbitrary",)),
    )(plan['blk_e'], plan['active'], xg, wg, wu, wd)
    row = lambda n: pl.BlockSpec((MOE_TB, n), lambda i, *_: (i, 0))
    const = lambda arr: pl.BlockSpec(arr.shape, lambda i, *_: (0,) * arr.ndim)
    return pl.pallas_call(
        _moe_combine_body,
        name="moe_combine",
        out_shape=jax.ShapeDtypeStruct((t, d), F32),
        grid_spec=pltpu.PrefetchScalarGridSpec(
            num_scalar_prefetch=3, grid=(nb,),
            in_specs=[pl.BlockSpec(memory_space=pl.ANY), etab, etab, row(d),
                      const(sg), const(su), const(sd), row(d), const(gpost),
                      pl.BlockSpec((None, None, 1, d), lambda i, *_: (seg_fn(i), 5, 0, 0))],
            out_specs=row(d),
            scratch_shapes=[pltpu.VMEM((2, MOE_CAP, d), MX), pltpu.VMEM((MOE_CHUNK, MOE_TB), MX),
                            pltpu.VMEM((MOE_TB, d), F32), pltpu.SemaphoreType.DMA((2,))]),
        compiler_params=_cparams(("arbitrary",)),
    )(*tables, yg, dst_t, w_t, x, sg, su, sd, h, gpost, mod4)


def _rope_tables(seq_len, n_ident):
    rows = seq_len // GRID_W
    row = jnp.repeat(jnp.arange(rows, dtype=F32), GRID_W)
    col = jnp.tile(jnp.arange(GRID_W, dtype=F32), rows)
    axis_dim = 32
    inv_freq = ROPE_THETA ** (-jnp.arange(0, axis_dim, 2, dtype=F32) / axis_dim)
    ang = jnp.concatenate([row[:, None] * inv_freq, col[:, None] * inv_freq], axis=-1)
    cos, sin = jnp.cos(ang), jnp.sin(ang)
    cos64 = jnp.concatenate([cos, cos], axis=-1)
    sin64 = jnp.concatenate([-sin, sin], axis=-1)
    cos64 = jnp.concatenate([jnp.ones((n_ident, 64), F32), cos64], axis=0)
    sin64 = jnp.concatenate([jnp.zeros((n_ident, 64), F32), sin64], axis=0)
    return cos64, sin64


def _even_weights(w_in):
    d = w_in.shape[0]
    cuts = np.cumsum([GLA_QK, GLA_QK, GLA_VW, GLA_VW, GLA_RANK, GLA_RANK, SWA_W,
                      SWA_KV_HEADS * SWA_HD])
    q, k, v, g, af, ab, sq, sk, sv = jnp.split(w_in, [int(c) for c in cuts], axis=1)
    a = jnp.concatenate([af, ab, jnp.zeros((d, 128 - 2 * GLA_RANK), w_in.dtype)], axis=1)

    def rep(wkv):
        return jnp.tile(wkv.reshape(d, SWA_KV_HEADS, 1, SWA_HD), (1, 1, SWA_G, 1)).reshape(d, SWA_W)

    return jnp.concatenate([q, k, v, g, a, sq, rep(sk), rep(sv)], axis=1).astype(MX)


def _layer_tail(h, l, mod4, seg_fn, p):
    x, dst_t, w_t, cpad = _ffn_pre(h, p['g_ffn_pre'][l][None], mod4, p['w_router'][l], p['b_router'][l], seg_fn)
    return _moe(x, dst_t, w_t, cpad[:, :, 0], l, p['w_exp_gate'], p['w_exp_up'], p['w_exp_down'],
                p['w_sh_gate'][l], p['w_sh_up'][l], p['w_sh_down'][l],
                h, p['g_ffn_post'][l][None], mod4, seg_fn)


def kernel(x, c, ctx, c_ctx, w_mod, b_mod, g_mix_pre, g_mix_post, g_ffn_pre, g_ffn_post, w_in_e, gla_wa2, gla_ba, gla_norm, swa_sink, w_out_e, w_in_o, mla_q_norm, mla_kv_norm, w_uq, w_ukv, w_out_o, w_router, b_router, w_exp_gate, w_exp_up, w_exp_down, w_sh_gate, w_sh_up, w_sh_down):
    p = dict(g_ffn_pre=g_ffn_pre, g_ffn_post=g_ffn_post, w_router=w_router, b_router=b_router,
             w_exp_gate=w_exp_gate, w_exp_up=w_exp_up, w_exp_down=w_exp_down,
             w_sh_gate=w_sh_gate, w_sh_up=w_sh_up, w_sh_down=w_sh_down)
    batch, seq_len, d = x.shape
    ctx_len = ctx.shape[1]
    depth = w_mod.shape[0]
    assert depth == 2 and batch * ctx_len == ROW_TILE and seq_len % ROW_TILE == 0
    lat_blocks = seq_len // ROW_TILE

    cvec = jnp.concatenate([c, c_ctx[None], jnp.zeros((8 - batch - 1, d), F32)], axis=0)
    mod = _mod_vectors(cvec, w_mod, b_mod)
    h = jnp.concatenate([ctx.reshape(batch * ctx_len, d), x.reshape(batch * seq_len, d)], axis=0)

    def seg_fns(tile):
        cb, lb = batch * ctx_len // tile, seq_len // tile
        return (lambda i: jnp.where(i < cb, batch, (i - cb) // lb)), (lambda i: i // lb)

    seg_all, seg_lat = seg_fns(ROW_TILE)
    moe_all, moe_lat = seg_fns(MOE_TB)
    pos_all = lambda i: jnp.where(i == 0, 0, 1 + (i - 1) % lat_blocks)
    cos64, sin64 = _rope_tables(seq_len, ROW_TILE)

    mod4 = mod[0].reshape(8, 6, 1, d)
    q, k, v, gg, a, sq, skr, svr = _even_in(
        h, g_mix_pre[0][None], mod4, _even_weights(w_in_e[0]),
        jnp.tile(cos64, (1, 2)), jnp.tile(sin64, (1, 2)), seg_all, pos_all)
    wa = jnp.zeros((2, 128, GLA_QK), F32)
    wa = wa.at[0, :GLA_RANK].set(gla_wa2[0, 0]).at[1, GLA_RANK:2 * GLA_RANK].set(gla_wa2[0, 1])
    o_f, o_b = _gla(q, k, v, a, wa, gla_ba[0][:, None, :], batch, ctx_len, seq_len)
    a_swa = _swa(sq, skr, svr, swa_sink[0], batch, ctx_len, seq_len)
    w_out = w_out_e[0].astype(MX)
    h = _even_out(h, o_f, o_b, gg, a_swa, gla_norm[0][None], w_out[:GLA_VW], w_out[GLA_VW:],
                  g_mix_post[0][None], mod4, seg_all)
    h = _layer_tail(h, 0, mod4, moe_all, p)

    mod4 = mod[1].reshape(8, 6, 1, d)
    ones = jnp.ones_like(cos64)
    cos_h = jnp.concatenate([cos64, ones], axis=1)
    sin_h = jnp.concatenate([sin64, 0 * ones], axis=1)
    w_in = jnp.concatenate([w_in_o[0], jnp.zeros((d, 128 - MLA_ROPE), F32)], axis=1).astype(MX)
    cq, ckv, kr = _odd_in(h, g_mix_pre[1][None], mod4, w_in, mla_q_norm[0][None],
                          mla_kv_norm[0][None], cos_h, sin_h, seg_all, pos_all)
    w_q = w_uq[0].reshape(MLA_Q_RANK, MLA_HEADS, MLA_NOPE + MLA_ROPE)
    w_q = jnp.pad(w_q, ((0, 0), (0, 0), (0, MLA_QK_PAD - MLA_NOPE - MLA_ROPE)))
    w_q = w_q.reshape(MLA_Q_RANK, MLA_HEADS * MLA_QK_PAD).astype(MX)
    n_lat = batch * seq_len
    qh = _q_up(cq, 1, n_lat, w_q, cos_h, sin_h, pos_all)
    w_kv = w_ukv[0].astype(MX)
    k_lat, v_lat = _kv_up(ckv, kr, 1, n_lat, ROW_TILE, w_kv)
    k_ctx, v_ctx = _kv_up(ckv, kr, 0, batch * ctx_len, ctx_len, w_kv)
    o = _mla_attn(qh, k_ctx, v_ctx, k_lat, v_lat, batch, ctx_len, seq_len)
    hl = _odd_out(h, 1, o, w_out_o[0].astype(MX), g_mix_post[1][None], mod4, seg_lat)
    hl = _layer_tail(hl, 1, mod4, moe_lat, p)
    return hl.reshape(batch, seq_len, d)
```

```python
import functools
import math

import jax
import jax.numpy as jnp
import numpy as np
from jax import lax
from jax.experimental import pallas as pl
from jax.experimental.pallas import tpu as pltpu

F32 = jnp.float32
MX = jnp.bfloat16
HI = lax.Precision.HIGHEST

GRID_W = 64
ROPE_THETA = 10000.0
RMS_EPS = 1e-6

GLA_HEADS, GLA_DK, GLA_DV, GLA_RANK, GLA_TAU = 4, 64, 128, 16, 16.0
GLA_QK = GLA_HEADS * GLA_DK
GLA_VW = GLA_HEADS * GLA_DV
GLA_CHUNK = 128

SWA_HEADS, SWA_KV_HEADS, SWA_HD, WINDOW = 8, 2, 64, 128
SWA_G = SWA_HEADS // SWA_KV_HEADS
SWA_W = SWA_HEADS * SWA_HD

MLA_HEADS, MLA_Q_RANK, MLA_KV_RANK = 8, 512, 256
MLA_NOPE, MLA_ROPE, MLA_V = 128, 64, 128
MLA_QK_PAD = 256
MLA_VW = 256
MLA_SCALE = (MLA_NOPE + MLA_ROPE) ** -0.5

N_EXPERTS, TOP_K, N_GROUPS, TOPK_GROUPS = 64, 8, 8, 4
GROUP_SIZE = N_EXPERTS // N_GROUPS
D_EXPERT = 256
ROUTED_SCALE = 2.5

ROW_TILE = 512
VMEM_LIMIT = 56 * 1024 * 1024


def _cparams(sem):
    return pltpu.CompilerParams(dimension_semantics=sem, vmem_limit_bytes=VMEM_LIMIT)


def _sigmoid(x):
    return 1.0 / (1.0 + jnp.exp(-x))


def _silu(x):
    return x * _sigmoid(x)


def _rms(x, g):
    ms = jnp.mean(x * x, axis=-1, keepdims=True)
    return x * lax.rsqrt(ms + RMS_EPS) * g


def _modulate(h, g, shift, scale):
    return _rms(h, g) * (1.0 + scale) + shift


def _dot(a, b):
    return jnp.dot(a, b, preferred_element_type=F32)


def _dot_t(a, b):
    return lax.dot_general(a, b, (((1,), (1,)), ((), ())), preferred_element_type=F32)


def _rope(x, cos, sin_signed):
    n = x.shape[-1]
    lane = lax.broadcasted_iota(jnp.int32, x.shape, 1)
    first = (lane % 64) < 32
    partner = jnp.where(first, pltpu.roll(x, n - 32, 1), pltpu.roll(x, 32, 1))
    return x * cos + partner * sin_signed


def _mod_body(c_ref, w_ref, b_ref, o_ref):
    s = _silu(c_ref[...])
    o_ref[0] = jnp.dot(s, w_ref[0], precision=HI, preferred_element_type=F32) + b_ref[0]


def _mod_vectors(cvec, w_mod, b_mod):
    depth, d, n = w_mod.shape
    tn = 1536
    return pl.pallas_call(
        _mod_body,
        name="mod_vectors",
        out_shape=jax.ShapeDtypeStruct((depth, 8, n), F32),
        grid=(depth, n // tn),
        in_specs=[pl.BlockSpec((8, d), lambda l, j: (0, 0)),
                  pl.BlockSpec((1, d, tn), lambda l, j: (l, 0, j)),
                  pl.BlockSpec((1, 1, tn), lambda l, j: (l, 0, j))],
        out_specs=pl.BlockSpec((1, 8, tn), lambda l, j: (l, 0, j)),
        compiler_params=_cparams(("arbitrary", "arbitrary")),
    )(cvec, w_mod, b_mod.reshape(depth, 1, n))


def _mod_spec(seg_fn, which, d):
    return pl.BlockSpec((None, None, 1, d), lambda i, *_: (seg_fn(i), which, 0, 0))


def _full_spec(arr):
    nd = arr.ndim
    return pl.BlockSpec(arr.shape, lambda *_: (0,) * nd)


_EV_Q, _EV_K, _EV_V, _EV_G, _EV_A, _EV_SQ, _EV_SK, _EV_SV, _EV_END = (
    0, 256, 512, 1024, 1536, 1664, 2176, 2688, 3200)


def _even_in_body(h_ref, g_ref, sh_ref, sc_ref, w_ref, cos_ref, sin_ref,
                  q_ref, k_ref, v_ref, gg_ref, a_ref, sq_ref, sk_ref, sv_ref):
    u = _modulate(h_ref[...], g_ref[...], sh_ref[...], sc_ref[...]).astype(MX)
    q_ref[...] = _dot(u, w_ref[:, _EV_Q:_EV_K]) * (GLA_DK ** -0.5)
    k_ref[...] = _dot(u, w_ref[:, _EV_K:_EV_V])
    v_ref[...] = _dot(u, w_ref[:, _EV_V:_EV_G])
    gg_ref[...] = _dot(u, w_ref[:, _EV_G:_EV_A])
    a_ref[...] = _dot(u, w_ref[:, _EV_A:_EV_SQ])
    cos = jnp.concatenate([cos_ref[...]] * (SWA_W // 128), axis=-1)
    sin = jnp.concatenate([sin_ref[...]] * (SWA_W // 128), axis=-1)
    sq = _rope(_dot(u, w_ref[:, _EV_SQ:_EV_SK]), cos, sin)
    sq_ref[...] = (sq * (SWA_HD ** -0.5)).astype(sq_ref.dtype)
    sk_ref[...] = _rope(_dot(u, w_ref[:, _EV_SK:_EV_SV]), cos, sin).astype(sk_ref.dtype)
    sv_ref[...] = _dot(u, w_ref[:, _EV_SV:_EV_END]).astype(sv_ref.dtype)


def _even_in(h, g_pre, mod4, w_cat, cos_t, sin_t, seg_fn, pos_fn):
    t, d = h.shape
    tm = ROW_TILE
    row = lambda n: pl.BlockSpec((tm, n), lambda i: (i, 0))
    outs = [(GLA_QK, F32), (GLA_QK, F32), (GLA_VW, F32), (GLA_VW, F32), (128, F32),
            (SWA_W, MX), (SWA_W, MX), (SWA_W, MX)]
    return pl.pallas_call(
        _even_in_body,
        name="even_in",
        out_shape=[jax.ShapeDtypeStruct((t, n), dt) for n, dt in outs],
        grid=(t // tm,),
        in_specs=[row(d), _full_spec(g_pre), _mod_spec(seg_fn, 0, d), _mod_spec(seg_fn, 1, d),
                  _full_spec(w_cat),
                  pl.BlockSpec((tm, 128), lambda i: (pos_fn(i), 0)),
                  pl.BlockSpec((tm, 128), lambda i: (pos_fn(i), 0))],
        out_specs=[row(n) for n, _ in outs],
        compiler_params=_cparams(("parallel",)),
    )(h, g_pre, mod4, mod4, w_cat, cos_t, sin_t)


def _gla_consts(c, reverse):
    nlev = int(round(math.log2(c)))
    idx = np.arange(c)
    i, m = idx[:, None], idx[None, :]
    sizes = [c >> (l + 1) for l in range(nlev)]
    a = (m <= i).astype(np.float32)
    masks = [np.eye(c, dtype=bool)]
    for s in sizes:
        bi, bj = i // s, m // s
        masks.append((bi % 2 == 1) & (bj == bi - 1))
    msk = np.stack(masks).astype(np.float32)
    if reverse:
        a = a[::-1, ::-1]
        msk = msk[:, ::-1, ::-1]
    return (np.ascontiguousarray(a), np.ascontiguousarray(np.tile(msk, (1, 1, GLA_HEADS))))


def _split3(x):
    hi = x.astype(MX)
    r1 = x - hi.astype(F32)
    mid = r1.astype(MX)
    lo = (r1 - mid.astype(F32)).astype(MX)
    return hi, mid, lo


def _gla_level_exponents(la, cum, s, reverse):
    c, w = la.shape
    nblk = c // s
    if s >= 8:
        zero = jnp.zeros((1, w), F32)
        if not reverse:
            qrows = [zero if b == 0 else cum[b * s - 1:b * s] for b in range(nblk)]
            krows = [cum[(b + 1) * s - 1:(b + 1) * s] for b in range(nblk)]
        else:
            qrows = [zero if b == nblk - 1 else cum[(b + 1) * s:(b + 1) * s + 1] for b in range(nblk)]
            krows = [cum[b * s:b * s + 1] for b in range(nblk)]
        spread = lambda rows: jnp.concatenate([jnp.broadcast_to(r, (s, w)) for r in rows], axis=0)
        return cum - spread(qrows), spread(krows) - cum
    pos = lax.broadcasted_iota(jnp.int32, (c, 1), 0) & (s - 1)
    qe, ke = la, jnp.zeros_like(la)
    for d in range(1, s):
        before = jnp.where(pos >= d, pltpu.roll(la, d, 0), 0.0)
        after = jnp.where(pos + d <= s - 1, pltpu.roll(la, c - d, 0), 0.0)
        qe, ke = (qe + after, ke + before) if reverse else (qe + before, ke + after)
    return qe, ke


def _gla_direction(q, k, v, a, wa, ba, amat, lmask, hmask, vmask, bdmask, st_ref, d, reverse):
    c = q.shape[0]
    nlev = int(round(math.log2(c)))
    x = jnp.dot(a, wa, precision=HI, preferred_element_type=F32) + ba
    la = (jnp.minimum(x, 0.0) - jnp.log1p(jnp.exp(-jnp.abs(x)))) * (1.0 / GLA_TAU)
    hi, mid, lo = _split3(la)
    cum = _dot(amat, hi) + _dot(amat, mid) + _dot(amat, lo)
    last = cum[0:1] if reverse else cum[c - 1:c]
    qd = (q * jnp.exp(cum)).astype(MX)
    kd = (k * jnp.exp(last - cum)).astype(MX)
    st = st_ref[d]
    o = _dot_t(qd, st.astype(MX))
    scat = jnp.zeros((c, GLA_HEADS * c), F32)
    for lev in range(nlev + 1):
        if lev == 0:
            ql, kl = q, k
        else:
            qe, ke = _gla_level_exponents(la, cum, c >> lev, reverse)
            ql, kl = q * jnp.exp(qe), k * jnp.exp(ke)
        kst = jnp.concatenate([kl.astype(MX)] * GLA_HEADS, axis=0) * hmask
        scat = scat + _dot_t(ql.astype(MX), kst) * lmask[lev]
    vbd = jnp.concatenate([v.astype(MX)] * GLA_HEADS, axis=0) * vmask
    o = o + _dot(scat.astype(MX), vbd)
    upd = lax.dot_general(v.astype(MX), kd, (((0,), (0,)), ((), ())), preferred_element_type=F32)
    st_ref[d] = st * jnp.exp(last) + upd * bdmask
    return o


def _gla_body(qf_ref, kf_ref, vf_ref, af_ref, qb_ref, kb_ref, vb_ref, ab_ref,
              wa_ref, ba_ref, amf_ref, lmf_ref, amb_ref, lmb_ref, hm_ref, vm_ref, bd_ref,
              of_ref, ob_ref, st_ref):
    @pl.when(pl.program_id(1) == 0)
    def _():
        st_ref[...] = jnp.zeros_like(st_ref)

    hm, vm, bd = hm_ref[...], vm_ref[...], bd_ref[...]
    of_ref[...] = _gla_direction(qf_ref[...], kf_ref[...], vf_ref[...], af_ref[...],
                                 wa_ref[0], ba_ref[0], amf_ref[...], lmf_ref, hm, vm, bd,
                                 st_ref, 0, False)
    ob_ref[...] = _gla_direction(qb_ref[...], kb_ref[...], vb_ref[...], ab_ref[...],
                                 wa_ref[1], ba_ref[1], amb_ref[...], lmb_ref, hm, vm, bd,
                                 st_ref, 1, True)


def _gla(q, k, v, a, wa, ba, batch, ctx_len, seq_len):
    t = q.shape[0]
    c = GLA_CHUNK
    nc, nl = ctx_len // c, seq_len // c
    amf, lmf = _gla_consts(c, False)
    amb, lmb = _gla_consts(c, True)
    r = np.arange(GLA_HEADS * c)[:, None] // c
    hm = (r == np.arange(GLA_QK)[None, :] // GLA_DK).astype(np.float32)
    vm = (r == np.arange(GLA_VW)[None, :] // GLA_DV).astype(np.float32)
    bd = (np.arange(GLA_VW)[:, None] // GLA_DV
          == np.arange(GLA_QK)[None, :] // GLA_DK).astype(np.float32)

    def fwd(b, s):
        return jnp.where(s < nc, nc * b + s, batch * nc + nl * b + (s - nc))

    def bwd(b, s):
        return jnp.where(s < nc, nc * b + (nc - 1 - s), batch * nc + nl * b + (nl - 1 - (s - nc)))

    def chunk(n, fn):
        return pl.BlockSpec((c, n), lambda b, s: (fn(b, s), 0))

    consts = [jnp.asarray(amf, MX), jnp.asarray(lmf), jnp.asarray(amb, MX), jnp.asarray(lmb),
              jnp.asarray(hm, MX), jnp.asarray(vm, MX), jnp.asarray(bd)]
    ins = [q, k, v, a, q, k, v, a, wa, ba] + consts
    specs = ([chunk(GLA_QK, fwd), chunk(GLA_QK, fwd), chunk(GLA_VW, fwd), chunk(128, fwd),
              chunk(GLA_QK, bwd), chunk(GLA_QK, bwd), chunk(GLA_VW, bwd), chunk(128, bwd)]
             + [_full_spec(z) for z in ins[8:]])
    return pl.pallas_call(
        _gla_body,
        name="gla_scan",
        out_shape=[jax.ShapeDtypeStruct((t, GLA_VW), F32)] * 2,
        grid=(batch, nc + nl),
        in_specs=specs,
        out_specs=[chunk(GLA_VW, fwd), chunk(GLA_VW, bwd)],
        scratch_shapes=[pltpu.VMEM((2, GLA_VW, GLA_QK), F32)],
        compiler_params=_cparams(("parallel", "arbitrary")),
    )(*ins)


def _swa_heads(q, kcat, vcat, valid, sink_ref, h):
    lane_head = lax.broadcasted_iota(jnp.int32, (1, SWA_G * SWA_HD), 1) // SWA_HD
    acc = jnp.zeros((q.shape[0], SWA_G * SWA_HD), F32)
    for g in range(SWA_G):
        hm = lane_head == g
        s = _dot_t(jnp.where(hm, q, jnp.zeros_like(q)), kcat)
        if valid is not None:
            s = jnp.where(valid, s, -jnp.inf)
        sk = sink_ref[h * SWA_G + g]
        m = jnp.maximum(jnp.max(s, axis=-1, keepdims=True), sk)
        p = jnp.exp(s - m)
        den = jnp.sum(p, axis=-1, keepdims=True) + jnp.exp(sk - m)
        og = _dot(p.astype(MX), jnp.where(hm, vcat, jnp.zeros_like(vcat)))
        acc = acc + og * (1.0 / den)
    return acc


def _swa_latent_body(sink_ref, q_ref, kc_ref, vc_ref, kp_ref, k0_ref, kn_ref,
                     vp_ref, v0_ref, vn_ref, o_ref, *, nblk):
    n = pl.program_id(1)
    qt, blk = q_ref.shape[0], kp_ref.shape[0]
    nctx = kc_ref.shape[0]
    shape = (qt, nctx + qt + 2 * blk)
    qi = lax.broadcasted_iota(jnp.int32, shape, 0)
    col = lax.broadcasted_iota(jnp.int32, shape, 1)
    si = col - nctx
    kpos = n * qt - blk + si
    valid = (col < nctx) | ((jnp.abs(si - blk - qi) <= WINDOW) & (kpos >= 0) & (kpos < nblk * blk))
    w = SWA_G * SWA_HD
    for h in range(SWA_KV_HEADS):
        cols = slice(h * w, (h + 1) * w)
        kcat = jnp.concatenate([kc_ref[:, cols], kp_ref[:, cols], k0_ref[:, cols], kn_ref[:, cols]], axis=0)
        vcat = jnp.concatenate([vc_ref[:, cols], vp_ref[:, cols], v0_ref[:, cols], vn_ref[:, cols]], axis=0)
        o_ref[:, cols] = _swa_heads(q_ref[:, cols], kcat, vcat, valid, sink_ref, h).astype(o_ref.dtype)


def _swa_ctx_body(sink_ref, q_ref, kc_ref, vc_ref, o_ref):
    h = pl.program_id(1)
    o_ref[...] = _swa_heads(q_ref[...], kc_ref[...], vc_ref[...], None, sink_ref, h).astype(o_ref.dtype)


def _swa(sq, skr, svr, sink, batch, ctx_len, seq_len):
    t = sq.shape[0]
    blk = WINDOW
    w = SWA_G * SWA_HD
    nb = seq_len // blk
    cb = ctx_len // blk
    lat0 = batch * cb
    smem = pl.BlockSpec(memory_space=pltpu.SMEM)
    ctx_kv = pl.BlockSpec((ctx_len, w), lambda b, h, n: (b, h))

    qb = 2
    qt, nq = qb * blk, nb // qb

    def edge(off):
        return pl.BlockSpec((blk, SWA_W), lambda b, n: (lat0 + b * nb + jnp.clip(n * qb + off, 0, nb - 1), 0))

    mid = pl.BlockSpec((qt, SWA_W), lambda b, n: (lat0 // qb + b * nq + n, 0))
    ctx_all = pl.BlockSpec((ctx_len, SWA_W), lambda b, n: (b, 0))
    lat = pl.pallas_call(
        functools.partial(_swa_latent_body, nblk=nb),
        name="swa_latent",
        out_shape=jax.ShapeDtypeStruct((batch * seq_len, SWA_W), MX),
        grid=(batch, nq),
        in_specs=[smem, mid, ctx_all, ctx_all, edge(-1), mid, edge(qb), edge(-1), mid, edge(qb)],
        out_specs=pl.BlockSpec((qt, SWA_W), lambda b, n: (b * nq + n, 0)),
        compiler_params=_cparams(("parallel", "arbitrary")),
    )(sink, sq, skr, svr, skr, skr, skr, svr, svr, svr)
    ctx = pl.pallas_call(
        _swa_ctx_body,
        name="swa_ctx",
        out_shape=jax.ShapeDtypeStruct((batch * ctx_len, SWA_W), MX),
        grid=(batch, SWA_KV_HEADS, cb),
        in_specs=[smem, pl.BlockSpec((blk, w), lambda b, h, n: (b * cb + n, h)), ctx_kv, ctx_kv],
        out_specs=pl.BlockSpec((blk, w), lambda b, h, n: (b * cb + n, h)),
        compiler_params=_cparams(("parallel", "parallel", "arbitrary")),
    )(sink, sq, skr, svr)
    return jnp.concatenate([ctx, lat], axis=0)


def _residual(h, y, gpost, gate):
    return h + gate * _rms(y, gpost)


def _even_out_body(h_ref, of_ref, ob_ref, gg_ref, a_ref, gn_ref, w1_ref, w2_ref, gpost_ref,
                   gate_ref, o_ref):
    o = of_ref[...] + ob_ref[...]
    gn = gn_ref[...]
    parts = [_rms(o[:, j * GLA_DV:(j + 1) * GLA_DV], gn) for j in range(GLA_HEADS)]
    gl = jnp.concatenate(parts, axis=-1) * _silu(gg_ref[...])
    y = _dot(gl.astype(MX), w1_ref[...]) + _dot(a_ref[...], w2_ref[...])
    o_ref[...] = _residual(h_ref[...], y, gpost_ref[...], gate_ref[...])


def _even_out(h, o_f, o_b, gg, a_swa, gn, w1, w2, gpost, mod4, seg_fn):
    t, d = h.shape
    tm = ROW_TILE
    row = lambda n: pl.BlockSpec((tm, n), lambda i: (i, 0))
    return pl.pallas_call(
        _even_out_body,
        name="even_out",
        out_shape=jax.ShapeDtypeStruct((t, d), F32),
        grid=(t // tm,),
        in_specs=[row(d), row(GLA_VW), row(GLA_VW), row(GLA_VW), row(SWA_W), _full_spec(gn),
                  _full_spec(w1), _full_spec(w2), _full_spec(gpost), _mod_spec(seg_fn, 2, d)],
        out_specs=row(d),
        compiler_params=_cparams(("parallel",)),
    )(h, o_f, o_b, gg, a_swa, gn, w1, w2, gpost, mod4)


def _odd_out_body(h_ref, o_ref_in, w_ref, gpost_ref, gate_ref, o_ref):
    y = _dot(o_ref_in[...], w_ref[...])
    o_ref[...] = _residual(h_ref[...], y, gpost_ref[...], gate_ref[...])


def _odd_out(h, h_off, o, w, gpost, mod4, seg_fn):
    t, d = o.shape[0], h.shape[1]
    tm = ROW_TILE
    row = lambda n: pl.BlockSpec((tm, n), lambda i: (i, 0))
    return pl.pallas_call(
        _odd_out_body,
        name="odd_out",
        out_shape=jax.ShapeDtypeStruct((t, d), F32),
        grid=(t // tm,),
        in_specs=[pl.BlockSpec((tm, d), lambda i: (i + h_off, 0)), row(o.shape[1]),
                  _full_spec(w), _full_spec(gpost), _mod_spec(seg_fn, 2, d)],
        out_specs=row(d),
        compiler_params=_cparams(("parallel",)),
    )(h, o, w, gpost, mod4)


def _odd_in_body(h_ref, g_ref, sh_ref, sc_ref, w_ref, qn_ref, kvn_ref, cos_ref, sin_ref,
                 cq_ref, ckv_ref, kr_ref):
    u = _modulate(h_ref[...], g_ref[...], sh_ref[...], sc_ref[...]).astype(MX)
    cq_ref[...] = _rms(_dot(u, w_ref[:, 0:MLA_Q_RANK]), qn_ref[...]).astype(cq_ref.dtype)
    c1 = MLA_Q_RANK + MLA_KV_RANK
    ckv_ref[...] = _rms(_dot(u, w_ref[:, MLA_Q_RANK:c1]), kvn_ref[...]).astype(ckv_ref.dtype)
    kr = _dot(u, w_ref[:, c1:c1 + 128])
    kr_ref[...] = _rope(kr, cos_ref[...], sin_ref[...]).astype(kr_ref.dtype)


def _odd_in(h, g_pre, mod4, w_cat, qn, kvn, cos_t, sin_t, seg_fn, pos_fn):
    t, d = h.shape
    tm = ROW_TILE
    row = lambda n: pl.BlockSpec((tm, n), lambda i: (i, 0))
    tab = pl.BlockSpec((tm, 128), lambda i: (pos_fn(i), 0))
    return pl.pallas_call(
        _odd_in_body,
        name="odd_in",
        out_shape=[jax.ShapeDtypeStruct((t, MLA_Q_RANK), MX),
                   jax.ShapeDtypeStruct((t, MLA_KV_RANK), MX),
                   jax.ShapeDtypeStruct((t, 128), MX)],
        grid=(t // tm,),
        in_specs=[row(d), _full_spec(g_pre), _mod_spec(seg_fn, 0, d), _mod_spec(seg_fn, 1, d),
                  _full_spec(w_cat), _full_spec(qn), _full_spec(kvn), tab, tab],
        out_specs=[row(MLA_Q_RANK), row(MLA_KV_RANK), row(128)],
        compiler_params=_cparams(("parallel",)),
    )(h, g_pre, mod4, mod4, w_cat, qn, kvn, cos_t, sin_t)


def _q_up_body(cq_ref, w_ref, cos_ref, sin_ref, o_ref):
    cq, cos, sin = cq_ref[...], cos_ref[...], sin_ref[...]
    scale = MLA_SCALE * math.log2(math.e)
    for h in range(MLA_HEADS):
        c0 = h * MLA_QK_PAD
        z = _dot(cq, w_ref[:, c0:c0 + MLA_QK_PAD])
        o_ref[:, c0:c0 + MLA_NOPE] = (z[:, :MLA_NOPE] * scale).astype(o_ref.dtype)
        zr = _rope(z[:, MLA_NOPE:], cos, sin)
        o_ref[:, c0 + MLA_NOPE:c0 + MLA_QK_PAD] = (zr * scale).astype(o_ref.dtype)


def _q_up(cq, row_off, nrows, w_pad, cos_t, sin_t, pos_fn):
    tm = ROW_TILE
    width = MLA_HEADS * MLA_QK_PAD
    return pl.pallas_call(
        _q_up_body,
        name="mla_q_up",
        out_shape=jax.ShapeDtypeStruct((nrows, width), MX),
        grid=(nrows // tm,),
        in_specs=[pl.BlockSpec((tm, MLA_Q_RANK), lambda i: (i + row_off, 0)),
                  _full_spec(w_pad),
                  pl.BlockSpec((tm, 128), lambda i: (pos_fn(i + row_off), 0)),
                  pl.BlockSpec((tm, 128), lambda i: (pos_fn(i + row_off), 0))],
        out_specs=pl.BlockSpec((tm, width), lambda i: (i, 0)),
        compiler_params=_cparams(("parallel",)),
    )(cq, w_pad, cos_t, sin_t)


def _kv_up_body(ckv_ref, kr_ref, w_ref, k_ref, v_ref):
    ckv, kr = ckv_ref[...], kr_ref[...]
    for h in range(MLA_HEADS):
        c0 = h * (MLA_NOPE + MLA_V)
        z = _dot(ckv, w_ref[:, c0:c0 + MLA_NOPE + MLA_V])
        k0 = h * MLA_QK_PAD
        k_ref[:, k0:k0 + MLA_NOPE] = z[:, :MLA_NOPE].astype(k_ref.dtype)
        k_ref[:, k0 + MLA_NOPE:k0 + MLA_QK_PAD] = kr
        v0 = h * MLA_VW
        v_ref[:, v0:v0 + MLA_V] = z[:, MLA_NOPE:].astype(v_ref.dtype)
        v_ref[:, v0 + MLA_V:v0 + MLA_VW] = jnp.ones((z.shape[0], MLA_VW - MLA_V), v_ref.dtype)


def _kv_up(ckv, kr, row_off, nrows, tm, w_ukv):
    return pl.pallas_call(
        _kv_up_body,
        name="mla_kv_up",
        out_shape=[jax.ShapeDtypeStruct((nrows, MLA_HEADS * MLA_QK_PAD), MX),
                   jax.ShapeDtypeStruct((nrows, MLA_HEADS * MLA_VW), MX)],
        grid=(nrows // tm,),
        in_specs=[pl.BlockSpec((tm, MLA_KV_RANK), lambda i: (i + row_off, 0)),
                  pl.BlockSpec((tm, 128), lambda i: (i + row_off, 0)),
                  _full_spec(w_ukv)],
        out_specs=[pl.BlockSpec((tm, MLA_HEADS * MLA_QK_PAD), lambda i: (i, 0)),
                   pl.BlockSpec((tm, MLA_HEADS * MLA_VW), lambda i: (i, 0))],
        compiler_params=_cparams(("parallel",)),
    )(ckv, kr, w_ukv)


def _mla_attn_body(q_ref, kc_ref, vc_ref, k_ref, v_ref, o_ref, *, tk, unroll):
    q = q_ref[...]
    s = _dot_t(q, kc_ref[...])
    m = jnp.max(s, axis=-1, keepdims=True)
    acc = _dot(jnp.exp2((s - m).astype(MX)), vc_ref[...])

    def step(j, carry):
        m, acc = carry
        start = pl.multiple_of(j * tk, tk)
        s = _dot_t(q, k_ref[pl.ds(start, tk), :])
        mn = jnp.maximum(m, jnp.max(s, axis=-1, keepdims=True))
        p = jnp.exp2((s - mn).astype(MX))
        acc = jnp.exp2(m - mn) * acc + _dot(p, v_ref[pl.ds(start, tk), :])
        return mn, acc

    m, acc = lax.fori_loop(0, k_ref.shape[0] // tk, step, (m, acc), unroll=unroll)
    o_ref[...] = (acc[:, :MLA_V] * (1.0 / acc[:, MLA_V:MLA_V + 1])).astype(o_ref.dtype)


def _mla_attn(q, k_ctx, v_ctx, k_lat, v_lat, batch, ctx_len, seq_len, tq=1024, tk=1024, unroll=8):
    nq = seq_len // tq
    return pl.pallas_call(
        functools.partial(_mla_attn_body, tk=tk, unroll=unroll),
        name="mla_attn",
        out_shape=jax.ShapeDtypeStruct((batch * seq_len, MLA_HEADS * MLA_V), MX),
        grid=(batch, MLA_HEADS, nq),
        in_specs=[pl.BlockSpec((tq, MLA_QK_PAD), lambda b, h, i: (b * nq + i, h)),
                  pl.BlockSpec((ctx_len, MLA_QK_PAD), lambda b, h, i: (b, h)),
                  pl.BlockSpec((ctx_len, MLA_VW), lambda b, h, i: (b, h)),
                  pl.BlockSpec((seq_len, MLA_QK_PAD), lambda b, h, i: (b, h)),
                  pl.BlockSpec((seq_len, MLA_VW), lambda b, h, i: (b, h))],
        out_specs=pl.BlockSpec((tq, MLA_V), lambda b, h, i: (b * nq + i, h)),
        compiler_params=_cparams(("parallel", "parallel", "arbitrary")),
    )(q, k_ctx, v_ctx, k_lat, v_lat)


def _route(scores, sel):
    e, t = sel.shape
    neg = -jnp.inf
    x3 = sel.reshape(N_GROUPS, GROUP_SIZE, t)
    pos = lax.broadcasted_iota(jnp.int32, x3.shape, 1)
    m1 = jnp.max(x3, axis=1, keepdims=True)
    i1 = jnp.min(jnp.where(x3 == m1, pos, GROUP_SIZE), axis=1, keepdims=True)
    m2 = jnp.max(jnp.where(pos == i1, neg, x3), axis=1, keepdims=True)
    gs = m1 + m2
    gid = lax.broadcasted_iota(jnp.int32, gs.shape, 0)
    beaten = jnp.zeros(gs.shape, jnp.int32)
    for g in range(N_GROUPS):
        other = gs[g:g + 1]
        beaten = beaten + jnp.where((other > gs) | ((other == gs) & (g < gid)), 1, 0)
    cur = jnp.where(beaten < TOPK_GROUPS, x3, neg).reshape(e, t)
    row = lax.broadcasted_iota(jnp.int32, (e, t), 0)
    chosen = jnp.zeros((e, t), F32)
    for _ in range(TOP_K):
        m = jnp.max(cur, axis=0, keepdims=True)
        i = jnp.min(jnp.where(cur == m, row, e), axis=0, keepdims=True)
        hit = row == i
        chosen = jnp.where(hit, 1.0, chosen)
        cur = jnp.where(hit, neg, cur)
    w = chosen * scores
    return chosen, w / jnp.sum(w, axis=0, keepdims=True) * ROUTED_SCALE


def _ffn_pre_body(h_ref, g_ref, sh_ref, sc_ref, wr_ref, br_ref, us_ref, ls_ref,
                  v_ref, dst_ref, wt_ref, cpad_ref):
    vl = _modulate(h_ref[...], g_ref[...], sh_ref[...], sc_ref[...])
    v_ref[...] = vl.astype(v_ref.dtype)
    logits = lax.dot_general(wr_ref[...], vl, (((1,), (1,)), ((), ())), precision=HI,
                             preferred_element_type=F32)
    scores = _sigmoid(logits)
    chosen, w = _route(scores, scores + br_ref[...])
    rank = _dot(chosen.astype(MX), us_ref[...])
    cnt = jnp.sum(chosen, axis=1, keepdims=True)
    cpad = jnp.ceil(cnt * (1.0 / MOE_PIECE)) * MOE_PIECE
    cpad_b = jnp.broadcast_to(cpad, (cpad.shape[0], 128))
    loff = _dot(ls_ref[...], cpad_b.astype(MX))
    dst_ref[...] = jnp.where(chosen > 0.0, loff[:, 0:1] + rank, -1.0).astype(jnp.int32)
    wt_ref[...] = w
    cpad_ref[...] = cpad_b.astype(jnp.int32)


def _ffn_pre(h, g_pre, mod4, w_router, b_router, seg_fn):
    t, d = h.shape
    tm = MOE_TB
    nb = t // tm
    row = lambda n: pl.BlockSpec((tm, n), lambda i: (i, 0))
    blk = lambda n: pl.BlockSpec((None, N_EXPERTS, n), lambda i: (i, 0, 0))
    us = jnp.asarray(np.triu(np.ones((tm, tm), np.float32), 1), MX)
    ls = jnp.asarray(np.tril(np.ones((N_EXPERTS, N_EXPERTS), np.float32), -1), MX)
    wr_t = w_router.T
    br_t = b_router.reshape(N_EXPERTS, 1)
    return pl.pallas_call(
        _ffn_pre_body,
        name="ffn_pre_router",
        out_shape=[jax.ShapeDtypeStruct((t, d), MX),
                   jax.ShapeDtypeStruct((nb, N_EXPERTS, tm), jnp.int32),
                   jax.ShapeDtypeStruct((nb, N_EXPERTS, tm), F32),
                   jax.ShapeDtypeStruct((nb, N_EXPERTS, 128), jnp.int32)],
        grid=(nb,),
        in_specs=[row(d), _full_spec(g_pre), _mod_spec(seg_fn, 3, d), _mod_spec(seg_fn, 4, d),
                  _full_spec(wr_t), _full_spec(br_t), _full_spec(us), _full_spec(ls)],
        out_specs=[row(d), blk(tm), blk(tm), blk(128)],
        compiler_params=_cparams(("parallel",)),
    )(h, g_pre, mod4, mod4, wr_t, br_t, us, ls)


MOE_TB = 512
MOE_PIECE = 16
MOE_CAP = TOP_K * MOE_TB + N_EXPERTS * MOE_PIECE
MOE_NPIECE = MOE_CAP // MOE_PIECE
MOE_CHUNK = 1024
MOE_PPC = MOE_CHUNK // MOE_PIECE
MOE_RB = 512


def _dispatch_plan(cpad, t):
    nb = cpad.shape[0]
    loff = jnp.cumsum(cpad, axis=1) - cpad
    tot = jnp.sum(cpad, axis=0)
    reg = (tot + MOE_RB - 1) // MOE_RB * MOE_RB
    gend = jnp.cumsum(reg)
    gbase = gend - reg
    goff = gbase[None] + jnp.cumsum(cpad, axis=0) - cpad
    prow = jnp.arange(MOE_NPIECE, dtype=jnp.int32) * MOE_PIECE
    pexp = jnp.sum((prow[None, :, None] >= (loff + cpad)[:, None, :]).astype(jnp.int32), axis=2)
    pexp = jnp.minimum(pexp, N_EXPERTS - 1)
    own = pexp[:, :, None] == jnp.arange(N_EXPERTS, dtype=jnp.int32)[None, None, :]
    gdst = jnp.sum(jnp.where(own, (goff - loff)[:, None, :], 0), axis=2) + prow[None]
    ng = t * TOP_K + nb * N_EXPERTS * (MOE_PIECE - 1) + N_EXPERTS * (MOE_RB - 1)
    ng = -(-ng // MOE_RB) * MOE_RB
    start = jnp.arange(ng // MOE_RB, dtype=jnp.int32) * MOE_RB
    blk_e = jnp.sum((start[:, None] >= gend[None, :]).astype(jnp.int32), axis=1)
    blk_e = jnp.minimum(blk_e, N_EXPERTS - 1)
    active = (start < (gbase + tot)[blk_e]).astype(jnp.int32)
    i32 = lambda z: z.astype(jnp.int32)
    return dict(pexp=i32(pexp), gdst=i32(gdst), ntot=i32(jnp.sum(cpad, axis=1) // MOE_PIECE),
                toff=i32(gbase + tot), tnp=i32((reg - tot) // MOE_PIECE),
                nused=i32(gend[-1:] // MOE_RB), blk_e=blk_e, active=active, ng=ng)


def _onehot_rows(pexp_ref, dst_ref, blk, c, val_of, out_ref):
    rows = lax.broadcasted_iota(jnp.int32, (MOE_PIECE, MOE_TB), 0)
    for p in range(MOE_PPC):
        e = pexp_ref[blk, c * MOE_PPC + p]
        hit = dst_ref[pl.ds(e, 1), :] == rows + (c * MOE_CHUNK + p * MOE_PIECE)
        out_ref[p * MOE_PIECE:(p + 1) * MOE_PIECE, :] = jnp.where(hit, val_of(e), 0.0).astype(out_ref.dtype)


def _repeat(n, fn):
    def body(_, c):
        fn()
        return c

    lax.fori_loop(0, n, body, 0)


def _start_chunk_pieces(n, c, start_fn):
    @pl.when(n >= (c + 1) * MOE_PPC)
    def _():
        for p in range(MOE_PPC):
            start_fn(c * MOE_PPC + p)

    @pl.when((n > c * MOE_PPC) & (n < (c + 1) * MOE_PPC))
    def _():
        def body(q, carry):
            start_fn(q)
            return carry

        lax.fori_loop(c * MOE_PPC, n, body, 0)


def _wait_pieces(n, wait_chunk, wait_piece):
    _repeat(n // MOE_PPC, wait_chunk)
    _repeat(n % MOE_PPC, wait_piece)


def _moe_sort_body(pexp_ref, gdst_ref, ntot_ref, toff_ref, tnp_ref, nused_ref, x_ref, dst_ref,
                   xg_ref, xs_ref, pi_ref, zb_ref, sem, zsem):
    i, nb = pl.program_id(0), pl.num_programs(0)
    slot = i % 2

    def piece_copy(s, lo, go):
        return pltpu.make_async_copy(xs_ref.at[s, pl.ds(lo, MOE_PIECE)],
                                     xg_ref.at[pl.ds(go, MOE_PIECE)], sem.at[s])

    def zero_copy(go):
        return pltpu.make_async_copy(zb_ref.at[pl.ds(0, MOE_PIECE)], xg_ref.at[pl.ds(go, MOE_PIECE)],
                                     zsem.at[0])

    def zero_block(b):
        return pltpu.make_async_copy(zb_ref, xg_ref.at[pl.ds(pl.multiple_of(b * MOE_RB, MOE_RB), MOE_RB)],
                                     zsem.at[1])

    def wait_slot(s, n):
        def chunk():
            pltpu.make_async_copy(xs_ref.at[s, pl.ds(0, MOE_CHUNK)], xg_ref.at[pl.ds(0, MOE_CHUNK)],
                                  sem.at[s]).wait()

        _wait_pieces(n, chunk, lambda: piece_copy(s, 0, 0).wait())

    def start(q):
        piece_copy(slot, pl.multiple_of(q * MOE_PIECE, MOE_PIECE),
                   pl.multiple_of(gdst_ref[i, q], MOE_PIECE)).start()

    @pl.when(i >= 2)
    def _():
        wait_slot(slot, ntot_ref[i - 2])

    x = x_ref[...]
    for c in range(MOE_CAP // MOE_CHUNK):
        @pl.when(c * MOE_PPC < ntot_ref[i])
        def _():
            _onehot_rows(pexp_ref, dst_ref, i, c, lambda e: 1.0, pi_ref)
            xs_ref[slot, c * MOE_CHUNK:(c + 1) * MOE_CHUNK, :] = _dot(pi_ref[...], x).astype(xs_ref.dtype)

        _start_chunk_pieces(ntot_ref[i], c, start)

    @pl.when(i == nb - 1)
    def _():
        zb_ref[...] = jnp.zeros_like(zb_ref)

        def tail(e, n):
            def piece(p, c):
                zero_copy(pl.multiple_of(toff_ref[e] + p * MOE_PIECE, MOE_PIECE)).start()
                return c

            lax.fori_loop(0, tnp_ref[e], piece, 0)
            return n + tnp_ref[e]

        nz = lax.fori_loop(0, N_EXPERTS, tail, 0)
        nblk = xg_ref.shape[0] // MOE_RB

        def unused(b, c):
            zero_block(b).start()
            return c

        lax.fori_loop(nused_ref[0], nblk, unused, 0)
        _repeat(nz, lambda: zero_copy(0).wait())
        _repeat(nblk - nused_ref[0], lambda: zero_block(0).wait())
        wait_slot(slot, ntot_ref[i])

    @pl.when((i == nb - 1) & (i >= 1))
    def _():
        wait_slot(1 - slot, ntot_ref[i - 1])


def _moe_expert_body(be_ref, act_ref, x_ref, wg_ref, wu_ref, wd_ref, o_ref):
    del be_ref
    i = pl.program_id(0)

    @pl.when(act_ref[i] != 0)
    def _():
        x = x_ref[...]
        hid = _silu(_dot(x, wg_ref[...].astype(MX))) * _dot(x, wu_ref[...].astype(MX))
        o_ref[...] = _dot(hid.astype(MX), wd_ref[...].astype(MX)).astype(o_ref.dtype)

    @pl.when(act_ref[i] == 0)
    def _():
        o_ref[...] = jnp.zeros_like(o_ref)


def _moe_combine_body(pexp_ref, gdst_ref, ntot_ref, yg_ref, dst_ref, wt_ref, x_ref,
                      sg_ref, su_ref, sd_ref, h_ref, gpost_ref, gate_ref, o_ref,
                      ys_ref, pw_ref, acc_ref, sem):
    i, nb = pl.program_id(0), pl.num_programs(0)
    slot = i % 2

    def piece_copy(s, lo, go):
        return pltpu.make_async_copy(yg_ref.at[pl.ds(go, MOE_PIECE)],
                                     ys_ref.at[s, pl.ds(lo, MOE_PIECE)], sem.at[s])

    def fetch(blk, s):
        n = ntot_ref[blk]

        def start(q):
            piece_copy(s, pl.multiple_of(q * MOE_PIECE, MOE_PIECE),
                       pl.multiple_of(gdst_ref[blk, q], MOE_PIECE)).start()

        for c in range(MOE_CAP // MOE_CHUNK):
            _start_chunk_pieces(n, c, start)

        def clear(q, carry):
            ys_ref[s, pl.ds(pl.multiple_of(q * MOE_PIECE, MOE_PIECE), MOE_PIECE), :] = jnp.zeros(
                (MOE_PIECE, ys_ref.shape[2]), ys_ref.dtype)
            return carry

        lax.fori_loop(n, (n + MOE_PPC - 1) // MOE_PPC * MOE_PPC, clear, 0)

    @pl.when(i == 0)
    def _():
        fetch(0, 0)

    @pl.when(i + 1 < nb)
    def _():
        fetch(i + 1, 1 - slot)

    x = x_ref[...]
    hid = _silu(_dot(x, sg_ref[...].astype(MX))) * _dot(x, su_ref[...].astype(MX))
    acc_ref[...] = _dot(hid.astype(MX), sd_ref[...].astype(MX))
    def wait_chunk():
        pltpu.make_async_copy(yg_ref.at[pl.ds(0, MOE_CHUNK)], ys_ref.at[slot, pl.ds(0, MOE_CHUNK)],
                              sem.at[slot]).wait()

    _wait_pieces(ntot_ref[i], wait_chunk, lambda: piece_copy(slot, 0, 0).wait())
    for c in range(MOE_CAP // MOE_CHUNK):
        @pl.when(c * MOE_PPC < ntot_ref[i])
        def _():
            _onehot_rows(pexp_ref, dst_ref, i, c, lambda e: wt_ref[pl.ds(e, 1), :], pw_ref)
            acc_ref[...] += lax.dot_general(
                pw_ref[...], ys_ref[slot, c * MOE_CHUNK:(c + 1) * MOE_CHUNK, :],
                (((0,), (0,)), ((), ())), preferred_element_type=F32)
    o_ref[...] = _residual(h_ref[...], acc_ref[...], gpost_ref[...], gate_ref[...])


def _moe(x, dst_t, w_t, cpad, layer, wg, wu, wd, sg, su, sd, h, gpost, mod4, seg_fn):
    t, d = x.shape
    nb = t // MOE_TB
    plan = _dispatch_plan(cpad, t)
    ng = plan['ng']
    tables = (plan['pexp'], plan['gdst'], plan['ntot'])
    etab = pl.BlockSpec((None, N_EXPERTS, MOE_TB), lambda i, *_: (i, 0, 0))
    xg = pl.pallas_call(
        _moe_sort_body,
        name="moe_sort",
        out_shape=jax.ShapeDtypeStruct((ng, d), MX),
        grid_spec=pltpu.PrefetchScalarGridSpec(
            num_scalar_prefetch=6, grid=(nb,),
            in_specs=[pl.BlockSpec((MOE_TB, d), lambda i, *_: (i, 0)), etab],
            out_specs=pl.BlockSpec(memory_space=pl.ANY),
            scratch_shapes=[pltpu.VMEM((2, MOE_CAP, d), MX), pltpu.VMEM((MOE_CHUNK, MOE_TB), MX),
                            pltpu.VMEM((MOE_RB, d), MX),
                            pltpu.SemaphoreType.DMA((2,)), pltpu.SemaphoreType.DMA((2,))]),
        compiler_params=_cparams(("arbitrary",)),
    )(*tables, plan['toff'], plan['tnp'], plan['nused'], x, dst_t)
    yg = pl.pallas_call(
        _moe_expert_body,
        name="moe_experts",
        out_shape=jax.ShapeDtypeStruct((ng, d), MX),
        grid_spec=pltpu.PrefetchScalarGridSpec(
            num_scalar_prefetch=2, grid=(ng // MOE_RB,),
            in_specs=[pl.BlockSpec((MOE_RB, d), lambda i, be, act: (jnp.where(act[i] != 0, i, 0), 0)),
                      pl.BlockSpec((None, None, d, D_EXPERT), lambda i, be, act: (layer, be[i], 0, 0)),
                      pl.BlockSpec((None, None, d, D_EXPERT), lambda i, be, act: (layer, be[i], 0, 0)),
                      pl.BlockSpec((None, None, D_EXPERT, d), lambda i, be, act: (layer, be[i], 0, 0))],
            out_specs=pl.BlockSpec((MOE_RB, d), lambda i, be, act: (i, 0))),
        compiler_params=_cparams(("arbitrary",)),
    )(plan['blk_e'], plan['active'], xg, wg, wu, wd)
    row = lambda n: pl.BlockSpec((MOE_TB, n), lambda i, *_: (i, 0))
    const = lambda arr: pl.BlockSpec(arr.shape, lambda i, *_: (0,) * arr.ndim)
    return pl.pallas_call(
        _moe_combine_body,
        name="moe_combine",
        out_shape=jax.ShapeDtypeStruct((t, d), F32),
        grid_spec=pltpu.PrefetchScalarGridSpec(
            num_scalar_prefetch=3, grid=(nb,),
            in_specs=[pl.BlockSpec(memory_space=pl.ANY), etab, etab, row(d),
                      const(sg), const(su), const(sd), row(d), const(gpost),
                      pl.BlockSpec((None, None, 1, d), lambda i, *_: (seg_fn(i), 5, 0, 0))],
            out_specs=row(d),
            scratch_shapes=[pltpu.VMEM((2, MOE_CAP, d), MX), pltpu.VMEM((MOE_CHUNK, MOE_TB), MX),
                            pltpu.VMEM((MOE_TB, d), F32), pltpu.SemaphoreType.DMA((2,))]),
        compiler_params=_cparams(("arbitrary",)),
    )(*tables, yg, dst_t, w_t, x, sg, su, sd, h, gpost, mod4)


def _rope_tables(seq_len, n_ident):
    rows = seq_len // GRID_W
    row = jnp.repeat(jnp.arange(rows, dtype=F32), GRID_W)
    col = jnp.tile(jnp.arange(GRID_W, dtype=F32), rows)
    axis_dim = 32
    inv_freq = ROPE_THETA ** (-jnp.arange(0, axis_dim, 2, dtype=F32) / axis_dim)
    ang = jnp.concatenate([row[:, None] * inv_freq, col[:, None] * inv_freq], axis=-1)
    cos, sin = jnp.cos(ang), jnp.sin(ang)
    cos64 = jnp.concatenate([cos, cos], axis=-1)
    sin64 = jnp.concatenate([-sin, sin], axis=-1)
    cos64 = jnp.concatenate([jnp.ones((n_ident, 64), F32), cos64], axis=0)
    sin64 = jnp.concatenate([jnp.zeros((n_ident, 64), F32), sin64], axis=0)
    return cos64, sin64


def _even_weights(w_in):
    d = w_in.shape[0]
    cuts = np.cumsum([GLA_QK, GLA_QK, GLA_VW, GLA_VW, GLA_RANK, GLA_RANK, SWA_W,
                      SWA_KV_HEADS * SWA_HD])
    q, k, v, g, af, ab, sq, sk, sv = jnp.split(w_in, [int(c) for c in cuts], axis=1)
    a = jnp.concatenate([af, ab, jnp.zeros((d, 128 - 2 * GLA_RANK), w_in.dtype)], axis=1)

    def rep(wkv):
        return jnp.tile(wkv.reshape(d, SWA_KV_HEADS, 1, SWA_HD), (1, 1, SWA_G, 1)).reshape(d, SWA_W)

    return jnp.concatenate([q, k, v, g, a, sq, rep(sk), rep(sv)], axis=1).astype(MX)


def _layer_tail(h, l, mod4, seg_fn, p):
    x, dst_t, w_t, cpad = _ffn_pre(h, p['g_ffn_pre'][l][None], mod4, p['w_router'][l], p['b_router'][l], seg_fn)
    return _moe(x, dst_t, w_t, cpad[:, :, 0], l, p['w_exp_gate'], p['w_exp_up'], p['w_exp_down'],
                p['w_sh_gate'][l], p['w_sh_up'][l], p['w_sh_down'][l],
                h, p['g_ffn_post'][l][None], mod4, seg_fn)


def kernel(x, c, ctx, c_ctx, w_mod, b_mod, g_mix_pre, g_mix_post, g_ffn_pre, g_ffn_post, w_in_e, gla_wa2, gla_ba, gla_norm, swa_sink, w_out_e, w_in_o, mla_q_norm, mla_kv_norm, w_uq, w_ukv, w_out_o, w_router, b_router, w_exp_gate, w_exp_up, w_exp_down, w_sh_gate, w_sh_up, w_sh_down):
    p = dict(g_ffn_pre=g_ffn_pre, g_ffn_post=g_ffn_post, w_router=w_router, b_router=b_router,
             w_exp_gate=w_exp_gate, w_exp_up=w_exp_up, w_exp_down=w_exp_down,
             w_sh_gate=w_sh_gate, w_sh_up=w_sh_up, w_sh_down=w_sh_down)
    batch, seq_len, d = x.shape
    ctx_len = ctx.shape[1]
    depth = w_mod.shape[0]
    assert depth == 2 and batch * ctx_len == ROW_TILE and seq_len % ROW_TILE == 0
    lat_blocks = seq_len // ROW_TILE

    cvec = jnp.concatenate([c, c_ctx[None], jnp.zeros((8 - batch - 1, d), F32)], axis=0)
    mod = _mod_vectors(cvec, w_mod, b_mod)
    h = jnp.concatenate([ctx.reshape(batch * ctx_len, d), x.reshape(batch * seq_len, d)], axis=0)

    def seg_fns(tile):
        cb, lb = batch * ctx_len // tile, seq_len // tile
        return (lambda i: jnp.where(i < cb, batch, (i - cb) // lb)), (lambda i: i // lb)

    seg_all, seg_lat = seg_fns(ROW_TILE)
    moe_all, moe_lat = seg_fns(MOE_TB)
    pos_all = lambda i: jnp.where(i == 0, 0, 1 + (i - 1) % lat_blocks)
    cos64, sin64 = _rope_tables(seq_len, ROW_TILE)

    mod4 = mod[0].reshape(8, 6, 1, d)
    q, k, v, gg, a, sq, skr, svr = _even_in(
        h, g_mix_pre[0][None], mod4, _even_weights(w_in_e[0]),
        jnp.tile(cos64, (1, 2)), jnp.tile(sin64, (1, 2)), seg_all, pos_all)
    wa = jnp.zeros((2, 128, GLA_QK), F32)
    wa = wa.at[0, :GLA_RANK].set(gla_wa2[0, 0]).at[1, GLA_RANK:2 * GLA_RANK].set(gla_wa2[0, 1])
    o_f, o_b = _gla(q, k, v, a, wa, gla_ba[0][:, None, :], batch, ctx_len, seq_len)
    a_swa = _swa(sq, skr, svr, swa_sink[0], batch, ctx_len, seq_len)
    w_out = w_out_e[0].astype(MX)
    h = _even_out(h, o_f, o_b, gg, a_swa, gla_norm[0][None], w_out[:GLA_VW], w_out[GLA_VW:],
                  g_mix_post[0][None], mod4, seg_all)
    h = _layer_tail(h, 0, mod4, moe_all, p)

    mod4 = mod[1].reshape(8, 6, 1, d)
    ones = jnp.ones_like(cos64)
    cos_h = jnp.concatenate([cos64, ones], axis=1)
    sin_h = jnp.concatenate([sin64, 0 * ones], axis=1)
    w_in = jnp.concatenate([w_in_o[0], jnp.zeros((d, 128 - MLA_ROPE), F32)], axis=1).astype(MX)
    cq, ckv, kr = _odd_in(h, g_mix_pre[1][None], mod4, w_in, mla_q_norm[0][None],
                          mla_kv_norm[0][None], cos_h, sin_h, seg_all, pos_all)
    w_q = w_uq[0].reshape(MLA_Q_RANK, MLA_HEADS, MLA_NOPE + MLA_ROPE)
    w_q = jnp.pad(w_q, ((0, 0), (0, 0), (0, MLA_QK_PAD - MLA_NOPE - MLA_ROPE)))
    w_q = w_q.reshape(MLA_Q_RANK, MLA_HEADS * MLA_QK_PAD).astype(MX)
    n_lat = batch * seq_len
    qh = _q_up(cq, 1, n_lat, w_q, cos_h, sin_h, pos_all)
    w_kv = w_ukv[0].astype(MX)
    k_lat, v_lat = _kv_up(ckv, kr, 1, n_lat, ROW_TILE, w_kv)
    k_ctx, v_ctx = _kv_up(ckv, kr, 0, batch * ctx_len, ctx_len, w_kv)
    o = _mla_attn(qh, k_ctx, v_ctx, k_lat, v_lat, batch, ctx_len, seq_len)
    hl = _odd_out(h, 1, o, w_out_o[0].astype(MX), g_mix_post[1][None], mod4, seg_lat)
    hl = _layer_tail(hl, 1, mod4, moe_lat, p)
    return hl.reshape(batch, seq_len, d)
```

```python
import functools
import math

import jax
import jax.numpy as jnp
import numpy as np
from jax import lax
from jax.experimental import pallas as pl
from jax.experimental.pallas import tpu as pltpu

F32 = jnp.float32
MX = jnp.bfloat16
HI = lax.Precision.HIGHEST

GRID_W = 64
ROPE_THETA = 10000.0
RMS_EPS = 1e-6

GLA_HEADS, GLA_DK, GLA_DV, GLA_RANK, GLA_TAU = 4, 64, 128, 16, 16.0
GLA_QK = GLA_HEADS * GLA_DK
GLA_VW = GLA_HEADS * GLA_DV
GLA_CHUNK = 128

SWA_HEADS, SWA_KV_HEADS, SWA_HD, WINDOW = 8, 2, 64, 128
SWA_G = SWA_HEADS // SWA_KV_HEADS
SWA_W = SWA_HEADS * SWA_HD

MLA_HEADS, MLA_Q_RANK, MLA_KV_RANK = 8, 512, 256
MLA_NOPE, MLA_ROPE, MLA_V = 128, 64, 128
MLA_QK_PAD = 256
MLA_VW = 256
MLA_SCALE = (MLA_NOPE + MLA_ROPE) ** -0.5

N_EXPERTS, TOP_K, N_GROUPS, TOPK_GROUPS = 64, 8, 8, 4
GROUP_SIZE = N_EXPERTS // N_GROUPS
D_EXPERT = 256
ROUTED_SCALE = 2.5

ROW_TILE = 512
VMEM_LIMIT = 56 * 1024 * 1024


def _cparams(sem):
    return pltpu.CompilerParams(dimension_semantics=sem, vmem_limit_bytes=VMEM_LIMIT)


def _sigmoid(x):
    return 1.0 / (1.0 + jnp.exp(-x))


def _silu(x):
    return x * _sigmoid(x)


def _rms(x, g):
    ms = jnp.mean(x * x, axis=-1, keepdims=True)
    return x * lax.rsqrt(ms + RMS_EPS) * g


def _modulate(h, g, shift, scale):
    return _rms(h, g) * (1.0 + scale) + shift


def _dot(a, b):
    return jnp.dot(a, b, preferred_element_type=F32)


def _dot_t(a, b):
    return lax.dot_general(a, b, (((1,), (1,)), ((), ())), preferred_element_type=F32)


def _rope(x, cos, sin_signed):
    n = x.shape[-1]
    lane = lax.broadcasted_iota(jnp.int32, x.shape, 1)
    first = (lane % 64) < 32
    partner = jnp.where(first, pltpu.roll(x, n - 32, 1), pltpu.roll(x, 32, 1))
    return x * cos + partner * sin_signed


def _mod_body(c_ref, w_ref, b_ref, o_ref):
    s = _silu(c_ref[...])
    o_ref[0] = jnp.dot(s, w_ref[0], precision=HI, preferred_element_type=F32) + b_ref[0]


def _mod_vectors(cvec, w_mod, b_mod):
    depth, d, n = w_mod.shape
    tn = 1536
    return pl.pallas_call(
        _mod_body,
        name="mod_vectors",
        out_shape=jax.ShapeDtypeStruct((depth, 8, n), F32),
        grid=(depth, n // tn),
        in_specs=[pl.BlockSpec((8, d), lambda l, j: (0, 0)),
                  pl.BlockSpec((1, d, tn), lambda l, j: (l, 0, j)),
                  pl.BlockSpec((1, 1, tn), lambda l, j: (l, 0, j))],
        out_specs=pl.BlockSpec((1, 8, tn), lambda l, j: (l, 0, j)),
        compiler_params=_cparams(("arbitrary", "arbitrary")),
    )(cvec, w_mod, b_mod.reshape(depth, 1, n))


def _mod_spec(seg_fn, which, d):
    return pl.BlockSpec((None, None, 1, d), lambda i, *_: (seg_fn(i), which, 0, 0))


def _full_spec(arr):
    nd = arr.ndim
    return pl.BlockSpec(arr.shape, lambda *_: (0,) * nd)


_EV_Q, _EV_K, _EV_V, _EV_G, _EV_A, _EV_SQ, _EV_SK, _EV_SV, _EV_END = (
    0, 256, 512, 1024, 1536, 1664, 2176, 2688, 3200)


def _even_in_body(h_ref, g_ref, sh_ref, sc_ref, w_ref, cos_ref, sin_ref,
                  q_ref, k_ref, v_ref, gg_ref, a_ref, sq_ref, sk_ref, sv_ref):
    u = _modulate(h_ref[...], g_ref[...], sh_ref[...], sc_ref[...]).astype(MX)
    q_ref[...] = _dot(u, w_ref[:, _EV_Q:_EV_K]) * (GLA_DK ** -0.5)
    k_ref[...] = _dot(u, w_ref[:, _EV_K:_EV_V])
    v_ref[...] = _dot(u, w_ref[:, _EV_V:_EV_G])
    gg_ref[...] = _dot(u, w_ref[:, _EV_G:_EV_A])
    a_ref[...] = _dot(u, w_ref[:, _EV_A:_EV_SQ])
    cos = jnp.concatenate([cos_ref[...]] * (SWA_W // 128), axis=-1)
    sin = jnp.concatenate([sin_ref[...]] * (SWA_W // 128), axis=-1)
    sq = _rope(_dot(u, w_ref[:, _EV_SQ:_EV_SK]), cos, sin)
    sq_ref[...] = (sq * (SWA_HD ** -0.5)).astype(sq_ref.dtype)
    sk_ref[...] = _rope(_dot(u, w_ref[:, _EV_SK:_EV_SV]), cos, sin).astype(sk_ref.dtype)
    sv_ref[...] = _dot(u, w_ref[:, _EV_SV:_EV_END]).astype(sv_ref.dtype)


def _even_in(h, g_pre, mod4, w_cat, cos_t, sin_t, seg_fn, pos_fn):
    t, d = h.shape
    tm = ROW_TILE
    row = lambda n: pl.BlockSpec((tm, n), lambda i: (i, 0))
    outs = [(GLA_QK, F32), (GLA_QK, F32), (GLA_VW, F32), (GLA_VW, F32), (128, F32),
            (SWA_W, MX), (SWA_W, MX), (SWA_W, MX)]
    return pl.pallas_call(
        _even_in_body,
        name="even_in",
        out_shape=[jax.ShapeDtypeStruct((t, n), dt) for n, dt in outs],
        grid=(t // tm,),
        in_specs=[row(d), _full_spec(g_pre), _mod_spec(seg_fn, 0, d), _mod_spec(seg_fn, 1, d),
                  _full_spec(w_cat),
                  pl.BlockSpec((tm, 128), lambda i: (pos_fn(i), 0)),
                  pl.BlockSpec((tm, 128), lambda i: (pos_fn(i), 0))],
        out_specs=[row(n) for n, _ in outs],
        compiler_params=_cparams(("parallel",)),
    )(h, g_pre, mod4, mod4, w_cat, cos_t, sin_t)


def _gla_consts(c, reverse):
    nlev = int(round(math.log2(c)))
    idx = np.arange(c)
    i, m = idx[:, None], idx[None, :]
    sizes = [c >> (l + 1) for l in range(nlev)]
    a = (m <= i).astype(np.float32)
    masks = [np.eye(c, dtype=bool)]
    for s in sizes:
        bi, bj = i // s, m // s
        masks.append((bi % 2 == 1) & (bj == bi - 1))
    msk = np.stack(masks).astype(np.float32)
    if reverse:
        a = a[::-1, ::-1]
        msk = msk[:, ::-1, ::-1]
    return (np.ascontiguousarray(a), np.ascontiguousarray(np.tile(msk, (1, 1, GLA_HEADS))))


def _split3(x):
    hi = x.astype(MX)
    r1 = x - hi.astype(F32)
    mid = r1.astype(MX)
    lo = (r1 - mid.astype(F32)).astype(MX)
    return hi, mid, lo


def _gla_level_exponents(la, cum, s, reverse):
    c, w = la.shape
    nblk = c // s
    if s >= 8:
        zero = jnp.zeros((1, w), F32)
        if not reverse:
            qrows = [zero if b == 0 else cum[b * s - 1:b * s] for b in range(nblk)]
            krows = [cum[(b + 1) * s - 1:(b + 1) * s] for b in range(nblk)]
        else:
            qrows = [zero if b == nblk - 1 else cum[(b + 1) * s:(b + 1) * s + 1] for b in range(nblk)]
            krows = [cum[b * s:b * s + 1] for b in range(nblk)]
        spread = lambda rows: jnp.concatenate([jnp.broadcast_to(r, (s, w)) for r in rows], axis=0)
        return cum - spread(qrows), spread(krows) - cum
    pos = lax.broadcasted_iota(jnp.int32, (c, 1), 0) & (s - 1)
    qe, ke = la, jnp.zeros_like(la)
    for d in range(1, s):
        before = jnp.where(pos >= d, pltpu.roll(la, d, 0), 0.0)
        after = jnp.where(pos + d <= s - 1, pltpu.roll(la, c - d, 0), 0.0)
        qe, ke = (qe + after, ke + before) if reverse else (qe + before, ke + after)
    return qe, ke


def _gla_direction(q, k, v, a, wa, ba, amat, lmask, hmask, vmask, bdmask, st_ref, d, reverse):
    c = q.shape[0]
    nlev = int(round(math.log2(c)))
    x = jnp.dot(a, wa, precision=HI, preferred_element_type=F32) + ba
    la = (jnp.minimum(x, 0.0) - jnp.log1p(jnp.exp(-jnp.abs(x)))) * (1.0 / GLA_TAU)
    hi, mid, lo = _split3(la)
    cum = _dot(amat, hi) + _dot(amat, mid) + _dot(amat, lo)
    last = cum[0:1] if reverse else cum[c - 1:c]
    qd = (q * jnp.exp(cum)).astype(MX)
    kd = (k * jnp.exp(last - cum)).astype(MX)
    st = st_ref[d]
    o = _dot_t(qd, st.astype(MX))
    scat = jnp.zeros((c, GLA_HEADS * c), F32)
    for lev in range(nlev + 1):
        if lev == 0:
            ql, kl = q, k
        else:
            qe, ke = _gla_level_exponents(la, cum, c >> lev, reverse)
            ql, kl = q * jnp.exp(qe), k * jnp.exp(ke)
        kst = jnp.concatenate([kl.astype(MX)] * GLA_HEADS, axis=0) * hmask
        scat = scat + _dot_t(ql.astype(MX), kst) * lmask[lev]
    vbd = jnp.concatenate([v.astype(MX)] * GLA_HEADS, axis=0) * vmask
    o = o + _dot(scat.astype(MX), vbd)
    upd = lax.dot_general(v.astype(MX), kd, (((0,), (0,)), ((), ())), preferred_element_type=F32)
    st_ref[d] = st * jnp.exp(last) + upd * bdmask
    return o


def _gla_body(qf_ref, kf_ref, vf_ref, af_ref, qb_ref, kb_ref, vb_ref, ab_ref,
              wa_ref, ba_ref, amf_ref, lmf_ref, amb_ref, lmb_ref, hm_ref, vm_ref, bd_ref,
              of_ref, ob_ref, st_ref):
    @pl.when(pl.program_id(1) == 0)
    def _():
        st_ref[...] = jnp.zeros_like(st_ref)

    hm, vm, bd = hm_ref[...], vm_ref[...], bd_ref[...]
    of_ref[...] = _gla_direction(qf_ref[...], kf_ref[...], vf_ref[...], af_ref[...],
                                 wa_ref[0], ba_ref[0], amf_ref[...], lmf_ref, hm, vm, bd,
                                 st_ref, 0, False)
    ob_ref[...] = _gla_direction(qb_ref[...], kb_ref[...], vb_ref[...], ab_ref[...],
                                 wa_ref[1], ba_ref[1], amb_ref[...], lmb_ref, hm, vm, bd,
                                 st_ref, 1, True)


def _gla(q, k, v, a, wa, ba, batch, ctx_len, seq_len):
    t = q.shape[0]
    c = GLA_CHUNK
    nc, nl = ctx_len // c, seq_len // c
    amf, lmf = _gla_consts(c, False)
    amb, lmb = _gla_consts(c, True)
    r = np.arange(GLA_HEADS * c)[:, None] // c
    hm = (r == np.arange(GLA_QK)[None, :] // GLA_DK).astype(np.float32)
    vm = (r == np.arange(GLA_VW)[None, :] // GLA_DV).astype(np.float32)
    bd = (np.arange(GLA_VW)[:, None] // GLA_DV
          == np.arange(GLA_QK)[None, :] // GLA_DK).astype(np.float32)

    def fwd(b, s):
        return jnp.where(s < nc, nc * b + s, batch * nc + nl * b + (s - nc))

    def bwd(b, s):
        return jnp.where(s < nc, nc * b + (nc - 1 - s), batch * nc + nl * b + (nl - 1 - (s - nc)))

    def chunk(n, fn):
        return pl.BlockSpec((c, n), lambda b, s: (fn(b, s), 0))

    consts = [jnp.asarray(amf, MX), jnp.asarray(lmf), jnp.asarray(amb, MX), jnp.asarray(lmb),
              jnp.asarray(hm, MX), jnp.asarray(vm, MX), jnp.asarray(bd)]
    ins = [q, k, v, a, q, k, v, a, wa, ba] + consts
    specs = ([chunk(GLA_QK, fwd), chunk(GLA_QK, fwd), chunk(GLA_VW, fwd), chunk(128, fwd),
              chunk(GLA_QK, bwd), chunk(GLA_QK, bwd), chunk(GLA_VW, bwd), chunk(128, bwd)]
             + [_full_spec(z) for z in ins[8:]])
    return pl.pallas_call(
        _gla_body,
        name="gla_scan",
        out_shape=[jax.ShapeDtypeStruct((t, GLA_VW), F32)] * 2,
        grid=(batch, nc + nl),
        in_specs=specs,
        out_specs=[chunk(GLA_VW, fwd), chunk(GLA_VW, bwd)],
        scratch_shapes=[pltpu.VMEM((2, GLA_VW, GLA_QK), F32)],
        compiler_params=_cparams(("parallel", "arbitrary")),
    )(*ins)


def _swa_heads(q, kcat, vcat, valid, sink_ref, h):
    lane_head = lax.broadcasted_iota(jnp.int32, (1, SWA_G * SWA_HD), 1) // SWA_HD
    acc = jnp.zeros((q.shape[0], SWA_G * SWA_HD), F32)
    for g in range(SWA_G):
        hm = lane_head == g
        s = _dot_t(jnp.where(hm, q, jnp.zeros_like(q)), kcat)
        if valid is not None:
            s = jnp.where(valid, s, -jnp.inf)
        sk = sink_ref[h * SWA_G + g]
        m = jnp.maximum(jnp.max(s, axis=-1, keepdims=True), sk)
        p = jnp.exp(s - m)
        den = jnp.sum(p, axis=-1, keepdims=True) + jnp.exp(sk - m)
        og = _dot(p.astype(MX), jnp.where(hm, vcat, jnp.zeros_like(vcat)))
        acc = acc + og * (1.0 / den)
    return acc


def _swa_latent_body(sink_ref, q_ref, kc_ref, vc_ref, kp_ref, k0_ref, kn_ref,
                     vp_ref, v0_ref, vn_ref, o_ref, *, nblk):
    n = pl.program_id(1)
    qt, blk = q_ref.shape[0], kp_ref.shape[0]
    nctx = kc_ref.shape[0]
    shape = (qt, nctx + qt + 2 * blk)
    qi = lax.broadcasted_iota(jnp.int32, shape, 0)
    col = lax.broadcasted_iota(jnp.int32, shape, 1)
    si = col - nctx
    kpos = n * qt - blk + si
    valid = (col < nctx) | ((jnp.abs(si - blk - qi) <= WINDOW) & (kpos >= 0) & (kpos < nblk * blk))
    w = SWA_G * SWA_HD
    for h in range(SWA_KV_HEADS):
        cols = slice(h * w, (h + 1) * w)
        kcat = jnp.concatenate([kc_ref[:, cols], kp_ref[:, cols], k0_ref[:, cols], kn_ref[:, cols]], axis=0)
        vcat = jnp.concatenate([vc_ref[:, cols], vp_ref[:, cols], v0_ref[:, cols], vn_ref[:, cols]], axis=0)
        o_ref[:, cols] = _swa_heads(q_ref[:, cols], kcat, vcat, valid, sink_ref, h).astype(o_ref.dtype)


def _swa_ctx_body(sink_ref, q_ref, kc_ref, vc_ref, o_ref):
    h = pl.program_id(1)
    o_ref[...] = _swa_heads(q_ref[...], kc_ref[...], vc_ref[...], None, sink_ref, h).astype(o_ref.dtype)


def _swa(sq, skr, svr, sink, batch, ctx_len, seq_len):
    t = sq.shape[0]
    blk = WINDOW
    w = SWA_G * SWA_HD
    nb = seq_len // blk
    cb = ctx_len // blk
    lat0 = batch * cb
    smem = pl.BlockSpec(memory_space=pltpu.SMEM)
    ctx_kv = pl.BlockSpec((ctx_len, w), lambda b, h, n: (b, h))

    qb = 2
    qt, nq = qb * blk, nb // qb

    def edge(off):
        return pl.BlockSpec((blk, SWA_W), lambda b, n: (lat0 + b * nb + jnp.clip(n * qb + off, 0, nb - 1), 0))

    mid = pl.BlockSpec((qt, SWA_W), lambda b, n: (lat0 // qb + b * nq + n, 0))
    ctx_all = pl.BlockSpec((ctx_len, SWA_W), lambda b, n: (b, 0))
    lat = pl.pallas_call(
        functools.partial(_swa_latent_body, nblk=nb),
        name="swa_latent",
        out_shape=jax.ShapeDtypeStruct((batch * seq_len, SWA_W), MX),
        grid=(batch, nq),
        in_specs=[smem, mid, ctx_all, ctx_all, edge(-1), mid, edge(qb), edge(-1), mid, edge(qb)],
        out_specs=pl.BlockSpec((qt, SWA_W), lambda b, n: (b * nq + n, 0)),
        compiler_params=_cparams(("parallel", "arbitrary")),
    )(sink, sq, skr, svr, skr, skr, skr, svr, svr, svr)
    ctx = pl.pallas_call(
        _swa_ctx_body,
        name="swa_ctx",
        out_shape=jax.ShapeDtypeStruct((batch * ctx_len, SWA_W), MX),
        grid=(batch, SWA_KV_HEADS, cb),
        in_specs=[smem, pl.BlockSpec((blk, w), lambda b, h, n: (b * cb + n, h)), ctx_kv, ctx_kv],
        out_specs=pl.BlockSpec((blk, w), lambda b, h, n: (b * cb + n, h)),
        compiler_params=_cparams(("parallel", "parallel", "arbitrary")),
    )(sink, sq, skr, svr)
    return jnp.concatenate([ctx, lat], axis=0)


def _residual(h, y, gpost, gate):
    return h + gate * _rms(y, gpost)


def _even_out_body(h_ref, of_ref, ob_ref, gg_ref, a_ref, gn_ref, w1_ref, w2_ref, gpost_ref,
                   gate_ref, o_ref):
    o = of_ref[...] + ob_ref[...]
    gn = gn_ref[...]
    parts = [_rms(o[:, j * GLA_DV:(j + 1) * GLA_DV], gn) for j in range(GLA_HEADS)]
    gl = jnp.concatenate(parts, axis=-1) * _silu(gg_ref[...])
    y = _dot(gl.astype(MX), w1_ref[...]) + _dot(a_ref[...], w2_ref[...])
    o_ref[...] = _residual(h_ref[...], y, gpost_ref[...], gate_ref[...])


def _even_out(h, o_f, o_b, gg, a_swa, gn, w1, w2, gpost, mod4, seg_fn):
    t, d = h.shape
    tm = ROW_TILE
    row = lambda n: pl.BlockSpec((tm, n), lambda i: (i, 0))
    return pl.pallas_call(
        _even_out_body,
        name="even_out",
        out_shape=jax.ShapeDtypeStruct((t, d), F32),
        grid=(t // tm,),
        in_specs=[row(d), row(GLA_VW), row(GLA_VW), row(GLA_VW), row(SWA_W), _full_spec(gn),
                  _full_spec(w1), _full_spec(w2), _full_spec(gpost), _mod_spec(seg_fn, 2, d)],
        out_specs=row(d),
        compiler_params=_cparams(("parallel",)),
    )(h, o_f, o_b, gg, a_swa, gn, w1, w2, gpost, mod4)


def _odd_out_body(h_ref, o_ref_in, w_ref, gpost_ref, gate_ref, o_ref):
    y = _dot(o_ref_in[...], w_ref[...])
    o_ref[...] = _residual(h_ref[...], y, gpost_ref[...], gate_ref[...])


def _odd_out(h, h_off, o, w, gpost, mod4, seg_fn):
    t, d = o.shape[0], h.shape[1]
    tm = ROW_TILE
    row = lambda n: pl.BlockSpec((tm, n), lambda i: (i, 0))
    return pl.pallas_call(
        _odd_out_body,
        name="odd_out",
        out_shape=jax.ShapeDtypeStruct((t, d), F32),
        grid=(t // tm,),
        in_specs=[pl.BlockSpec((tm, d), lambda i: (i + h_off, 0)), row(o.shape[1]),
                  _full_spec(w), _full_spec(gpost), _mod_spec(seg_fn, 2, d)],
        out_specs=row(d),
        compiler_params=_cparams(("parallel",)),
    )(h, o, w, gpost, mod4)


def _odd_in_body(h_ref, g_ref, sh_ref, sc_ref, w_ref, qn_ref, kvn_ref, cos_ref, sin_ref,
                 cq_ref, ckv_ref, kr_ref):
    u = _modulate(h_ref[...], g_ref[...], sh_ref[...], sc_ref[...]).astype(MX)
    cq_ref[...] = _rms(_dot(u, w_ref[:, 0:MLA_Q_RANK]), qn_ref[...]).astype(cq_ref.dtype)
    c1 = MLA_Q_RANK + MLA_KV_RANK
    ckv_ref[...] = _rms(_dot(u, w_ref[:, MLA_Q_RANK:c1]), kvn_ref[...]).astype(ckv_ref.dtype)
    kr = _dot(u, w_ref[:, c1:c1 + 128])
    kr_ref[...] = _rope(kr, cos_ref[...], sin_ref[...]).astype(kr_ref.dtype)


def _odd_in(h, g_pre, mod4, w_cat, qn, kvn, cos_t, sin_t, seg_fn, pos_fn):
    t, d = h.shape
    tm = ROW_TILE
    row = lambda n: pl.BlockSpec((tm, n), lambda i: (i, 0))
    tab = pl.BlockSpec((tm, 128), lambda i: (pos_fn(i), 0))
    return pl.pallas_call(
        _odd_in_body,
        name="odd_in",
        out_shape=[jax.ShapeDtypeStruct((t, MLA_Q_RANK), MX),
                   jax.ShapeDtypeStruct((t, MLA_KV_RANK), MX),
                   jax.ShapeDtypeStruct((t, 128), MX)],
        grid=(t // tm,),
        in_specs=[row(d), _full_spec(g_pre), _mod_spec(seg_fn, 0, d), _mod_spec(seg_fn, 1, d),
                  _full_spec(w_cat), _full_spec(qn), _full_spec(kvn), tab, tab],
        out_specs=[row(MLA_Q_RANK), row(MLA_KV_RANK), row(128)],
        compiler_params=_cparams(("parallel",)),
    )(h, g_pre, mod4, mod4, w_cat, qn, kvn, cos_t, sin_t)


def _q_up_body(cq_ref, w_ref, cos_ref, sin_ref, o_ref):
    cq, cos, sin = cq_ref[...], cos_ref[...], sin_ref[...]
    scale = MLA_SCALE * math.log2(math.e)
    for h in range(MLA_HEADS):
        c0 = h * MLA_QK_PAD
        z = _dot(cq, w_ref[:, c0:c0 + MLA_QK_PAD])
        o_ref[:, c0:c0 + MLA_NOPE] = (z[:, :MLA_NOPE] * scale).astype(o_ref.dtype)
        zr = _rope(z[:, MLA_NOPE:], cos, sin)
        o_ref[:, c0 + MLA_NOPE:c0 + MLA_QK_PAD] = (zr * scale).astype(o_ref.dtype)


def _q_up(cq, row_off, nrows, w_pad, cos_t, sin_t, pos_fn):
    tm = ROW_TILE
    width = MLA_HEADS * MLA_QK_PAD
    return pl.pallas_call(
        _q_up_body,
        name="mla_q_up",
        out_shape=jax.ShapeDtypeStruct((nrows, width), MX),
        grid=(nrows // tm,),
        in_specs=[pl.BlockSpec((tm, MLA_Q_RANK), lambda i: (i + row_off, 0)),
                  _full_spec(w_pad),
                  pl.BlockSpec((tm, 128), lambda i: (pos_fn(i + row_off), 0)),
                  pl.BlockSpec((tm, 128), lambda i: (pos_fn(i + row_off), 0))],
        out_specs=pl.BlockSpec((tm, width), lambda i: (i, 0)),
        compiler_params=_cparams(("parallel",)),
    )(cq, w_pad, cos_t, sin_t)


def _kv_up_body(ckv_ref, kr_ref, w_ref, k_ref, v_ref):
    ckv, kr = ckv_ref[...], kr_ref[...]
    for h in range(MLA_HEADS):
        c0 = h * (MLA_NOPE + MLA_V)
        z = _dot(ckv, w_ref[:, c0:c0 + MLA_NOPE + MLA_V])
        k0 = h * MLA_QK_PAD
        k_ref[:, k0:k0 + MLA_NOPE] = z[:, :MLA_NOPE].astype(k_ref.dtype)
        k_ref[:, k0 + MLA_NOPE:k0 + MLA_QK_PAD] = kr
        v0 = h * MLA_VW
        v_ref[:, v0:v0 + MLA_V] = z[:, MLA_NOPE:].astype(v_ref.dtype)
        v_ref[:, v0 + MLA_V:v0 + MLA_VW] = jnp.ones((z.shape[0], MLA_VW - MLA_V), v_ref.dtype)


def _kv_up(ckv, kr, row_off, nrows, tm, w_ukv):
    return pl.pallas_call(
        _kv_up_body,
        name="mla_kv_up",
        out_shape=[jax.ShapeDtypeStruct((nrows, MLA_HEADS * MLA_QK_PAD), MX),
                   jax.ShapeDtypeStruct((nrows, MLA_HEADS * MLA_VW), MX)],
        grid=(nrows // tm,),
        in_specs=[pl.BlockSpec((tm, MLA_KV_RANK), lambda i: (i + row_off, 0)),
                  pl.BlockSpec((tm, 128), lambda i: (i + row_off, 0)),
                  _full_spec(w_ukv)],
        out_specs=[pl.BlockSpec((tm, MLA_HEADS * MLA_QK_PAD), lambda i: (i, 0)),
                   pl.BlockSpec((tm, MLA_HEADS * MLA_VW), lambda i: (i, 0))],
        compiler_params=_cparams(("parallel",)),
    )(ckv, kr, w_ukv)


def _mla_attn_body(q_ref, kc_ref, vc_ref, k_ref, v_ref, o_ref, *, tk, unroll):
    q = q_ref[...]
    s = _dot_t(q, kc_ref[...])
    m = jnp.max(s, axis=-1, keepdims=True)
    acc = _dot(jnp.exp2((s - m).astype(MX)), vc_ref[...])

    def step(j, carry):
        m, acc = carry
        start = pl.multiple_of(j * tk, tk)
        s = _dot_t(q, k_ref[pl.ds(start, tk), :])
        mn = jnp.maximum(m, jnp.max(s, axis=-1, keepdims=True))
        p = jnp.exp2((s - mn).astype(MX))
        acc = jnp.exp2(m - mn) * acc + _dot(p, v_ref[pl.ds(start, tk), :])
        return mn, acc

    m, acc = lax.fori_loop(0, k_ref.shape[0] // tk, step, (m, acc), unroll=unroll)
    o_ref[...] = (acc[:, :MLA_V] * (1.0 / acc[:, MLA_V:MLA_V + 1])).astype(o_ref.dtype)


def _mla_attn(q, k_ctx, v_ctx, k_lat, v_lat, batch, ctx_len, seq_len, tq=1024, tk=1024, unroll=8):
    nq = seq_len // tq
    return pl.pallas_call(
        functools.partial(_mla_attn_body, tk=tk, unroll=unroll),
        name="mla_attn",
        out_shape=jax.ShapeDtypeStruct((batch * seq_len, MLA_HEADS * MLA_V), MX),
        grid=(batch, MLA_HEADS, nq),
        in_specs=[pl.BlockSpec((tq, MLA_QK_PAD), lambda b, h, i: (b * nq + i, h)),
                  pl.BlockSpec((ctx_len, MLA_QK_PAD), lambda b, h, i: (b, h)),
                  pl.BlockSpec((ctx_len, MLA_VW), lambda b, h, i: (b, h)),
                  pl.BlockSpec((seq_len, MLA_QK_PAD), lambda b, h, i: (b, h)),
                  pl.BlockSpec((seq_len, MLA_VW), lambda b, h, i: (b, h))],
        out_specs=pl.BlockSpec((tq, MLA_V), lambda b, h, i: (b * nq + i, h)),
        compiler_params=_cparams(("parallel", "parallel", "arbitrary")),
    )(q, k_ctx, v_ctx, k_lat, v_lat)


def _route(scores, sel):
    e, t = sel.shape
    neg = -jnp.inf
    x3 = sel.reshape(N_GROUPS, GROUP_SIZE, t)
    pos = lax.broadcasted_iota(jnp.int32, x3.shape, 1)
    m1 = jnp.max(x3, axis=1, keepdims=True)
    i1 = jnp.min(jnp.where(x3 == m1, pos, GROUP_SIZE), axis=1, keepdims=True)
    m2 = jnp.max(jnp.where(pos == i1, neg, x3), axis=1, keepdims=True)
    gs = m1 + m2
    gid = lax.broadcasted_iota(jnp.int32, gs.shape, 0)
    beaten = jnp.zeros(gs.shape, jnp.int32)
    for g in range(N_GROUPS):
        other = gs[g:g + 1]
        beaten = beaten + jnp.where((other > gs) | ((other == gs) & (g < gid)), 1, 0)
    cur = jnp.where(beaten < TOPK_GROUPS, x3, neg).reshape(e, t)
    row = lax.broadcasted_iota(jnp.int32, (e, t), 0)
    chosen = jnp.zeros((e, t), F32)
    for _ in range(TOP_K):
        m = jnp.max(cur, axis=0, keepdims=True)
        i = jnp.min(jnp.where(cur == m, row, e), axis=0, keepdims=True)
        hit = row == i
        chosen = jnp.where(hit, 1.0, chosen)
        cur = jnp.where(hit, neg, cur)
    w = chosen * scores
    return chosen, w / jnp.sum(w, axis=0, keepdims=True) * ROUTED_SCALE


def _ffn_pre_body(h_ref, g_ref, sh_ref, sc_ref, wr_ref, br_ref, us_ref, ls_ref,
                  v_ref, dst_ref, wt_ref, cpad_ref):
    vl = _modulate(h_ref[...], g_ref[...], sh_ref[...], sc_ref[...])
    v_ref[...] = vl.astype(v_ref.dtype)
    logits = lax.dot_general(wr_ref[...], vl, (((1,), (1,)), ((), ())), precision=HI,
                             preferred_element_type=F32)
    scores = _sigmoid(logits)
    chosen, w = _route(scores, scores + br_ref[...])
    rank = _dot(chosen.astype(MX), us_ref[...])
    cnt = jnp.sum(chosen, axis=1, keepdims=True)
    cpad = jnp.ceil(cnt * (1.0 / MOE_PIECE)) * MOE_PIECE
    cpad_b = jnp.broadcast_to(cpad, (cpad.shape[0], 128))
    loff = _dot(ls_ref[...], cpad_b.astype(MX))
    dst_ref[...] = jnp.where(chosen > 0.0, loff[:, 0:1] + rank, -1.0).astype(jnp.int32)
    wt_ref[...] = w
    cpad_ref[...] = cpad_b.astype(jnp.int32)


def _ffn_pre(h, g_pre, mod4, w_router, b_router, seg_fn):
    t, d = h.shape
    tm = MOE_TB
    nb = t // tm
    row = lambda n: pl.BlockSpec((tm, n), lambda i: (i, 0))
    blk = lambda n: pl.BlockSpec((None, N_EXPERTS, n), lambda i: (i, 0, 0))
    us = jnp.asarray(np.triu(np.ones((tm, tm), np.float32), 1), MX)
    ls = jnp.asarray(np.tril(np.ones((N_EXPERTS, N_EXPERTS), np.float32), -1), MX)
    wr_t = w_router.T
    br_t = b_router.reshape(N_EXPERTS, 1)
    return pl.pallas_call(
        _ffn_pre_body,
        name="ffn_pre_router",
        out_shape=[jax.ShapeDtypeStruct((t, d), MX),
                   jax.ShapeDtypeStruct((nb, N_EXPERTS, tm), jnp.int32),
                   jax.ShapeDtypeStruct((nb, N_EXPERTS, tm), F32),
                   jax.ShapeDtypeStruct((nb, N_EXPERTS, 128), jnp.int32)],
        grid=(nb,),
        in_specs=[row(d), _full_spec(g_pre), _mod_spec(seg_fn, 3, d), _mod_spec(seg_fn, 4, d),
                  _full_spec(wr_t), _full_spec(br_t), _full_spec(us), _full_spec(ls)],
        out_specs=[row(d), blk(tm), blk(tm), blk(128)],
        compiler_params=_cparams(("parallel",)),
    )(h, g_pre, mod4, mod4, wr_t, br_t, us, ls)


MOE_TB = 512
MOE_PIECE = 16
MOE_CAP = TOP_K * MOE_TB + N_EXPERTS * MOE_PIECE
MOE_NPIECE = MOE_CAP // MOE_PIECE
MOE_CHUNK = 1024
MOE_PPC = MOE_CHUNK // MOE_PIECE
MOE_RB = 512


def _dispatch_plan(cpad, t):
    nb = cpad.shape[0]
    loff = jnp.cumsum(cpad, axis=1) - cpad
    tot = jnp.sum(cpad, axis=0)
    reg = (tot + MOE_RB - 1) // MOE_RB * MOE_RB
    gend = jnp.cumsum(reg)
    gbase = gend - reg
    goff = gbase[None] + jnp.cumsum(cpad, axis=0) - cpad
    prow = jnp.arange(MOE_NPIECE, dtype=jnp.int32) * MOE_PIECE
    pexp = jnp.sum((prow[None, :, None] >= (loff + cpad)[:, None, :]).astype(jnp.int32), axis=2)
    pexp = jnp.minimum(pexp, N_EXPERTS - 1)
    own = pexp[:, :, None] == jnp.arange(N_EXPERTS, dtype=jnp.int32)[None, None, :]
    gdst = jnp.sum(jnp.where(own, (goff - loff)[:, None, :], 0), axis=2) + prow[None]
    ng = t * TOP_K + nb * N_EXPERTS * (MOE_PIECE - 1) + N_EXPERTS * (MOE_RB - 1)
    ng = -(-ng // (2 * MOE_RB)) * (2 * MOE_RB)
    start = jnp.arange(ng // MOE_RB, dtype=jnp.int32) * MOE_RB
    blk_e = jnp.sum((start[:, None] >= gend[None, :]).astype(jnp.int32), axis=1)
    blk_e = jnp.minimum(blk_e, N_EXPERTS - 1)
    active = (start < (gbase + tot)[blk_e]).astype(jnp.int32)
    i32 = lambda z: z.astype(jnp.int32)
    return dict(pexp=i32(pexp), gdst=i32(gdst), ntot=i32(jnp.sum(cpad, axis=1) // MOE_PIECE),
                toff=i32(gbase + tot), tnp=i32((reg - tot) // MOE_PIECE),
                nused=i32(gend[-1:] // MOE_RB), blk_e=blk_e, active=active, ng=ng)


def _onehot_rows(pexp_ref, dst_ref, blk, c, val_of, out_ref):
    rows = lax.broadcasted_iota(jnp.int32, (MOE_PIECE, MOE_TB), 0)
    for p in range(MOE_PPC):
        e = pexp_ref[blk, c * MOE_PPC + p]
        hit = dst_ref[pl.ds(e, 1), :] == rows + (c * MOE_CHUNK + p * MOE_PIECE)
        out_ref[p * MOE_PIECE:(p + 1) * MOE_PIECE, :] = jnp.where(hit, val_of(e), 0.0).astype(out_ref.dtype)


def _repeat(n, fn):
    def body(_, c):
        fn()
        return c

    lax.fori_loop(0, n, body, 0)


def _start_chunk_pieces(n, c, start_fn):
    @pl.when(n >= (c + 1) * MOE_PPC)
    def _():
        for p in range(MOE_PPC):
            start_fn(c * MOE_PPC + p)

    @pl.when((n > c * MOE_PPC) & (n < (c + 1) * MOE_PPC))
    def _():
        def body(q, carry):
            start_fn(q)
            return carry

        lax.fori_loop(c * MOE_PPC, n, body, 0)


def _wait_pieces(n, wait_chunk, wait_piece):
    _repeat(n // MOE_PPC, wait_chunk)
    _repeat(n % MOE_PPC, wait_piece)


def _moe_sort_body(pexp_ref, gdst_ref, ntot_ref, toff_ref, tnp_ref, nused_ref, x_ref, dst_ref,
                   xg_ref, xs_ref, pi_ref, zb_ref, sem, zsem):
    i, nb = pl.program_id(0), pl.num_programs(0)
    slot = i % 2

    def piece_copy(s, lo, go):
        return pltpu.make_async_copy(xs_ref.at[s, pl.ds(lo, MOE_PIECE)],
                                     xg_ref.at[pl.ds(go, MOE_PIECE)], sem.at[s])

    def zero_copy(go):
        return pltpu.make_async_copy(zb_ref.at[pl.ds(0, MOE_PIECE)], xg_ref.at[pl.ds(go, MOE_PIECE)],
                                     zsem.at[0])

    def zero_block(b):
        return pltpu.make_async_copy(zb_ref, xg_ref.at[pl.ds(pl.multiple_of(b * MOE_RB, MOE_RB), MOE_RB)],
                                     zsem.at[1])

    def wait_slot(s, n):
        def chunk():
            pltpu.make_async_copy(xs_ref.at[s, pl.ds(0, MOE_CHUNK)], xg_ref.at[pl.ds(0, MOE_CHUNK)],
                                  sem.at[s]).wait()

        _wait_pieces(n, chunk, lambda: piece_copy(s, 0, 0).wait())

    def start(q):
        piece_copy(slot, pl.multiple_of(q * MOE_PIECE, MOE_PIECE),
                   pl.multiple_of(gdst_ref[i, q], MOE_PIECE)).start()

    @pl.when(i >= 2)
    def _():
        wait_slot(slot, ntot_ref[i - 2])

    x = x_ref[...]
    for c in range(MOE_CAP // MOE_CHUNK):
        _onehot_rows(pexp_ref, dst_ref, i, c, lambda e: 1.0, pi_ref.at[c])
        xs_ref[slot, c * MOE_CHUNK:(c + 1) * MOE_CHUNK, :] = _dot(pi_ref[c], x).astype(xs_ref.dtype)
    for c in range(MOE_CAP // MOE_CHUNK):
        _start_chunk_pieces(ntot_ref[i], c, start)

    @pl.when(i == nb - 1)
    def _():
        zb_ref[...] = jnp.zeros_like(zb_ref)

        def tail(e, n):
            def piece(p, c):
                zero_copy(pl.multiple_of(toff_ref[e] + p * MOE_PIECE, MOE_PIECE)).start()
                return c

            lax.fori_loop(0, tnp_ref[e], piece, 0)
            return n + tnp_ref[e]

        nz = lax.fori_loop(0, N_EXPERTS, tail, 0)
        nblk = xg_ref.shape[0] // MOE_RB

        def unused(b, c):
            zero_block(b).start()
            return c

        lax.fori_loop(nused_ref[0], nblk, unused, 0)
        _repeat(nz, lambda: zero_copy(0).wait())
        _repeat(nblk - nused_ref[0], lambda: zero_block(0).wait())
        wait_slot(slot, ntot_ref[i])

    @pl.when((i == nb - 1) & (i >= 1))
    def _():
        wait_slot(1 - slot, ntot_ref[i - 1])


def _moe_expert_body(be_ref, act_ref, x_ref, *refs):
    del be_ref
    w_refs, o_ref = refs[:-1], refs[-1]
    i = pl.program_id(0)
    live = (act_ref[2 * i] != 0) | (act_ref[2 * i + 1] != 0)

    @pl.when(live)
    def _():
        for r in range(2):
            wg_ref, wu_ref, wd_ref = w_refs[3 * r:3 * r + 3]
            rows = slice(r * MOE_RB, (r + 1) * MOE_RB)
            x = x_ref[rows, :]
            hid = _silu(_dot(x, wg_ref[...].astype(MX))) * _dot(x, wu_ref[...].astype(MX))
            o_ref[rows, :] = _dot(hid.astype(MX), wd_ref[...].astype(MX)).astype(o_ref.dtype)

    @pl.when(jnp.logical_not(live))
    def _():
        o_ref[...] = jnp.zeros_like(o_ref)


def _moe_combine_body(pexp_ref, gdst_ref, ntot_ref, yg_ref, dst_ref, wt_ref, x_ref,
                      sg_ref, su_ref, sd_ref, h_ref, gpost_ref, gate_ref, o_ref,
                      ys_ref, pw_ref, sem):
    i, nb = pl.program_id(0), pl.num_programs(0)
    slot = i % 2

    def piece_copy(s, lo, go):
        return pltpu.make_async_copy(yg_ref.at[pl.ds(go, MOE_PIECE)],
                                     ys_ref.at[s, pl.ds(lo, MOE_PIECE)], sem.at[s])

    def fetch(blk, s):
        n = ntot_ref[blk]

        def start(q):
            piece_copy(s, pl.multiple_of(q * MOE_PIECE, MOE_PIECE),
                       pl.multiple_of(gdst_ref[blk, q], MOE_PIECE)).start()

        for c in range(MOE_CAP // MOE_CHUNK):
            _start_chunk_pieces(n, c, start)

        def clear(q, carry):
            ys_ref[s, pl.ds(pl.multiple_of(q * MOE_PIECE, MOE_PIECE), MOE_PIECE), :] = jnp.zeros(
                (MOE_PIECE, ys_ref.shape[2]), ys_ref.dtype)
            return carry

        lax.fori_loop(n, MOE_NPIECE, clear, 0)

    @pl.when(i == 0)
    def _():
        fetch(0, 0)

    @pl.when(i + 1 < nb)
    def _():
        fetch(i + 1, 1 - slot)

    x = x_ref[...]
    hid = _silu(_dot(x, sg_ref[...].astype(MX))) * _dot(x, su_ref[...].astype(MX))
    acc = _dot(hid.astype(MX), sd_ref[...].astype(MX))

    def wait_chunk():
        pltpu.make_async_copy(yg_ref.at[pl.ds(0, MOE_CHUNK)], ys_ref.at[slot, pl.ds(0, MOE_CHUNK)],
                              sem.at[slot]).wait()

    _wait_pieces(ntot_ref[i], wait_chunk, lambda: piece_copy(slot, 0, 0).wait())
    for c in range(MOE_CAP // MOE_CHUNK):
        _onehot_rows(pexp_ref, dst_ref, i, c, lambda e: wt_ref[pl.ds(e, 1), :], pw_ref.at[c])
        acc = acc + lax.dot_general(
            pw_ref[c], ys_ref[slot, c * MOE_CHUNK:(c + 1) * MOE_CHUNK, :],
            (((0,), (0,)), ((), ())), preferred_element_type=F32)
    o_ref[...] = _residual(h_ref[...], acc, gpost_ref[...], gate_ref[...])


def _moe(x, dst_t, w_t, cpad, layer, wg, wu, wd, sg, su, sd, h, gpost, mod4, seg_fn):
    t, d = x.shape
    nb = t // MOE_TB
    plan = _dispatch_plan(cpad, t)
    ng = plan['ng']
    tables = (plan['pexp'], plan['gdst'], plan['ntot'])
    etab = pl.BlockSpec((None, N_EXPERTS, MOE_TB), lambda i, *_: (i, 0, 0))
    xg = pl.pallas_call(
        _moe_sort_body,
        name="moe_sort",
        out_shape=jax.ShapeDtypeStruct((ng, d), MX),
        grid_spec=pltpu.PrefetchScalarGridSpec(
            num_scalar_prefetch=6, grid=(nb,),
            in_specs=[pl.BlockSpec((MOE_TB, d), lambda i, *_: (i, 0)), etab],
            out_specs=pl.BlockSpec(memory_space=pl.ANY),
            scratch_shapes=[pltpu.VMEM((2, MOE_CAP, d), MX),
                            pltpu.VMEM((MOE_CAP // MOE_CHUNK, MOE_CHUNK, MOE_TB), MX),
                            pltpu.VMEM((MOE_RB, d), MX),
                            pltpu.SemaphoreType.DMA((2,)), pltpu.SemaphoreType.DMA((2,))]),
        compiler_params=_cparams(("arbitrary",)),
    )(*tables, plan['toff'], plan['tnp'], plan['nused'], x, dst_t)
    yg = pl.pallas_call(
        _moe_expert_body,
        name="moe_experts",
        out_shape=jax.ShapeDtypeStruct((ng, d), MX),
        grid_spec=pltpu.PrefetchScalarGridSpec(
            num_scalar_prefetch=2, grid=(ng // (2 * MOE_RB),),
            in_specs=[pl.BlockSpec((2 * MOE_RB, d), lambda i, be, act: (i, 0))] + [
                pl.BlockSpec((None, None) + w.shape[2:],
                             functools.partial(lambda i, be, act, r: (layer, be[2 * i + r], 0, 0), r=r))
                for r in range(2) for w in (wg, wu, wd)],
            out_specs=pl.BlockSpec((2 * MOE_RB, d), lambda i, be, act: (i, 0))),
        compiler_params=_cparams(("arbitrary",)),
    )(plan['blk_e'], plan['active'], xg, wg, wu, wd, wg, wu, wd)
    row = lambda n: pl.BlockSpec((MOE_TB, n), lambda i, *_: (i, 0))
    const = lambda arr: pl.BlockSpec(arr.shape, lambda i, *_: (0,) * arr.ndim)
    return pl.pallas_call(
        _moe_combine_body,
        name="moe_combine",
        out_shape=jax.ShapeDtypeStruct((t, d), F32),
        grid_spec=pltpu.PrefetchScalarGridSpec(
            num_scalar_prefetch=3, grid=(nb,),
            in_specs=[pl.BlockSpec(memory_space=pl.ANY), etab, etab, row(d),
                      const(sg), const(su), const(sd), row(d), const(gpost),
                      pl.BlockSpec((None, None, 1, d), lambda i, *_: (seg_fn(i), 5, 0, 0))],
            out_specs=row(d),
            scratch_shapes=[pltpu.VMEM((2, MOE_CAP, d), MX),
                            pltpu.VMEM((MOE_CAP // MOE_CHUNK, MOE_CHUNK, MOE_TB), MX),
                            pltpu.SemaphoreType.DMA((2,))]),
        compiler_params=_cparams(("arbitrary",)),
    )(*tables, yg, dst_t, w_t, x, sg, su, sd, h, gpost, mod4)


def _rope_tables(seq_len, n_ident):
    rows = seq_len // GRID_W
    row = jnp.repeat(jnp.arange(rows, dtype=F32), GRID_W)
    col = jnp.tile(jnp.arange(GRID_W, dtype=F32), rows)
    axis_dim = 32
    inv_freq = ROPE_THETA ** (-jnp.arange(0, axis_dim, 2, dtype=F32) / axis_dim)
    ang = jnp.concatenate([row[:, None] * inv_freq, col[:, None] * inv_freq], axis=-1)
    cos, sin = jnp.cos(ang), jnp.sin(ang)
    cos64 = jnp.concatenate([cos, cos], axis=-1)
    sin64 = jnp.concatenate([-sin, sin], axis=-1)
    cos64 = jnp.concatenate([jnp.ones((n_ident, 64), F32), cos64], axis=0)
    sin64 = jnp.concatenate([jnp.zeros((n_ident, 64), F32), sin64], axis=0)
    return cos64, sin64


def _even_weights(w_in):
    d = w_in.shape[0]
    cuts = np.cumsum([GLA_QK, GLA_QK, GLA_VW, GLA_VW, GLA_RANK, GLA_RANK, SWA_W,
                      SWA_KV_HEADS * SWA_HD])
    q, k, v, g, af, ab, sq, sk, sv = jnp.split(w_in, [int(c) for c in cuts], axis=1)
    a = jnp.concatenate([af, ab, jnp.zeros((d, 128 - 2 * GLA_RANK), w_in.dtype)], axis=1)

    def rep(wkv):
        return jnp.tile(wkv.reshape(d, SWA_KV_HEADS, 1, SWA_HD), (1, 1, SWA_G, 1)).reshape(d, SWA_W)

    return jnp.concatenate([q, k, v, g, a, sq, rep(sk), rep(sv)], axis=1).astype(MX)


def _layer_tail(h, l, mod4, seg_fn, p):
    x, dst_t, w_t, cpad = _ffn_pre(h, p['g_ffn_pre'][l][None], mod4, p['w_router'][l], p['b_router'][l], seg_fn)
    return _moe(x, dst_t, w_t, cpad[:, :, 0], l, p['w_exp_gate'], p['w_exp_up'], p['w_exp_down'],
                p['w_sh_gate'][l], p['w_sh_up'][l], p['w_sh_down'][l],
                h, p['g_ffn_post'][l][None], mod4, seg_fn)


def kernel(x, c, ctx, c_ctx, w_mod, b_mod, g_mix_pre, g_mix_post, g_ffn_pre, g_ffn_post, w_in_e, gla_wa2, gla_ba, gla_norm, swa_sink, w_out_e, w_in_o, mla_q_norm, mla_kv_norm, w_uq, w_ukv, w_out_o, w_router, b_router, w_exp_gate, w_exp_up, w_exp_down, w_sh_gate, w_sh_up, w_sh_down):
    p = dict(g_ffn_pre=g_ffn_pre, g_ffn_post=g_ffn_post, w_router=w_router, b_router=b_router,
             w_exp_gate=w_exp_gate, w_exp_up=w_exp_up, w_exp_down=w_exp_down,
             w_sh_gate=w_sh_gate, w_sh_up=w_sh_up, w_sh_down=w_sh_down)
    batch, seq_len, d = x.shape
    ctx_len = ctx.shape[1]
    depth = w_mod.shape[0]
    assert depth == 2 and batch * ctx_len == ROW_TILE and seq_len % ROW_TILE == 0
    lat_blocks = seq_len // ROW_TILE

    cvec = jnp.concatenate([c, c_ctx[None], jnp.zeros((8 - batch - 1, d), F32)], axis=0)
    mod = _mod_vectors(cvec, w_mod, b_mod)
    h = jnp.concatenate([ctx.reshape(batch * ctx_len, d), x.reshape(batch * seq_len, d)], axis=0)

    def seg_fns(tile):
        cb, lb = batch * ctx_len // tile, seq_len // tile
        return (lambda i: jnp.where(i < cb, batch, (i - cb) // lb)), (lambda i: i // lb)

    seg_all, seg_lat = seg_fns(ROW_TILE)
    moe_all, moe_lat = seg_fns(MOE_TB)
    pos_all = lambda i: jnp.where(i == 0, 0, 1 + (i - 1) % lat_blocks)
    cos64, sin64 = _rope_tables(seq_len, ROW_TILE)

    mod4 = mod[0].reshape(8, 6, 1, d)
    q, k, v, gg, a, sq, skr, svr = _even_in(
        h, g_mix_pre[0][None], mod4, _even_weights(w_in_e[0]),
        jnp.tile(cos64, (1, 2)), jnp.tile(sin64, (1, 2)), seg_all, pos_all)
    wa = jnp.zeros((2, 128, GLA_QK), F32)
    wa = wa.at[0, :GLA_RANK].set(gla_wa2[0, 0]).at[1, GLA_RANK:2 * GLA_RANK].set(gla_wa2[0, 1])
    o_f, o_b = _gla(q, k, v, a, wa, gla_ba[0][:, None, :], batch, ctx_len, seq_len)
    a_swa = _swa(sq, skr, svr, swa_sink[0], batch, ctx_len, seq_len)
    w_out = w_out_e[0].astype(MX)
    h = _even_out(h, o_f, o_b, gg, a_swa, gla_norm[0][None], w_out[:GLA_VW], w_out[GLA_VW:],
                  g_mix_post[0][None], mod4, seg_all)
    h = _layer_tail(h, 0, mod4, moe_all, p)

    mod4 = mod[1].reshape(8, 6, 1, d)
    ones = jnp.ones_like(cos64)
    cos_h = jnp.concatenate([cos64, ones], axis=1)
    sin_h = jnp.concatenate([sin64, 0 * ones], axis=1)
    w_in = jnp.concatenate([w_in_o[0], jnp.zeros((d, 128 - MLA_ROPE), F32)], axis=1).astype(MX)
    cq, ckv, kr = _odd_in(h, g_mix_pre[1][None], mod4, w_in, mla_q_norm[0][None],
                          mla_kv_norm[0][None], cos_h, sin_h, seg_all, pos_all)
    w_q = w_uq[0].reshape(MLA_Q_RANK, MLA_HEADS, MLA_NOPE + MLA_ROPE)
    w_q = jnp.pad(w_q, ((0, 0), (0, 0), (0, MLA_QK_PAD - MLA_NOPE - MLA_ROPE)))
    w_q = w_q.reshape(MLA_Q_RANK, MLA_HEADS * MLA_QK_PAD).astype(MX)
    n_lat = batch * seq_len
    qh = _q_up(cq, 1, n_lat, w_q, cos_h, sin_h, pos_all)
    w_kv = w_ukv[0].astype(MX)
    k_lat, v_lat = _kv_up(ckv, kr, 1, n_lat, ROW_TILE, w_kv)
    k_ctx, v_ctx = _kv_up(ckv, kr, 0, batch * ctx_len, ctx_len, w_kv)
    o = _mla_attn(qh, k_ctx, v_ctx, k_lat, v_lat, batch, ctx_len, seq_len)
    hl = _odd_out(h, 1, o, w_out_o[0].astype(MX), g_mix_post[1][None], mod4, seg_lat)
    hl = _layer_tail(hl, 1, mod4, moe_lat, p)
    return hl.reshape(batch, seq_len, d)
```

```python
import functools
import math

import jax
import jax.numpy as jnp
import numpy as np
from jax import lax
from jax.experimental import pallas as pl
from jax.experimental.pallas import tpu as pltpu

F32 = jnp.float32
MX = jnp.bfloat16
HI = lax.Precision.HIGHEST

GRID_W = 64
ROPE_THETA = 10000.0
RMS_EPS = 1e-6

GLA_HEADS, GLA_DK, GLA_DV, GLA_RANK, GLA_TAU = 4, 64, 128, 16, 16.0
GLA_QK = GLA_HEADS * GLA_DK
GLA_VW = GLA_HEADS * GLA_DV
GLA_CHUNK = 128

SWA_HEADS, SWA_KV_HEADS, SWA_HD, WINDOW = 8, 2, 64, 128
SWA_G = SWA_HEADS // SWA_KV_HEADS
SWA_W = SWA_HEADS * SWA_HD

MLA_HEADS, MLA_Q_RANK, MLA_KV_RANK = 8, 512, 256
MLA_NOPE, MLA_ROPE, MLA_V = 128, 64, 128
MLA_QK_PAD = 256
MLA_VW = 256
MLA_SCALE = (MLA_NOPE + MLA_ROPE) ** -0.5

N_EXPERTS, TOP_K, N_GROUPS, TOPK_GROUPS = 64, 8, 8, 4
GROUP_SIZE = N_EXPERTS // N_GROUPS
D_EXPERT = 256
ROUTED_SCALE = 2.5

ROW_TILE = 512
VMEM_LIMIT = 56 * 1024 * 1024


def _cparams(sem):
    return pltpu.CompilerParams(dimension_semantics=sem, vmem_limit_bytes=VMEM_LIMIT)


def _sigmoid(x):
    return 1.0 / (1.0 + jnp.exp(-x))


def _silu(x):
    return x * _sigmoid(x)


def _rms(x, g):
    ms = jnp.mean(x * x, axis=-1, keepdims=True)
    return x * lax.rsqrt(ms + RMS_EPS) * g


def _modulate(h, g, shift, scale):
    return _rms(h, g) * (1.0 + scale) + shift


def _dot(a, b):
    return jnp.dot(a, b, preferred_element_type=F32)


def _dot_t(a, b):
    return lax.dot_general(a, b, (((1,), (1,)), ((), ())), preferred_element_type=F32)


def _rope(x, cos, sin_signed):
    n = x.shape[-1]
    lane = lax.broadcasted_iota(jnp.int32, x.shape, 1)
    first = (lane % 64) < 32
    partner = jnp.where(first, pltpu.roll(x, n - 32, 1), pltpu.roll(x, 32, 1))
    return x * cos + partner * sin_signed


def _mod_body(c_ref, w_ref, b_ref, o_ref):
    s = _silu(c_ref[...])
    o_ref[0] = jnp.dot(s, w_ref[0], precision=HI, preferred_element_type=F32) + b_ref[0]


def _mod_vectors(cvec, w_mod, b_mod):
    depth, d, n = w_mod.shape
    tn = 1536
    return pl.pallas_call(
        _mod_body,
        name="mod_vectors",
        out_shape=jax.ShapeDtypeStruct((depth, 8, n), F32),
        grid=(depth, n // tn),
        in_specs=[pl.BlockSpec((8, d), lambda l, j: (0, 0)),
                  pl.BlockSpec((1, d, tn), lambda l, j: (l, 0, j)),
                  pl.BlockSpec((1, 1, tn), lambda l, j: (l, 0, j))],
        out_specs=pl.BlockSpec((1, 8, tn), lambda l, j: (l, 0, j)),
        compiler_params=_cparams(("arbitrary", "arbitrary")),
    )(cvec, w_mod, b_mod.reshape(depth, 1, n))


def _mod_spec(seg_fn, which, d):
    return pl.BlockSpec((None, None, 1, d), lambda i, *_: (seg_fn(i), which, 0, 0))


def _full_spec(arr):
    nd = arr.ndim
    return pl.BlockSpec(arr.shape, lambda *_: (0,) * nd)


_EV_Q, _EV_K, _EV_V, _EV_G, _EV_A, _EV_SQ, _EV_SK, _EV_SV, _EV_END = (
    0, 256, 512, 1024, 1536, 1664, 2176, 2688, 3200)


def _even_in_body(h_ref, g_ref, sh_ref, sc_ref, w_ref, cos_ref, sin_ref,
                  q_ref, k_ref, v_ref, gg_ref, a_ref, sq_ref, sk_ref, sv_ref):
    u = _modulate(h_ref[...], g_ref[...], sh_ref[...], sc_ref[...]).astype(MX)
    q_ref[...] = _dot(u, w_ref[:, _EV_Q:_EV_K]) * (GLA_DK ** -0.5)
    k_ref[...] = _dot(u, w_ref[:, _EV_K:_EV_V])
    v_ref[...] = _dot(u, w_ref[:, _EV_V:_EV_G])
    gg_ref[...] = _dot(u, w_ref[:, _EV_G:_EV_A])
    a_ref[...] = _dot(u, w_ref[:, _EV_A:_EV_SQ])
    cos = jnp.concatenate([cos_ref[...]] * (SWA_W // 128), axis=-1)
    sin = jnp.concatenate([sin_ref[...]] * (SWA_W // 128), axis=-1)
    sq = _rope(_dot(u, w_ref[:, _EV_SQ:_EV_SK]), cos, sin)
    sq_ref[...] = (sq * (SWA_HD ** -0.5)).astype(sq_ref.dtype)
    sk_ref[...] = _rope(_dot(u, w_ref[:, _EV_SK:_EV_SV]), cos, sin).astype(sk_ref.dtype)
    sv_ref[...] = _dot(u, w_ref[:, _EV_SV:_EV_END]).astype(sv_ref.dtype)


def _even_in(h, g_pre, mod4, w_cat, cos_t, sin_t, seg_fn, pos_fn):
    t, d = h.shape
    tm = ROW_TILE
    row = lambda n: pl.BlockSpec((tm, n), lambda i: (i, 0))
    outs = [(GLA_QK, F32), (GLA_QK, F32), (GLA_VW, F32), (GLA_VW, F32), (128, F32),
            (SWA_W, MX), (SWA_W, MX), (SWA_W, MX)]
    return pl.pallas_call(
        _even_in_body,
        name="even_in",
        out_shape=[jax.ShapeDtypeStruct((t, n), dt) for n, dt in outs],
        grid=(t // tm,),
        in_specs=[row(d), _full_spec(g_pre), _mod_spec(seg_fn, 0, d), _mod_spec(seg_fn, 1, d),
                  _full_spec(w_cat),
                  pl.BlockSpec((tm, 128), lambda i: (pos_fn(i), 0)),
                  pl.BlockSpec((tm, 128), lambda i: (pos_fn(i), 0))],
        out_specs=[row(n) for n, _ in outs],
        compiler_params=_cparams(("parallel",)),
    )(h, g_pre, mod4, mod4, w_cat, cos_t, sin_t)


def _gla_consts(c, reverse):
    nlev = int(round(math.log2(c)))
    idx = np.arange(c)
    i, m = idx[:, None], idx[None, :]
    sizes = [c >> (l + 1) for l in range(nlev)]
    a = (m <= i).astype(np.float32)
    masks = [np.eye(c, dtype=bool)]
    for s in sizes:
        bi, bj = i // s, m // s
        masks.append((bi % 2 == 1) & (bj == bi - 1))
    msk = np.stack(masks).astype(np.float32)
    if reverse:
        a = a[::-1, ::-1]
        msk = msk[:, ::-1, ::-1]
    return (np.ascontiguousarray(a), np.ascontiguousarray(np.tile(msk, (1, 1, GLA_HEADS))))


def _split3(x):
    hi = x.astype(MX)
    r1 = x - hi.astype(F32)
    mid = r1.astype(MX)
    lo = (r1 - mid.astype(F32)).astype(MX)
    return hi, mid, lo


def _gla_level_exponents(la, cum, s, reverse):
    c, w = la.shape
    nblk = c // s
    if s >= 8:
        zero = jnp.zeros((1, w), F32)
        if not reverse:
            qrows = [zero if b == 0 else cum[b * s - 1:b * s] for b in range(nblk)]
            krows = [cum[(b + 1) * s - 1:(b + 1) * s] for b in range(nblk)]
        else:
            qrows = [zero if b == nblk - 1 else cum[(b + 1) * s:(b + 1) * s + 1] for b in range(nblk)]
            krows = [cum[b * s:b * s + 1] for b in range(nblk)]
        spread = lambda rows: jnp.concatenate([jnp.broadcast_to(r, (s, w)) for r in rows], axis=0)
        return cum - spread(qrows), spread(krows) - cum
    pos = lax.broadcasted_iota(jnp.int32, (c, 1), 0) & (s - 1)
    qe, ke = la, jnp.zeros_like(la)
    for d in range(1, s):
        before = jnp.where(pos >= d, pltpu.roll(la, d, 0), 0.0)
        after = jnp.where(pos + d <= s - 1, pltpu.roll(la, c - d, 0), 0.0)
        qe, ke = (qe + after, ke + before) if reverse else (qe + before, ke + after)
    return qe, ke


def _gla_direction(q, k, v, a, wa, ba, amat, lmask, hmask, vmask, bdmask, st_ref, d, reverse):
    c = q.shape[0]
    nlev = int(round(math.log2(c)))
    x = jnp.dot(a, wa, precision=HI, preferred_element_type=F32) + ba
    la = (jnp.minimum(x, 0.0) - jnp.log1p(jnp.exp(-jnp.abs(x)))) * (1.0 / GLA_TAU)
    hi, mid, lo = _split3(la)
    cum = _dot(amat, hi) + _dot(amat, mid) + _dot(amat, lo)
    last = cum[0:1] if reverse else cum[c - 1:c]
    qd = (q * jnp.exp(cum)).astype(MX)
    kd = (k * jnp.exp(last - cum)).astype(MX)
    st = st_ref[d]
    o = _dot_t(qd, st.astype(MX))
    scat = jnp.zeros((c, GLA_HEADS * c), F32)
    for lev in range(nlev + 1):
        if lev == 0:
            ql, kl = q, k
        else:
            qe, ke = _gla_level_exponents(la, cum, c >> lev, reverse)
            ql, kl = q * jnp.exp(qe), k * jnp.exp(ke)
        kst = jnp.concatenate([kl.astype(MX)] * GLA_HEADS, axis=0) * hmask
        scat = scat + _dot_t(ql.astype(MX), kst) * lmask[lev]
    vbd = jnp.concatenate([v.astype(MX)] * GLA_HEADS, axis=0) * vmask
    o = o + _dot(scat.astype(MX), vbd)
    upd = lax.dot_general(v.astype(MX), kd, (((0,), (0,)), ((), ())), preferred_element_type=F32)
    st_ref[d] = st * jnp.exp(last) + upd * bdmask
    return o


def _gla_body(qf_ref, kf_ref, vf_ref, af_ref, qb_ref, kb_ref, vb_ref, ab_ref,
              wa_ref, ba_ref, amf_ref, lmf_ref, amb_ref, lmb_ref, hm_ref, vm_ref, bd_ref,
              of_ref, ob_ref, st_ref):
    @pl.when(pl.program_id(1) == 0)
    def _():
        st_ref[...] = jnp.zeros_like(st_ref)

    hm, vm, bd = hm_ref[...], vm_ref[...], bd_ref[...]
    of_ref[...] = _gla_direction(qf_ref[...], kf_ref[...], vf_ref[...], af_ref[...],
                                 wa_ref[0], ba_ref[0], amf_ref[...], lmf_ref, hm, vm, bd,
                                 st_ref, 0, False)
    ob_ref[...] = _gla_direction(qb_ref[...], kb_ref[...], vb_ref[...], ab_ref[...],
                                 wa_ref[1], ba_ref[1], amb_ref[...], lmb_ref, hm, vm, bd,
                                 st_ref, 1, True)


def _gla(q, k, v, a, wa, ba, batch, ctx_len, seq_len):
    t = q.shape[0]
    c = GLA_CHUNK
    nc, nl = ctx_len // c, seq_len // c
    amf, lmf = _gla_consts(c, False)
    amb, lmb = _gla_consts(c, True)
    r = np.arange(GLA_HEADS * c)[:, None] // c
    hm = (r == np.arange(GLA_QK)[None, :] // GLA_DK).astype(np.float32)
    vm = (r == np.arange(GLA_VW)[None, :] // GLA_DV).astype(np.float32)
    bd = (np.arange(GLA_VW)[:, None] // GLA_DV
          == np.arange(GLA_QK)[None, :] // GLA_DK).astype(np.float32)

    def fwd(b, s):
        return jnp.where(s < nc, nc * b + s, batch * nc + nl * b + (s - nc))

    def bwd(b, s):
        return jnp.where(s < nc, nc * b + (nc - 1 - s), batch * nc + nl * b + (nl - 1 - (s - nc)))

    def chunk(n, fn):
        return pl.BlockSpec((c, n), lambda b, s: (fn(b, s), 0))

    consts = [jnp.asarray(amf, MX), jnp.asarray(lmf), jnp.asarray(amb, MX), jnp.asarray(lmb),
              jnp.asarray(hm, MX), jnp.asarray(vm, MX), jnp.asarray(bd)]
    ins = [q, k, v, a, q, k, v, a, wa, ba] + consts
    specs = ([chunk(GLA_QK, fwd), chunk(GLA_QK, fwd), chunk(GLA_VW, fwd), chunk(128, fwd),
              chunk(GLA_QK, bwd), chunk(GLA_QK, bwd), chunk(GLA_VW, bwd), chunk(128, bwd)]
             + [_full_spec(z) for z in ins[8:]])
    return pl.pallas_call(
        _gla_body,
        name="gla_scan",
        out_shape=[jax.ShapeDtypeStruct((t, GLA_VW), F32)] * 2,
        grid=(batch, nc + nl),
        in_specs=specs,
        out_specs=[chunk(GLA_VW, fwd), chunk(GLA_VW, bwd)],
        scratch_shapes=[pltpu.VMEM((2, GLA_VW, GLA_QK), F32)],
        compiler_params=_cparams(("parallel", "arbitrary")),
    )(*ins)


def _swa_heads(q, kcat, vcat, valid, sink_ref, h):
    lane_head = lax.broadcasted_iota(jnp.int32, (1, SWA_G * SWA_HD), 1) // SWA_HD
    acc = jnp.zeros((q.shape[0], SWA_G * SWA_HD), F32)
    for g in range(SWA_G):
        hm = lane_head == g
        s = _dot_t(jnp.where(hm, q, jnp.zeros_like(q)), kcat)
        if valid is not None:
            s = jnp.where(valid, s, -jnp.inf)
        sk = sink_ref[h * SWA_G + g]
        m = jnp.maximum(jnp.max(s, axis=-1, keepdims=True), sk)
        p = jnp.exp(s - m)
        den = jnp.sum(p, axis=-1, keepdims=True) + jnp.exp(sk - m)
        og = _dot(p.astype(MX), jnp.where(hm, vcat, jnp.zeros_like(vcat)))
        acc = acc + og * (1.0 / den)
    return acc


def _swa_latent_body(sink_ref, q_ref, kc_ref, vc_ref, kp_ref, k0_ref, kn_ref,
                     vp_ref, v0_ref, vn_ref, o_ref, *, nblk):
    n = pl.program_id(1)
    qt, blk = q_ref.shape[0], kp_ref.shape[0]
    nctx = kc_ref.shape[0]
    shape = (qt, nctx + qt + 2 * blk)
    qi = lax.broadcasted_iota(jnp.int32, shape, 0)
    col = lax.broadcasted_iota(jnp.int32, shape, 1)
    si = col - nctx
    kpos = n * qt - blk + si
    valid = (col < nctx) | ((jnp.abs(si - blk - qi) <= WINDOW) & (kpos >= 0) & (kpos < nblk * blk))
    w = SWA_G * SWA_HD
    for h in range(SWA_KV_HEADS):
        cols = slice(h * w, (h + 1) * w)
        kcat = jnp.concatenate([kc_ref[:, cols], kp_ref[:, cols], k0_ref[:, cols], kn_ref[:, cols]], axis=0)
        vcat = jnp.concatenate([vc_ref[:, cols], vp_ref[:, cols], v0_ref[:, cols], vn_ref[:, cols]], axis=0)
        o_ref[:, cols] = _swa_heads(q_ref[:, cols], kcat, vcat, valid, sink_ref, h).astype(o_ref.dtype)


def _swa_ctx_body(sink_ref, q_ref, kc_ref, vc_ref, o_ref):
    h = pl.program_id(1)
    o_ref[...] = _swa_heads(q_ref[...], kc_ref[...], vc_ref[...], None, sink_ref, h).astype(o_ref.dtype)


def _swa(sq, skr, svr, sink, batch, ctx_len, seq_len):
    t = sq.shape[0]
    blk = WINDOW
    w = SWA_G * SWA_HD
    nb = seq_len // blk
    cb = ctx_len // blk
    lat0 = batch * cb
    smem = pl.BlockSpec(memory_space=pltpu.SMEM)
    ctx_kv = pl.BlockSpec((ctx_len, w), lambda b, h, n: (b, h))

    qb = 2
    qt, nq = qb * blk, nb // qb

    def edge(off):
        return pl.BlockSpec((blk, SWA_W), lambda b, n: (lat0 + b * nb + jnp.clip(n * qb + off, 0, nb - 1), 0))

    mid = pl.BlockSpec((qt, SWA_W), lambda b, n: (lat0 // qb + b * nq + n, 0))
    ctx_all = pl.BlockSpec((ctx_len, SWA_W), lambda b, n: (b, 0))
    lat = pl.pallas_call(
        functools.partial(_swa_latent_body, nblk=nb),
        name="swa_latent",
        out_shape=jax.ShapeDtypeStruct((batch * seq_len, SWA_W), MX),
        grid=(batch, nq),
        in_specs=[smem, mid, ctx_all, ctx_all, edge(-1), mid, edge(qb), edge(-1), mid, edge(qb)],
        out_specs=pl.BlockSpec((qt, SWA_W), lambda b, n: (b * nq + n, 0)),
        compiler_params=_cparams(("parallel", "arbitrary")),
    )(sink, sq, skr, svr, skr, skr, skr, svr, svr, svr)
    ctx = pl.pallas_call(
        _swa_ctx_body,
        name="swa_ctx",
        out_shape=jax.ShapeDtypeStruct((batch * ctx_len, SWA_W), MX),
        grid=(batch, SWA_KV_HEADS, cb),
        in_specs=[smem, pl.BlockSpec((blk, w), lambda b, h, n: (b * cb + n, h)), ctx_kv, ctx_kv],
        out_specs=pl.BlockSpec((blk, w), lambda b, h, n: (b * cb + n, h)),
        compiler_params=_cparams(("parallel", "parallel", "arbitrary")),
    )(sink, sq, skr, svr)
    return jnp.concatenate([ctx, lat], axis=0)


def _residual(h, y, gpost, gate):
    return h + gate * _rms(y, gpost)


def _even_out_body(h_ref, of_ref, ob_ref, gg_ref, a_ref, gn_ref, w1_ref, w2_ref, gpost_ref,
                   gate_ref, o_ref):
    o = of_ref[...] + ob_ref[...]
    gn = gn_ref[...]
    parts = [_rms(o[:, j * GLA_DV:(j + 1) * GLA_DV], gn) for j in range(GLA_HEADS)]
    gl = jnp.concatenate(parts, axis=-1) * _silu(gg_ref[...])
    y = _dot(gl.astype(MX), w1_ref[...]) + _dot(a_ref[...], w2_ref[...])
    o_ref[...] = _residual(h_ref[...], y, gpost_ref[...], gate_ref[...])


def _even_out(h, o_f, o_b, gg, a_swa, gn, w1, w2, gpost, mod4, seg_fn):
    t, d = h.shape
    tm = ROW_TILE
    row = lambda n: pl.BlockSpec((tm, n), lambda i: (i, 0))
    return pl.pallas_call(
        _even_out_body,
        name="even_out",
        out_shape=jax.ShapeDtypeStruct((t, d), F32),
        grid=(t // tm,),
        in_specs=[row(d), row(GLA_VW), row(GLA_VW), row(GLA_VW), row(SWA_W), _full_spec(gn),
                  _full_spec(w1), _full_spec(w2), _full_spec(gpost), _mod_spec(seg_fn, 2, d)],
        out_specs=row(d),
        compiler_params=_cparams(("parallel",)),
    )(h, o_f, o_b, gg, a_swa, gn, w1, w2, gpost, mod4)


def _odd_out_body(h_ref, o_ref_in, w_ref, gpost_ref, gate_ref, o_ref):
    y = _dot(o_ref_in[...], w_ref[...])
    o_ref[...] = _residual(h_ref[...], y, gpost_ref[...], gate_ref[...])


def _odd_out(h, h_off, o, w, gpost, mod4, seg_fn):
    t, d = o.shape[0], h.shape[1]
    tm = ROW_TILE
    row = lambda n: pl.BlockSpec((tm, n), lambda i: (i, 0))
    return pl.pallas_call(
        _odd_out_body,
        name="odd_out",
        out_shape=jax.ShapeDtypeStruct((t, d), F32),
        grid=(t // tm,),
        in_specs=[pl.BlockSpec((tm, d), lambda i: (i + h_off, 0)), row(o.shape[1]),
                  _full_spec(w), _full_spec(gpost), _mod_spec(seg_fn, 2, d)],
        out_specs=row(d),
        compiler_params=_cparams(("parallel",)),
    )(h, o, w, gpost, mod4)


def _odd_in_body(h_ref, g_ref, sh_ref, sc_ref, w_ref, qn_ref, kvn_ref, cos_ref, sin_ref,
                 cq_ref, ckv_ref, kr_ref):
    u = _modulate(h_ref[...], g_ref[...], sh_ref[...], sc_ref[...]).astype(MX)
    cq_ref[...] = _rms(_dot(u, w_ref[:, 0:MLA_Q_RANK]), qn_ref[...]).astype(cq_ref.dtype)
    c1 = MLA_Q_RANK + MLA_KV_RANK
    ckv_ref[...] = _rms(_dot(u, w_ref[:, MLA_Q_RANK:c1]), kvn_ref[...]).astype(ckv_ref.dtype)
    kr = _dot(u, w_ref[:, c1:c1 + 128])
    kr_ref[...] = _rope(kr, cos_ref[...], sin_ref[...]).astype(kr_ref.dtype)


def _odd_in(h, g_pre, mod4, w_cat, qn, kvn, cos_t, sin_t, seg_fn, pos_fn):
    t, d = h.shape
    tm = ROW_TILE
    row = lambda n: pl.BlockSpec((tm, n), lambda i: (i, 0))
    tab = pl.BlockSpec((tm, 128), lambda i: (pos_fn(i), 0))
    return pl.pallas_call(
        _odd_in_body,
        name="odd_in",
        out_shape=[jax.ShapeDtypeStruct((t, MLA_Q_RANK), MX),
                   jax.ShapeDtypeStruct((t, MLA_KV_RANK), MX),
                   jax.ShapeDtypeStruct((t, 128), MX)],
        grid=(t // tm,),
        in_specs=[row(d), _full_spec(g_pre), _mod_spec(seg_fn, 0, d), _mod_spec(seg_fn, 1, d),
                  _full_spec(w_cat), _full_spec(qn), _full_spec(kvn), tab, tab],
        out_specs=[row(MLA_Q_RANK), row(MLA_KV_RANK), row(128)],
        compiler_params=_cparams(("parallel",)),
    )(h, g_pre, mod4, mod4, w_cat, qn, kvn, cos_t, sin_t)


def _q_up_body(cq_ref, w_ref, cos_ref, sin_ref, o_ref):
    cq, cos, sin = cq_ref[...], cos_ref[...], sin_ref[...]
    scale = MLA_SCALE * math.log2(math.e)
    for h in range(MLA_HEADS):
        c0 = h * MLA_QK_PAD
        z = _dot(cq, w_ref[:, c0:c0 + MLA_QK_PAD])
        o_ref[:, c0:c0 + MLA_NOPE] = (z[:, :MLA_NOPE] * scale).astype(o_ref.dtype)
        zr = _rope(z[:, MLA_NOPE:], cos, sin)
        o_ref[:, c0 + MLA_NOPE:c0 + MLA_QK_PAD] = (zr * scale).astype(o_ref.dtype)


def _q_up(cq, row_off, nrows, w_pad, cos_t, sin_t, pos_fn):
    tm = ROW_TILE
    width = MLA_HEADS * MLA_QK_PAD
    return pl.pallas_call(
        _q_up_body,
        name="mla_q_up",
        out_shape=jax.ShapeDtypeStruct((nrows, width), MX),
        grid=(nrows // tm,),
        in_specs=[pl.BlockSpec((tm, MLA_Q_RANK), lambda i: (i + row_off, 0)),
                  _full_spec(w_pad),
                  pl.BlockSpec((tm, 128), lambda i: (pos_fn(i + row_off), 0)),
                  pl.BlockSpec((tm, 128), lambda i: (pos_fn(i + row_off), 0))],
        out_specs=pl.BlockSpec((tm, width), lambda i: (i, 0)),
        compiler_params=_cparams(("parallel",)),
    )(cq, w_pad, cos_t, sin_t)


def _kv_up_body(ckv_ref, kr_ref, w_ref, k_ref, v_ref):
    ckv, kr = ckv_ref[...], kr_ref[...]
    for h in range(MLA_HEADS):
        c0 = h * (MLA_NOPE + MLA_V)
        z = _dot(ckv, w_ref[:, c0:c0 + MLA_NOPE + MLA_V])
        k0 = h * MLA_QK_PAD
        k_ref[:, k0:k0 + MLA_NOPE] = z[:, :MLA_NOPE].astype(k_ref.dtype)
        k_ref[:, k0 + MLA_NOPE:k0 + MLA_QK_PAD] = kr
        v0 = h * MLA_VW
        v_ref[:, v0:v0 + MLA_V] = z[:, MLA_NOPE:].astype(v_ref.dtype)
        v_ref[:, v0 + MLA_V:v0 + MLA_VW] = jnp.ones((z.shape[0], MLA_VW - MLA_V), v_ref.dtype)


def _kv_up(ckv, kr, row_off, nrows, tm, w_ukv):
    return pl.pallas_call(
        _kv_up_body,
        name="mla_kv_up",
        out_shape=[jax.ShapeDtypeStruct((nrows, MLA_HEADS * MLA_QK_PAD), MX),
                   jax.ShapeDtypeStruct((nrows, MLA_HEADS * MLA_VW), MX)],
        grid=(nrows // tm,),
        in_specs=[pl.BlockSpec((tm, MLA_KV_RANK), lambda i: (i + row_off, 0)),
                  pl.BlockSpec((tm, 128), lambda i: (i + row_off, 0)),
                  _full_spec(w_ukv)],
        out_specs=[pl.BlockSpec((tm, MLA_HEADS * MLA_QK_PAD), lambda i: (i, 0)),
                   pl.BlockSpec((tm, MLA_HEADS * MLA_VW), lambda i: (i, 0))],
        compiler_params=_cparams(("parallel",)),
    )(ckv, kr, w_ukv)


def _mla_attn_body(q_ref, kc_ref, vc_ref, k_ref, v_ref, o_ref, *, tk, unroll):
    q = q_ref[...]
    s = _dot_t(q, kc_ref[...])
    m = jnp.max(s, axis=-1, keepdims=True)
    acc = _dot(jnp.exp2((s - m).astype(MX)), vc_ref[...])

    def step(j, carry):
        m, acc = carry
        start = pl.multiple_of(j * tk, tk)
        s = _dot_t(q, k_ref[pl.ds(start, tk), :])
        mn = jnp.maximum(m, jnp.max(s, axis=-1, keepdims=True))
        p = jnp.exp2((s - mn).astype(MX))
        acc = jnp.exp2(m - mn) * acc + _dot(p, v_ref[pl.ds(start, tk), :])
        return mn, acc

    m, acc = lax.fori_loop(0, k_ref.shape[0] // tk, step, (m, acc), unroll=unroll)
    o_ref[...] = (acc[:, :MLA_V] * (1.0 / acc[:, MLA_V:MLA_V + 1])).astype(o_ref.dtype)


def _mla_attn(q, k_ctx, v_ctx, k_lat, v_lat, batch, ctx_len, seq_len, tq=1024, tk=1024, unroll=8):
    nq = seq_len // tq
    return pl.pallas_call(
        functools.partial(_mla_attn_body, tk=tk, unroll=unroll),
        name="mla_attn",
        out_shape=jax.ShapeDtypeStruct((batch * seq_len, MLA_HEADS * MLA_V), MX),
        grid=(batch, MLA_HEADS, nq),
        in_specs=[pl.BlockSpec((tq, MLA_QK_PAD), lambda b, h, i: (b * nq + i, h)),
                  pl.BlockSpec((ctx_len, MLA_QK_PAD), lambda b, h, i: (b, h)),
                  pl.BlockSpec((ctx_len, MLA_VW), lambda b, h, i: (b, h)),
                  pl.BlockSpec((seq_len, MLA_QK_PAD), lambda b, h, i: (b, h)),
                  pl.BlockSpec((seq_len, MLA_VW), lambda b, h, i: (b, h))],
        out_specs=pl.BlockSpec((tq, MLA_V), lambda b, h, i: (b * nq + i, h)),
        compiler_params=_cparams(("parallel", "parallel", "arbitrary")),
    )(q, k_ctx, v_ctx, k_lat, v_lat)


def _route(scores, sel):
    e, t = sel.shape
    neg = -jnp.inf
    x3 = sel.reshape(N_GROUPS, GROUP_SIZE, t)
    pos = lax.broadcasted_iota(jnp.int32, x3.shape, 1)
    m1 = jnp.max(x3, axis=1, keepdims=True)
    i1 = jnp.min(jnp.where(x3 == m1, pos, GROUP_SIZE), axis=1, keepdims=True)
    m2 = jnp.max(jnp.where(pos == i1, neg, x3), axis=1, keepdims=True)
    gs = m1 + m2
    gid = lax.broadcasted_iota(jnp.int32, gs.shape, 0)
    beaten = jnp.zeros(gs.shape, jnp.int32)
    for g in range(N_GROUPS):
        other = gs[g:g + 1]
        beaten = beaten + jnp.where((other > gs) | ((other == gs) & (g < gid)), 1, 0)
    cur = jnp.where(beaten < TOPK_GROUPS, x3, neg).reshape(e, t)
    row = lax.broadcasted_iota(jnp.int32, (e, t), 0)
    chosen = jnp.zeros((e, t), F32)
    for _ in range(TOP_K):
        m = jnp.max(cur, axis=0, keepdims=True)
        i = jnp.min(jnp.where(cur == m, row, e), axis=0, keepdims=True)
        hit = row == i
        chosen = jnp.where(hit, 1.0, chosen)
        cur = jnp.where(hit, neg, cur)
    w = chosen * scores
    return chosen, w / jnp.sum(w, axis=0, keepdims=True) * ROUTED_SCALE


def _ffn_pre_body(h_ref, g_ref, sh_ref, sc_ref, wr_ref, br_ref, us_ref, ls_ref,
                  v_ref, dst_ref, wt_ref, cpad_ref):
    vl = _modulate(h_ref[...], g_ref[...], sh_ref[...], sc_ref[...])
    v_ref[...] = vl.astype(v_ref.dtype)
    logits = lax.dot_general(wr_ref[...], vl, (((1,), (1,)), ((), ())), precision=HI,
                             preferred_element_type=F32)
    scores = _sigmoid(logits)
    chosen, w = _route(scores, scores + br_ref[...])
    rank = _dot(chosen.astype(MX), us_ref[...])
    cnt = jnp.sum(chosen, axis=1, keepdims=True)
    cpad = jnp.ceil(cnt * (1.0 / MOE_PIECE)) * MOE_PIECE
    cpad_b = jnp.broadcast_to(cpad, (cpad.shape[0], 128))
    loff = _dot(ls_ref[...], cpad_b.astype(MX))
    dst_ref[...] = jnp.where(chosen > 0.0, loff[:, 0:1] + rank, -1.0).astype(jnp.int32)
    wt_ref[...] = w
    cpad_ref[...] = cpad_b.astype(jnp.int32)


def _ffn_pre(h, g_pre, mod4, w_router, b_router, seg_fn):
    t, d = h.shape
    tm = MOE_TB
    nb = t // tm
    row = lambda n: pl.BlockSpec((tm, n), lambda i: (i, 0))
    blk = lambda n: pl.BlockSpec((None, N_EXPERTS, n), lambda i: (i, 0, 0))
    us = jnp.asarray(np.triu(np.ones((tm, tm), np.float32), 1), MX)
    ls = jnp.asarray(np.tril(np.ones((N_EXPERTS, N_EXPERTS), np.float32), -1), MX)
    wr_t = w_router.T
    br_t = b_router.reshape(N_EXPERTS, 1)
    return pl.pallas_call(
        _ffn_pre_body,
        name="ffn_pre_router",
        out_shape=[jax.ShapeDtypeStruct((t, d), MX),
                   jax.ShapeDtypeStruct((nb, N_EXPERTS, tm), jnp.int32),
                   jax.ShapeDtypeStruct((nb, N_EXPERTS, tm), F32),
                   jax.ShapeDtypeStruct((nb, N_EXPERTS, 128), jnp.int32)],
        grid=(nb,),
        in_specs=[row(d), _full_spec(g_pre), _mod_spec(seg_fn, 3, d), _mod_spec(seg_fn, 4, d),
                  _full_spec(wr_t), _full_spec(br_t), _full_spec(us), _full_spec(ls)],
        out_specs=[row(d), blk(tm), blk(tm), blk(128)],
        compiler_params=_cparams(("parallel",)),
    )(h, g_pre, mod4, mod4, wr_t, br_t, us, ls)


MOE_TB = 512
MOE_PIECE = 16
MOE_CAP = TOP_K * MOE_TB + N_EXPERTS * MOE_PIECE
MOE_NPIECE = MOE_CAP // MOE_PIECE
MOE_CHUNK = 1024
MOE_PPC = MOE_CHUNK // MOE_PIECE
MOE_RB = 512


def _dispatch_plan(cpad, t):
    nb = cpad.shape[0]
    loff = jnp.cumsum(cpad, axis=1) - cpad
    tot = jnp.sum(cpad, axis=0)
    reg = (tot + MOE_RB - 1) // MOE_RB * MOE_RB
    gend = jnp.cumsum(reg)
    gbase = gend - reg
    goff = gbase[None] + jnp.cumsum(cpad, axis=0) - cpad
    prow = jnp.arange(MOE_NPIECE, dtype=jnp.int32) * MOE_PIECE
    pexp = jnp.sum((prow[None, :, None] >= (loff + cpad)[:, None, :]).astype(jnp.int32), axis=2)
    pexp = jnp.minimum(pexp, N_EXPERTS - 1)
    own = pexp[:, :, None] == jnp.arange(N_EXPERTS, dtype=jnp.int32)[None, None, :]
    gdst = jnp.sum(jnp.where(own, (goff - loff)[:, None, :], 0), axis=2) + prow[None]
    ng = t * TOP_K + nb * N_EXPERTS * (MOE_PIECE - 1) + N_EXPERTS * (MOE_RB - 1)
    ng = -(-ng // (2 * MOE_RB)) * (2 * MOE_RB)
    start = jnp.arange(ng // MOE_RB, dtype=jnp.int32) * MOE_RB
    blk_e = jnp.sum((start[:, None] >= gend[None, :]).astype(jnp.int32), axis=1)
    blk_e = jnp.minimum(blk_e, N_EXPERTS - 1)
    active = (start < (gbase + tot)[blk_e]).astype(jnp.int32)
    i32 = lambda z: z.astype(jnp.int32)
    return dict(pexp=i32(pexp), gdst=i32(gdst), ntot=i32(jnp.sum(cpad, axis=1) // MOE_PIECE),
                toff=i32(gbase + tot), tnp=i32((reg - tot) // MOE_PIECE),
                nused=i32(gend[-1:] // MOE_RB), blk_e=blk_e, active=active, ng=ng)


def _onehot_rows(pexp_ref, dst_ref, blk, c, val_of, out_ref):
    rows = lax.broadcasted_iota(jnp.int32, (MOE_PIECE, MOE_TB), 0)
    for p in range(MOE_PPC):
        e = pexp_ref[blk, c * MOE_PPC + p]
        hit = dst_ref[pl.ds(e, 1), :] == rows + (c * MOE_CHUNK + p * MOE_PIECE)
        out_ref[p * MOE_PIECE:(p + 1) * MOE_PIECE, :] = jnp.where(hit, val_of(e), 0.0).astype(out_ref.dtype)


def _repeat(n, fn):
    def body(_, c):
        fn()
        return c

    lax.fori_loop(0, n, body, 0)


def _start_chunk_pieces(n, c, start_fn):
    @pl.when(n >= (c + 1) * MOE_PPC)
    def _():
        for p in range(MOE_PPC):
            start_fn(c * MOE_PPC + p, p % 2)

    @pl.when((n > c * MOE_PPC) & (n < (c + 1) * MOE_PPC))
    def _():
        def body(q, carry):
            start_fn(q, 0)
            return carry

        lax.fori_loop(c * MOE_PPC, n, body, 0)


def _wait_pieces(n, wait_chunk, wait_piece):
    _repeat(n // MOE_PPC, wait_chunk)
    _repeat(n % MOE_PPC, wait_piece)


def _moe_sort_body(pexp_ref, gdst_ref, ntot_ref, toff_ref, tnp_ref, nused_ref, x_ref, dst_ref,
                   xg_ref, xs_ref, pi_ref, zb_ref, sem, zsem):
    i, nb = pl.program_id(0), pl.num_programs(0)
    slot = i % 2

    def piece_copy(s, lo, go):
        return pltpu.make_async_copy(xs_ref.at[s, pl.ds(lo, MOE_PIECE)],
                                     xg_ref.at[pl.ds(go, MOE_PIECE)], sem.at[s])

    def zero_copy(go):
        return pltpu.make_async_copy(zb_ref.at[pl.ds(0, MOE_PIECE)], xg_ref.at[pl.ds(go, MOE_PIECE)],
                                     zsem.at[0])

    def zero_block(b):
        return pltpu.make_async_copy(zb_ref, xg_ref.at[pl.ds(pl.multiple_of(b * MOE_RB, MOE_RB), MOE_RB)],
                                     zsem.at[1])

    def wait_slot(s, n):
        def chunk():
            pltpu.make_async_copy(xs_ref.at[s, pl.ds(0, MOE_CHUNK)], xg_ref.at[pl.ds(0, MOE_CHUNK)],
                                  sem.at[s]).wait()

        _wait_pieces(n, chunk, lambda: piece_copy(s, 0, 0).wait())

    def start(q, priority):
        piece_copy(slot, pl.multiple_of(q * MOE_PIECE, MOE_PIECE),
                   pl.multiple_of(gdst_ref[i, q], MOE_PIECE)).start(priority=priority)

    @pl.when(i >= 2)
    def _():
        wait_slot(slot, ntot_ref[i - 2])

    x = x_ref[...]
    for c in range(MOE_CAP // MOE_CHUNK):
        _onehot_rows(pexp_ref, dst_ref, i, c, lambda e: 1.0, pi_ref.at[c])
        xs_ref[slot, c * MOE_CHUNK:(c + 1) * MOE_CHUNK, :] = _dot(pi_ref[c], x).astype(xs_ref.dtype)
    for c in range(MOE_CAP // MOE_CHUNK):
        _start_chunk_pieces(ntot_ref[i], c, start)

    @pl.when(i == nb - 1)
    def _():
        zb_ref[...] = jnp.zeros_like(zb_ref)

        def tail(e, n):
            def piece(p, c):
                zero_copy(pl.multiple_of(toff_ref[e] + p * MOE_PIECE, MOE_PIECE)).start()
                return c

            lax.fori_loop(0, tnp_ref[e], piece, 0)
            return n + tnp_ref[e]

        nz = lax.fori_loop(0, N_EXPERTS, tail, 0)
        nblk = xg_ref.shape[0] // MOE_RB

        def unused(b, c):
            zero_block(b).start()
            return c

        lax.fori_loop(nused_ref[0], nblk, unused, 0)
        _repeat(nz, lambda: zero_copy(0).wait())
        _repeat(nblk - nused_ref[0], lambda: zero_block(0).wait())
        wait_slot(slot, ntot_ref[i])

    @pl.when((i == nb - 1) & (i >= 1))
    def _():
        wait_slot(1 - slot, ntot_ref[i - 1])


def _moe_expert_body(be_ref, act_ref, x_ref, *refs):
    del be_ref
    w_refs, o_ref = refs[:-1], refs[-1]
    i = pl.program_id(0)
    live = (act_ref[2 * i] != 0) | (act_ref[2 * i + 1] != 0)

    @pl.when(live)
    def _():
        for r in range(2):
            wg_ref, wu_ref, wd_ref = w_refs[3 * r:3 * r + 3]
            rows = slice(r * MOE_RB, (r + 1) * MOE_RB)
            x = x_ref[rows, :]
            hid = _silu(_dot(x, wg_ref[...].astype(MX))) * _dot(x, wu_ref[...].astype(MX))
            o_ref[rows, :] = _dot(hid.astype(MX), wd_ref[...].astype(MX)).astype(o_ref.dtype)

    @pl.when(jnp.logical_not(live))
    def _():
        o_ref[...] = jnp.zeros_like(o_ref)


def _moe_combine_body(pexp_ref, gdst_ref, ntot_ref, yg_ref, dst_ref, wt_ref, x_ref,
                      sg_ref, su_ref, sd_ref, h_ref, gpost_ref, gate_ref, o_ref,
                      ys_ref, pw_ref, sem):
    i, nb = pl.program_id(0), pl.num_programs(0)
    slot = i % 2

    def piece_copy(s, lo, go):
        return pltpu.make_async_copy(yg_ref.at[pl.ds(go, MOE_PIECE)],
                                     ys_ref.at[s, pl.ds(lo, MOE_PIECE)], sem.at[s])

    def fetch(blk, s):
        n = ntot_ref[blk]

        def start(q, priority):
            piece_copy(s, pl.multiple_of(q * MOE_PIECE, MOE_PIECE),
                       pl.multiple_of(gdst_ref[blk, q], MOE_PIECE)).start(priority=priority)

        for c in range(MOE_CAP // MOE_CHUNK):
            _start_chunk_pieces(n, c, start)

        def clear(q, carry):
            ys_ref[s, pl.ds(pl.multiple_of(q * MOE_PIECE, MOE_PIECE), MOE_PIECE), :] = jnp.zeros(
                (MOE_PIECE, ys_ref.shape[2]), ys_ref.dtype)
            return carry

        lax.fori_loop(n, MOE_NPIECE, clear, 0)

    @pl.when(i == 0)
    def _():
        fetch(0, 0)

    @pl.when(i + 1 < nb)
    def _():
        fetch(i + 1, 1 - slot)

    x = x_ref[...]
    hid = _silu(_dot(x, sg_ref[...].astype(MX))) * _dot(x, su_ref[...].astype(MX))
    acc = _dot(hid.astype(MX), sd_ref[...].astype(MX))

    def wait_chunk():
        pltpu.make_async_copy(yg_ref.at[pl.ds(0, MOE_CHUNK)], ys_ref.at[slot, pl.ds(0, MOE_CHUNK)],
                              sem.at[slot]).wait()

    _wait_pieces(ntot_ref[i], wait_chunk, lambda: piece_copy(slot, 0, 0).wait())
    for c in range(MOE_CAP // MOE_CHUNK):
        _onehot_rows(pexp_ref, dst_ref, i, c, lambda e: wt_ref[pl.ds(e, 1), :], pw_ref.at[c])
        acc = acc + lax.dot_general(
            pw_ref[c], ys_ref[slot, c * MOE_CHUNK:(c + 1) * MOE_CHUNK, :],
            (((0,), (0,)), ((), ())), preferred_element_type=F32)
    o_ref[...] = _residual(h_ref[...], acc, gpost_ref[...], gate_ref[...])


def _moe(x, dst_t, w_t, cpad, layer, wg, wu, wd, sg, su, sd, h, gpost, mod4, seg_fn):
    t, d = x.shape
    nb = t // MOE_TB
    plan = _dispatch_plan(cpad, t)
    ng = plan['ng']
    tables = (plan['pexp'], plan['gdst'], plan['ntot'])
    etab = pl.BlockSpec((None, N_EXPERTS, MOE_TB), lambda i, *_: (i, 0, 0))
    xg = pl.pallas_call(
        _moe_sort_body,
        name="moe_sort",
        out_shape=jax.ShapeDtypeStruct((ng, d), MX),
        grid_spec=pltpu.PrefetchScalarGridSpec(
            num_scalar_prefetch=6, grid=(nb,),
            in_specs=[pl.BlockSpec((MOE_TB, d), lambda i, *_: (i, 0)), etab],
            out_specs=pl.BlockSpec(memory_space=pl.ANY),
            scratch_shapes=[pltpu.VMEM((2, MOE_CAP, d), MX),
                            pltpu.VMEM((MOE_CAP // MOE_CHUNK, MOE_CHUNK, MOE_TB), MX),
                            pltpu.VMEM((MOE_RB, d), MX),
                            pltpu.SemaphoreType.DMA((2,)), pltpu.SemaphoreType.DMA((2,))]),
        compiler_params=_cparams(("arbitrary",)),
    )(*tables, plan['toff'], plan['tnp'], plan['nused'], x, dst_t)
    yg = pl.pallas_call(
        _moe_expert_body,
        name="moe_experts",
        out_shape=jax.ShapeDtypeStruct((ng, d), MX),
        grid_spec=pltpu.PrefetchScalarGridSpec(
            num_scalar_prefetch=2, grid=(ng // (2 * MOE_RB),),
            in_specs=[pl.BlockSpec((2 * MOE_RB, d), lambda i, be, act: (i, 0))] + [
                pl.BlockSpec((None, None) + w.shape[2:],
                             functools.partial(lambda i, be, act, r: (layer, be[2 * i + r], 0, 0), r=r))
                for r in range(2) for w in (wg, wu, wd)],
            out_specs=pl.BlockSpec((2 * MOE_RB, d), lambda i, be, act: (i, 0))),
        compiler_params=_cparams(("arbitrary",)),
    )(plan['blk_e'], plan['active'], xg, wg, wu, wd, wg, wu, wd)
    row = lambda n: pl.BlockSpec((MOE_TB, n), lambda i, *_: (i, 0))
    const = lambda arr: pl.BlockSpec(arr.shape, lambda i, *_: (0,) * arr.ndim)
    return pl.pallas_call(
        _moe_combine_body,
        name="moe_combine",
        out_shape=jax.ShapeDtypeStruct((t, d), F32),
        grid_spec=pltpu.PrefetchScalarGridSpec(
            num_scalar_prefetch=3, grid=(nb,),
            in_specs=[pl.BlockSpec(memory_space=pl.ANY), etab, etab, row(d),
                      const(sg), const(su), const(sd), row(d), const(gpost),
                      pl.BlockSpec((None, None, 1, d), lambda i, *_: (seg_fn(i), 5, 0, 0))],
            out_specs=row(d),
            scratch_shapes=[pltpu.VMEM((2, MOE_CAP, d), MX),
                            pltpu.VMEM((MOE_CAP // MOE_CHUNK, MOE_CHUNK, MOE_TB), MX),
                            pltpu.SemaphoreType.DMA((2,))]),
        compiler_params=_cparams(("arbitrary",)),
    )(*tables, yg, dst_t, w_t, x, sg, su, sd, h, gpost, mod4)


def _rope_tables(seq_len, n_ident):
    rows = seq_len // GRID_W
    row = jnp.repeat(jnp.arange(rows, dtype=F32), GRID_W)
    col = jnp.tile(jnp.arange(GRID_W, dtype=F32), rows)
    axis_dim = 32
    inv_freq = ROPE_THETA ** (-jnp.arange(0, axis_dim, 2, dtype=F32) / axis_dim)
    ang = jnp.concatenate([row[:, None] * inv_freq, col[:, None] * inv_freq], axis=-1)
    cos, sin = jnp.cos(ang), jnp.sin(ang)
    cos64 = jnp.concatenate([cos, cos], axis=-1)
    sin64 = jnp.concatenate([-sin, sin], axis=-1)
    cos64 = jnp.concatenate([jnp.ones((n_ident, 64), F32), cos64], axis=0)
    sin64 = jnp.concatenate([jnp.zeros((n_ident, 64), F32), sin64], axis=0)
    return cos64, sin64


def _even_weights(w_in):
    d = w_in.shape[0]
    cuts = np.cumsum([GLA_QK, GLA_QK, GLA_VW, GLA_VW, GLA_RANK, GLA_RANK, SWA_W,
                      SWA_KV_HEADS * SWA_HD])
    q, k, v, g, af, ab, sq, sk, sv = jnp.split(w_in, [int(c) for c in cuts], axis=1)
    a = jnp.concatenate([af, ab, jnp.zeros((d, 128 - 2 * GLA_RANK), w_in.dtype)], axis=1)

    def rep(wkv):
        return jnp.tile(wkv.reshape(d, SWA_KV_HEADS, 1, SWA_HD), (1, 1, SWA_G, 1)).reshape(d, SWA_W)

    return jnp.concatenate([q, k, v, g, a, sq, rep(sk), rep(sv)], axis=1).astype(MX)


def _layer_tail(h, l, mod4, seg_fn, p):
    x, dst_t, w_t, cpad = _ffn_pre(h, p['g_ffn_pre'][l][None], mod4, p['w_router'][l], p['b_router'][l], seg_fn)
    return _moe(x, dst_t, w_t, cpad[:, :, 0], l, p['w_exp_gate'], p['w_exp_up'], p['w_exp_down'],
                p['w_sh_gate'][l], p['w_sh_up'][l], p['w_sh_down'][l],
                h, p['g_ffn_post'][l][None], mod4, seg_fn)


def kernel(x, c, ctx, c_ctx, w_mod, b_mod, g_mix_pre, g_mix_post, g_ffn_pre, g_ffn_post, w_in_e, gla_wa2, gla_ba, gla_norm, swa_sink, w_out_e, w_in_o, mla_q_norm, mla_kv_norm, w_uq, w_ukv, w_out_o, w_router, b_router, w_exp_gate, w_exp_up, w_exp_down, w_sh_gate, w_sh_up, w_sh_down):
    p = dict(g_ffn_pre=g_ffn_pre, g_ffn_post=g_ffn_post, w_router=w_router, b_router=b_router,
             w_exp_gate=w_exp_gate, w_exp_up=w_exp_up, w_exp_down=w_exp_down,
             w_sh_gate=w_sh_gate, w_sh_up=w_sh_up, w_sh_down=w_sh_down)
    batch, seq_len, d = x.shape
    ctx_len = ctx.shape[1]
    depth = w_mod.shape[0]
    assert depth == 2 and batch * ctx_len == ROW_TILE and seq_len % ROW_TILE == 0
    lat_blocks = seq_len // ROW_TILE

    cvec = jnp.concatenate([c, c_ctx[None], jnp.zeros((8 - batch - 1, d), F32)], axis=0)
    mod = _mod_vectors(cvec, w_mod, b_mod)
    h = jnp.concatenate([ctx.reshape(batch * ctx_len, d), x.reshape(batch * seq_len, d)], axis=0)

    def seg_fns(tile):
        cb, lb = batch * ctx_len // tile, seq_len // tile
        return (lambda i: jnp.where(i < cb, batch, (i - cb) // lb)), (lambda i: i // lb)

    seg_all, seg_lat = seg_fns(ROW_TILE)
    moe_all, moe_lat = seg_fns(MOE_TB)
    pos_all = lambda i: jnp.where(i == 0, 0, 1 + (i - 1) % lat_blocks)
    cos64, sin64 = _rope_tables(seq_len, ROW_TILE)

    mod4 = mod[0].reshape(8, 6, 1, d)
    q, k, v, gg, a, sq, skr, svr = _even_in(
        h, g_mix_pre[0][None], mod4, _even_weights(w_in_e[0]),
        jnp.tile(cos64, (1, 2)), jnp.tile(sin64, (1, 2)), seg_all, pos_all)
    wa = jnp.zeros((2, 128, GLA_QK), F32)
    wa = wa.at[0, :GLA_RANK].set(gla_wa2[0, 0]).at[1, GLA_RANK:2 * GLA_RANK].set(gla_wa2[0, 1])
    o_f, o_b = _gla(q, k, v, a, wa, gla_ba[0][:, None, :], batch, ctx_len, seq_len)
    a_swa = _swa(sq, skr, svr, swa_sink[0], batch, ctx_len, seq_len)
    w_out = w_out_e[0].astype(MX)
    h = _even_out(h, o_f, o_b, gg, a_swa, gla_norm[0][None], w_out[:GLA_VW], w_out[GLA_VW:],
                  g_mix_post[0][None], mod4, seg_all)
    h = _layer_tail(h, 0, mod4, moe_all, p)

    mod4 = mod[1].reshape(8, 6, 1, d)
    ones = jnp.ones_like(cos64)
    cos_h = jnp.concatenate([cos64, ones], axis=1)
    sin_h = jnp.concatenate([sin64, 0 * ones], axis=1)
    w_in = jnp.concatenate([w_in_o[0], jnp.zeros((d, 128 - MLA_ROPE), F32)], axis=1).astype(MX)
    cq, ckv, kr = _odd_in(h, g_mix_pre[1][None], mod4, w_in, mla_q_norm[0][None],
                          mla_kv_norm[0][None], cos_h, sin_h, seg_all, pos_all)
    w_q = w_uq[0].reshape(MLA_Q_RANK, MLA_HEADS, MLA_NOPE + MLA_ROPE)
    w_q = jnp.pad(w_q, ((0, 0), (0, 0), (0, MLA_QK_PAD - MLA_NOPE - MLA_ROPE)))
    w_q = w_q.reshape(MLA_Q_RANK, MLA_HEADS * MLA_QK_PAD).astype(MX)
    n_lat = batch * seq_len
    qh = _q_up(cq, 1, n_lat, w_q, cos_h, sin_h, pos_all)
    w_kv = w_ukv[0].astype(MX)
    k_lat, v_lat = _kv_up(ckv, kr, 1, n_lat, ROW_TILE, w_kv)
    k_ctx, v_ctx = _kv_up(ckv, kr, 0, batch * ctx_len, ctx_len, w_kv)
    o = _mla_attn(qh, k_ctx, v_ctx, k_lat, v_lat, batch, ctx_len, seq_len)
    hl = _odd_out(h, 1, o, w_out_o[0].astype(MX), g_mix_post[1][None], mod4, seg_lat)
    hl = _layer_tail(hl, 1, mod4, moe_lat, p)
    return hl.reshape(batch, seq_len, d)
```
